```python
import math
import jax, jax.numpy as jnp
from jax import lax
import numpy as np

D_MODEL = 1024
BATCH = 8
SEQ = 2048
DEPTH = 2
DEC_BATCH = 128
DEC_SEQ = 4
PAST_LEN = 16384
PAGE_SIZE = 128

D_RNN = D_MODEL
N_LRU_BLOCKS = 8
LRU_BLOCK = D_RNN // N_LRU_BLOCKS
CONV_LRU = 4
LRU_C = 8.0
N_HEADS = 8
N_KV = 2
HEAD_DIM = D_MODEL // N_HEADS
WINDOW = 128
N_BUCKETS = 32
MAX_EXACT = N_BUCKETS // 2
MAX_DISTANCE = 128
D_FF = 4 * D_MODEL
CONV_FF = 3
EPS = 1e-6

Q_COLS = N_HEADS * HEAD_DIM
KV_COLS = N_KV * HEAD_DIM
IN_COLS = D_RNN + Q_COLS + 2 * KV_COLS + 2 * D_MODEL

kernel_name = "hawk_swa_sink_convffn_step"


def _rmsnorm(x, g):
    x32 = x.astype(jnp.float32)
    y = x32 * lax.rsqrt(jnp.mean(x32 * x32, axis=-1, keepdims=True) + EPS) * g.astype(jnp.float32)
    return y.astype(x.dtype)


def _causal_dwconv(x, prev, w, b):
    width = w.shape[0]
    xx = jnp.concatenate([prev.astype(x.dtype), x], axis=1)
    y = lax.conv_general_dilated(xx, w[:, None, :].astype(x.dtype), window_strides=(1,), padding='VALID',
                                 dimension_numbers=('NWC', 'WIO', 'NWC'), feature_group_count=x.shape[-1])
    return y + b.astype(x.dtype), xx[:, -(width - 1):]


def _rg_lru(x, h0, wr, br, wi, bi, lam, pos0):
    B, T, _ = x.shape
    x32 = x.astype(jnp.float32)
    xb = x32.reshape(B, T, N_LRU_BLOCKS, LRU_BLOCK)
    r = jax.nn.sigmoid(jnp.einsum('btnc,ncd->btnd', xb, wr.astype(jnp.float32)).reshape(B, T, D_RNN) + br.astype(jnp.float32))
    i = jax.nn.sigmoid(jnp.einsum('btnc,ncd->btnd', xb, wi.astype(jnp.float32)).reshape(B, T, D_RNN) + bi.astype(jnp.float32))
    log_a = -LRU_C * r * jax.nn.softplus(-lam.astype(jnp.float32))
    a = jnp.exp(log_a)
    mult = jnp.sqrt(jnp.maximum(-jnp.expm1(2.0 * log_a), 0.0))
    pos = pos0 + jnp.arange(T)
    mult = jnp.where((pos == 0)[None, :, None], 1.0, mult)
    bterm = mult * (i * x32)
    bterm = jnp.concatenate([bterm[:, :1] + a[:, :1] * h0.astype(jnp.float32)[:, None], bterm[:, 1:]], axis=1)

    def combine(left, right):
        a1, b1 = left
        a2, b2 = right
        return a1 * a2, a2 * b1 + b2

    _, h = lax.associative_scan(combine, (a, bterm), axis=1)
    return h.astype(x.dtype), h[:, -1]


def _t5_bucket(d):
    n = jnp.maximum(d, 0)
    nf = jnp.maximum(n, 1).astype(jnp.float32)
    large = MAX_EXACT + (jnp.log(nf / MAX_EXACT) / math.log(MAX_DISTANCE / MAX_EXACT)
                         * (N_BUCKETS - MAX_EXACT)).astype(jnp.int32)
    large = jnp.minimum(large, N_BUCKETS - 1)
    return jnp.where(n < MAX_EXACT, n, large)


def _band_attention(qb, kb, vb, d, valid, sink, rel_bias):
    B, N, Tq, H, HD = qb.shape
    Tk = kb.shape[2]
    G = H // N_KV
    qg = qb.reshape(B, N, Tq, N_KV, G, HD)
    s = jnp.einsum('bnqkgd,bnskd->bnkgqs', qg, kb).astype(jnp.float32) * (HD ** -0.5)
    bias = rel_bias.astype(jnp.float32)[_t5_bucket(d)]
    bias = jnp.transpose(bias, (2, 0, 1)).reshape(N_KV, G, Tq, Tk)
    s = jnp.where(valid[None, :, None, None], s + bias, -jnp.inf)
    sk = sink.astype(jnp.float32).reshape(1, 1, N_KV, G, 1, 1)
    m = jnp.maximum(jnp.max(s, axis=-1, keepdims=True), sk)
    p = jnp.exp(s - m)
    p = p / (jnp.sum(p, axis=-1, keepdims=True) + jnp.exp(sk - m))
    o = jnp.einsum('bnkgqs,bnskd->bnqkgd', p.astype(vb.dtype), vb)
    return o.reshape(B, N * Tq, H * HD)


def _mixer(h, lru_h0, lru_conv_prev, k_cache, v_cache, pos0, w_in, conv_w, conv_b, wr, br, wi, bi, lam,
           w_lru_o, w_attn_o, w_out, sink, rel_bias):
    B, T, _ = h.shape
    proj = h @ w_in
    c0 = D_RNN
    c1 = c0 + Q_COLS
    c2 = c1 + KV_COLS
    c3 = c2 + KV_COLS
    xr, q, k, v, g = jnp.split(proj, [c0, c1, c2, c3], axis=-1)
    g_lru, g_attn = jnp.split(g, 2, axis=-1)
    xc, conv_state = _causal_dwconv(xr, lru_conv_prev, conv_w, conv_b)
    lru_out, h_last = _rg_lru(xc, lru_h0, wr, br, wi, bi, lam, pos0)
    q = q.reshape(B, T, N_HEADS, HEAD_DIM)
    k = k.reshape(B, T, N_KV, HEAD_DIM)
    v = v.reshape(B, T, N_KV, HEAD_DIM)
    if k_cache is None:
        nb = T // WINDOW
        qb = q.reshape(B, nb, WINDOW, N_HEADS, HEAD_DIM)

        def blocks(z):
            zpad = jnp.concatenate([jnp.zeros((B, WINDOW) + z.shape[2:], z.dtype), z], axis=1)
            prev = zpad[:, :T].reshape(B, nb, WINDOW, N_KV, HEAD_DIM)
            cur = z.reshape(B, nb, WINDOW, N_KV, HEAD_DIM)
            return jnp.concatenate([prev, cur], axis=2)

        kb, vb = blocks(k), blocks(v)
        d = (jnp.arange(WINDOW)[:, None] + WINDOW) - jnp.arange(2 * WINDOW)[None, :]
        in_band = (d >= 0) & (d < WINDOW)
        key_exists = (jnp.arange(nb)[:, None, None] > 0) | (jnp.arange(2 * WINDOW)[None, None, :] >= WINDOW)
        valid = in_band[None] & key_exists
        attn = _band_attention(qb, kb, vb, d, valid, sink, rel_bias)
        k_state, v_state = k[:, -WINDOW:], v[:, -WINDOW:]
    else:
        kk = jnp.concatenate([k_cache.astype(k.dtype), k], axis=1)
        vv = jnp.concatenate([v_cache.astype(v.dtype), v], axis=1)
        d = (jnp.arange(T)[:, None] + WINDOW) - jnp.arange(WINDOW + T)[None, :]
        valid = ((d >= 0) & (d < WINDOW))[None]
        attn = _band_attention(q[:, None], kk[:, None], vv[:, None], d, valid, sink, rel_bias)
        k_state, v_state = kk[:, -WINDOW:], vv[:, -WINDOW:]
    merged = jax.nn.sigmoid(g_lru) * (lru_out @ w_lru_o) + jax.nn.sigmoid(g_attn) * (attn @ w_attn_o)
    return merged @ w_out, h_last, conv_state, k_state, v_state


def _conv_ffn(h, prev, w_up, conv_w, conv_b, w_down):
    u = h @ w_up
    u, conv_state = _causal_dwconv(u, prev, conv_w, conv_b)
    val, gate = jnp.split(u, 2, axis=-1)
    return (jax.nn.gelu(gate, approximate=True) * val) @ w_down, conv_state


def _layer(x, lru_h0, lru_conv_prev, k_cache, v_cache, ffn_prev, pos0, l,
           norm_mix_pre, norm_mix_post, norm_ffn_pre, norm_ffn_post, w_in, conv_lru_w, conv_lru_b,
           lru_wr, lru_br, lru_wi, lru_bi, lru_lambda, w_lru_o, w_attn_o, w_out, attn_sink, rel_bias,
           w_up, ffn_conv_w, ffn_conv_b, w_down):
    m, h_last, conv_state, k_state, v_state = _mixer(
        _rmsnorm(x, norm_mix_pre[l]), lru_h0, lru_conv_prev, k_cache, v_cache, pos0,
        w_in[l], conv_lru_w[l], conv_lru_b[l], lru_wr[l], lru_br[l], lru_wi[l], lru_bi[l], lru_lambda[l],
        w_lru_o[l], w_attn_o[l], w_out[l], attn_sink[l], rel_bias)
    x = x + _rmsnorm(m, norm_mix_post[l])
    f, ffn_state = _conv_ffn(_rmsnorm(x, norm_ffn_pre[l]), ffn_prev, w_up[l], ffn_conv_w[l], ffn_conv_b[l], w_down[l])
    x = x + _rmsnorm(f, norm_ffn_post[l])
    return x, h_last, conv_state, k_state, v_state, ffn_state


def setup_inputs(seed: int = 0) -> dict:
    key = jax.random.key(seed)
    ks = jax.random.split(key, 32)
    nrm = lambda k, shape, s: s * jax.random.normal(k, shape, jnp.float32)
    u = jax.random.uniform(ks[20], (DEPTH, D_RNN), jnp.float32, 0.9, 0.999)
    sa = u ** (1.0 / LRU_C)
    return {
        "x_prompt": nrm(ks[0], (BATCH, SEQ, D_MODEL), 1.0),
        "x_sample": nrm(ks[1], (DEC_BATCH, DEC_SEQ, D_MODEL), 1.0),
        "state_lru_h": nrm(ks[2], (DEPTH, DEC_BATCH, D_RNN), 0.5),
        "state_lru_conv": nrm(ks[3], (DEPTH, DEC_BATCH, CONV_LRU - 1, D_RNN), 1.0),
        "cache_win_k": nrm(ks[4], (DEPTH, DEC_BATCH, WINDOW, N_KV, HEAD_DIM), 1.0),
        "cache_win_v": nrm(ks[5], (DEPTH, DEC_BATCH, WINDOW, N_KV, HEAD_DIM), 1.0),
        "state_ffn_conv": nrm(ks[6], (DEPTH, DEC_BATCH, CONV_FF - 1, 2 * D_FF), 1.0),
        "norm_mix_pre": 1.0 + nrm(ks[7], (DEPTH, D_MODEL), 0.05),
        "norm_mix_post": 1.0 + nrm(ks[8], (DEPTH, D_MODEL), 0.05),
        "norm_ffn_pre": 1.0 + nrm(ks[9], (DEPTH, D_MODEL), 0.05),
        "norm_ffn_post": 1.0 + nrm(ks[10], (DEPTH, D_MODEL), 0.05),
        "w_in": nrm(ks[11], (DEPTH, D_MODEL, IN_COLS), D_MODEL ** -0.5),
        "conv_lru_w": nrm(ks[12], (DEPTH, CONV_LRU, D_RNN), CONV_LRU ** -0.5),
        "conv_lru_b": nrm(ks[13], (DEPTH, D_RNN), 0.01),
        "lru_wr": nrm(ks[14], (DEPTH, N_LRU_BLOCKS, LRU_BLOCK, LRU_BLOCK), LRU_BLOCK ** -0.5),
        "lru_br": nrm(ks[15], (DEPTH, D_RNN), 0.1),
        "lru_wi": nrm(ks[16], (DEPTH, N_LRU_BLOCKS, LRU_BLOCK, LRU_BLOCK), LRU_BLOCK ** -0.5),
        "lru_bi": nrm(ks[17], (DEPTH, D_RNN), 0.1),
        "lru_lambda": jnp.log(sa) - jnp.log1p(-sa),
        "w_lru_o": nrm(ks[18], (DEPTH, D_RNN, D_MODEL), D_RNN ** -0.5),
        "w_attn_o": nrm(ks[19], (DEPTH, Q_COLS, D_MODEL), Q_COLS ** -0.5),
        "w_out": nrm(ks[21], (DEPTH, D_MODEL, D_MODEL), D_MODEL ** -0.5),
        "attn_sink": nrm(ks[22], (DEPTH, N_HEADS), 1.0),
        "rel_bias": nrm(ks[23], (N_BUCKETS, N_HEADS), 0.5),
        "w_up": nrm(ks[24], (DEPTH, D_MODEL, 2 * D_FF), D_MODEL ** -0.5),
        "ffn_conv_w": nrm(ks[25], (DEPTH, CONV_FF, 2 * D_FF), CONV_FF ** -0.5),
        "ffn_conv_b": nrm(ks[26], (DEPTH, 2 * D_FF), 0.01),
        "w_down": nrm(ks[27], (DEPTH, D_FF, D_MODEL), D_FF ** -0.5),
    }


def reference(x_prompt, x_sample, state_lru_h, state_lru_conv, cache_win_k, cache_win_v, state_ffn_conv,
              norm_mix_pre, norm_mix_post, norm_ffn_pre, norm_ffn_post, w_in, conv_lru_w, conv_lru_b,
              lru_wr, lru_br, lru_wi, lru_bi, lru_lambda, w_lru_o, w_attn_o, w_out, attn_sink, rel_bias,
              w_up, ffn_conv_w, ffn_conv_b, w_down):
    weights = (norm_mix_pre, norm_mix_post, norm_ffn_pre, norm_ffn_post, w_in, conv_lru_w, conv_lru_b,
               lru_wr, lru_br, lru_wi, lru_bi, lru_lambda, w_lru_o, w_attn_o, w_out, attn_sink, rel_bias,
               w_up, ffn_conv_w, ffn_conv_b, w_down)
    bp = x_prompt.shape[0]
    yp, ys = x_prompt, x_sample
    p_h, p_c, p_k, p_v, p_f = [], [], [], [], []
    s_h, s_c, s_k, s_v, s_f = [], [], [], [], []
    for l in range(DEPTH):
        yp, h1, c1, k1, v1, f1 = _layer(
            yp, jnp.zeros((bp, D_RNN), jnp.float32), jnp.zeros((bp, CONV_LRU - 1, D_RNN), yp.dtype),
            None, None, jnp.zeros((bp, CONV_FF - 1, 2 * D_FF), yp.dtype), 0, l, *weights)
        p_h.append(h1); p_c.append(c1); p_k.append(k1); p_v.append(v1); p_f.append(f1)
        ys, h2, c2, k2, v2, f2 = _layer(
            ys, state_lru_h[l], state_lru_conv[l], cache_win_k[l], cache_win_v[l], state_ffn_conv[l],
            PAST_LEN, l, *weights)
        s_h.append(h2); s_c.append(c2); s_k.append(k2); s_v.append(v2); s_f.append(f2)
    return (yp, ys,
            jnp.stack(p_h), jnp.stack(p_c), jnp.stack(p_k), jnp.stack(p_v), jnp.stack(p_f),
            jnp.stack(s_h), jnp.stack(s_c), jnp.stack(s_k), jnp.stack(s_v), jnp.stack(s_f))
```

```python
import functools
import math

import numpy as np
import jax
import jax.numpy as jnp
from jax import lax
from jax.experimental import pallas as pl
from jax.experimental.pallas import tpu as pltpu

D_MODEL = 1024
DEPTH = 2
PAST_LEN = 16384
D_RNN = D_MODEL
N_LRU_BLOCKS = 8
LRU_BLOCK = D_RNN // N_LRU_BLOCKS
CONV_LRU = 4
LRU_C = 8.0
N_HEADS = 8
N_KV = 2
GROUP = N_HEADS // N_KV
HEAD_DIM = D_MODEL // N_HEADS
WINDOW = 128
N_BUCKETS = 32
MAX_EXACT = N_BUCKETS // 2
MAX_DISTANCE = 128
D_FF = 4 * D_MODEL
CONV_FF = 3
EPS = 1e-6
Q_COLS = N_HEADS * HEAD_DIM
KV_COLS = N_KV * HEAD_DIM
IN_COLS = D_RNN + Q_COLS + 2 * KV_COLS + 2 * D_MODEL
C_Q = D_RNN
C_K = C_Q + Q_COLS
C_V = C_K + KV_COLS
C_G = C_V + KV_COLS

F32 = jnp.float32
BF16 = jnp.bfloat16

SUBLANES = 8
LANES = 128
VMEM_LIMIT_BYTES = 56 * 1024 * 1024

V_NORM_MIX_PRE, V_NORM_MIX_POST, V_CONV_B, V_BR, V_BI, V_LAMBDA, V_CONV_W = 0, 1, 2, 3, 4, 5, 6
V_NORM_FFN_PRE, V_NORM_FFN_POST = 10, 11
VEC_ROWS = 16

PROMPT_TILE = 256
FF_CHUNK = 512
SAMPLE_KEYS = 136
SINK_COL = WINDOW + 4
SAMPLE_BATCH_TILE = 16


def _bucket_thresholds():
    d = np.arange(0, 2 * WINDOW)
    nf = np.maximum(d, 1).astype(np.float64)
    large = MAX_EXACT + (np.log(nf / MAX_EXACT) / math.log(MAX_DISTANCE / MAX_EXACT)
                         * (N_BUCKETS - MAX_EXACT)).astype(np.int64)
    bucket = np.where(d < MAX_EXACT, d, np.minimum(large, N_BUCKETS - 1))
    return tuple(int(d[bucket >= b].min()) for b in range(1, N_BUCKETS))


_BUCKET_THRESHOLDS = _bucket_thresholds()


def _dot(a, b):
    return jnp.dot(a, b, preferred_element_type=F32)


def _dot_nt(a, b):
    return lax.dot_general(a, b, (((1,), (1,)), ((), ())), preferred_element_type=F32)


def _rmsnorm(x, g):
    return x * lax.rsqrt(jnp.mean(x * x, axis=-1, keepdims=True) + EPS) * g


def _sigmoid(x):
    return 1.0 / (1.0 + jnp.exp(-x))


def _gelu_tanh(x):
    return 0.5 * x * (1.0 + jnp.tanh(math.sqrt(2.0 / math.pi) * (x + 0.044715 * (x * x * x))))


def _const_spec(block_shape, index):
    return pl.BlockSpec(block_shape, lambda *_: index, pipeline_mode=pl.Buffered(1))


def _bucket_of(d):
    n = jnp.maximum(d, 0)
    bucket = jnp.zeros(d.shape, jnp.int32)
    for thr in _BUCKET_THRESHOLDS:
        bucket = bucket + jnp.where(n >= thr, 1, 0)
    return bucket


def _table_kernel(rel_ref, sink_ref, pt_ref, st_ref):
    qi = lax.broadcasted_iota(jnp.int32, (WINDOW, 2 * WINDOW), 0)
    kj = lax.broadcasted_iota(jnp.int32, (WINDOW, 2 * WINDOW), 1)
    d = qi + WINDOW - kj
    bucket = _bucket_of(d)
    in_band = jnp.where(d >= 0, jnp.where(d < WINDOW, 1, 0), 0)
    cur_only = jnp.where(kj >= WINDOW, in_band, 0)
    for h in range(N_HEADS):
        val = jnp.zeros(d.shape, F32)
        for b in range(N_BUCKETS):
            val = jnp.where(bucket == b, rel_ref[b, h], val)
        pt_ref[1, h] = jnp.where(in_band == 1, val, -jnp.inf)
        pt_ref[0, h] = jnp.where(cur_only == 1, val, -jnp.inf)

    r = lax.broadcasted_iota(jnp.int32, (4 * N_HEADS, SAMPLE_KEYS), 0)
    j = lax.broadcasted_iota(jnp.int32, (4 * N_HEADS, SAMPLE_KEYS), 1)
    t = lax.shift_right_logical(r, 3)
    hh = lax.bitwise_and(r, N_HEADS - 1)
    d = t + WINDOW - j
    bucket = _bucket_of(d)
    in_band = jnp.where(d >= 0, jnp.where(d < WINDOW, 1, 0), 0)
    val = jnp.zeros(d.shape, F32)
    for h in range(N_HEADS):
        hval = jnp.zeros(d.shape, F32)
        for b in range(N_BUCKETS):
            hval = jnp.where(bucket == b, rel_ref[b, h], hval)
        val = jnp.where(hh == h, hval, val)
    val = jnp.where(in_band == 1, val, -jnp.inf)
    for l in range(DEPTH):
        sk = jnp.zeros(d.shape, F32)
        for h in range(N_HEADS):
            sk = jnp.where(hh == h, sink_ref[l, h], sk)
        st_ref[l] = jnp.where(j == SINK_COL, sk, val)


def _bias_tables(rel_bias, attn_sink):
    smem = pl.BlockSpec(memory_space=pltpu.SMEM)
    return pl.pallas_call(
        _table_kernel,
        out_shape=(jax.ShapeDtypeStruct((2, N_HEADS, WINDOW, 2 * WINDOW), F32),
                   jax.ShapeDtypeStruct((DEPTH, 4 * N_HEADS, SAMPLE_KEYS), F32)),
        in_specs=[smem, smem],
        name="bias_tables",
    )(rel_bias, attn_sink)


def _lru_coeffs(xc, vec_ref, wri_ref, n):
    cb = slice(n * LRU_BLOCK, (n + 1) * LRU_BLOCK)
    rw = _dot(xc.astype(BF16), wri_ref[n])
    r = _sigmoid(rw[:, :LRU_BLOCK] + vec_ref[V_BR:V_BR + 1, cb])
    i = _sigmoid(rw[:, LRU_BLOCK:] + vec_ref[V_BI:V_BI + 1, cb])
    z = -vec_ref[V_LAMBDA:V_LAMBDA + 1, cb]
    softplus = jnp.maximum(z, 0.0) + jnp.log1p(jnp.exp(-jnp.abs(z)))
    log_a = (-LRU_C * softplus) * r
    a = jnp.exp(log_a)
    mult = jnp.sqrt(jnp.maximum(1.0 - a * a, 0.0))
    return a, i * xc, mult


def _pmix_kernel(x_ref, vec_ref, w_in_ref, wri_ref, w_lo_ref, w_ao_ref, w_out_ref, tab_ref, sink_ref,
                 y_ref, hlast_ref, cstate_ref, kstate_ref, vstate_ref,
                 h_s, xr_s, q_s, k_s, v_s, lru_s, attn_s, hc_s, *, layer, tc):
    t = pl.program_id(1)

    @pl.when(t == 0)
    def _():
        xr_s[0:SUBLANES, :] = jnp.zeros((SUBLANES, D_RNN), F32)
        k_s[0:WINDOW, :] = jnp.zeros((WINDOW, KV_COLS), BF16)
        v_s[0:WINDOW, :] = jnp.zeros((WINDOW, KV_COLS), BF16)
        hc_s[...] = jnp.zeros(hc_s.shape, F32)

    h_s[...] = _rmsnorm(x_ref[...], vec_ref[V_NORM_MIX_PRE:V_NORM_MIX_PRE + 1, :]).astype(BF16)
    hb = h_s[...]
    xr_s[SUBLANES:SUBLANES + tc, :] = _dot(hb, w_in_ref[:, 0:C_Q])
    q_s[...] = (_dot(hb, w_in_ref[:, C_Q:C_K]) * (HEAD_DIM ** -0.5)).astype(BF16)
    kf = _dot(hb, w_in_ref[:, C_K:C_V])
    vf = _dot(hb, w_in_ref[:, C_V:C_G])
    k_s[WINDOW:WINDOW + tc, :] = kf.astype(BF16)
    v_s[WINDOW:WINDOW + tc, :] = vf.astype(BF16)
    kstate_ref[...] = kf[tc - WINDOW:tc, :]
    vstate_ref[...] = vf[tc - WINDOW:tc, :]
    cstate_ref[...] = xr_s[SUBLANES + tc - (CONV_LRU - 1):SUBLANES + tc, :]

    sub = lax.broadcasted_iota(jnp.int32, (SUBLANES, LRU_BLOCK), 0)
    seq_start = (sub + t) == 0
    for n in range(N_LRU_BLOCKS):
        cb = slice(n * LRU_BLOCK, (n + 1) * LRU_BLOCK)
        xc = vec_ref[V_CONV_B:V_CONV_B + 1, cb]
        for j in range(CONV_LRU):
            off = SUBLANES - (CONV_LRU - 1) + j
            xc = xc + vec_ref[V_CONV_W + j:V_CONV_W + j + 1, cb] * xr_s[off:off + tc, cb]
        a, ix, mult = _lru_coeffs(xc, vec_ref, wri_ref, n)
        b = mult * ix
        carry = hc_s[0:1, cb]
        hs = []
        for g in range(tc // SUBLANES):
            rows = slice(g * SUBLANES, (g + 1) * SUBLANES)
            ag, bg = a[rows], b[rows]
            if g == 0:
                bg = jnp.where(seq_start, ix[rows], bg)
            for s in (1, 2, 4):
                ash = jnp.where(sub >= s, pltpu.roll(ag, s, 0), 1.0)
                bsh = jnp.where(sub >= s, pltpu.roll(bg, s, 0), 0.0)
                bg = ag * bsh + bg
                ag = ag * ash
            hg = ag * carry + bg
            carry = hg[SUBLANES - 1:SUBLANES, :]
            hs.append(hg)
        lru_s[:, cb] = jnp.concatenate(hs, axis=0).astype(BF16)
        hc_s[0:1, cb] = carry
        hlast_ref[0:1, cb] = carry

    first = jnp.where(t == 0, 0, 1)
    for j in range(tc // WINDOW):
        qrows = slice(j * WINDOW, (j + 1) * WINDOW)
        krows = slice(j * WINDOW, (j + 2) * WINDOW)
        variant = first if j == 0 else 1
        for g in range(N_KV):
            kvc = slice(g * HEAD_DIM, (g + 1) * HEAD_DIM)
            q4 = jnp.concatenate(
                [q_s[qrows, (g * GROUP + hg) * HEAD_DIM:(g * GROUP + hg + 1) * HEAD_DIM] for hg in range(GROUP)],
                axis=0)
            s = _dot_nt(q4, k_s[krows, kvc])
            ps, invs = [], []
            for hg in range(GROUP):
                head = g * GROUP + hg
                sh = s[hg * WINDOW:(hg + 1) * WINDOW] + tab_ref[variant, head]
                sk = sink_ref[layer, head]
                m = jnp.maximum(jnp.max(sh, axis=-1, keepdims=True), sk)
                p = jnp.exp(sh - m)
                den = jnp.sum(p, axis=-1, keepdims=True) + jnp.exp(sk - m)
                ps.append(p.astype(BF16))
                invs.append(1.0 / den)
            o4 = _dot(jnp.concatenate(ps, axis=0), v_s[krows, kvc])
            for hg in range(GROUP):
                head = g * GROUP + hg
                attn_s[qrows, head * HEAD_DIM:(head + 1) * HEAD_DIM] = (
                    o4[hg * WINDOW:(hg + 1) * WINDOW] * invs[hg]).astype(BF16)

    xr_s[0:SUBLANES, :] = xr_s[tc:tc + SUBLANES, :]
    k_s[0:WINDOW, :] = k_s[tc:tc + WINDOW, :]
    v_s[0:WINDOW, :] = v_s[tc:tc + WINDOW, :]

    g_lru = _sigmoid(_dot(hb, w_in_ref[:, C_G:C_G + D_MODEL]))
    g_att = _sigmoid(_dot(hb, w_in_ref[:, C_G + D_MODEL:IN_COLS]))
    merged = g_lru * _dot(lru_s[...], w_lo_ref[...]) + g_att * _dot(attn_s[...], w_ao_ref[...])
    m = _dot(merged.astype(BF16), w_out_ref[...])
    y_ref[...] = x_ref[...] + _rmsnorm(m, vec_ref[V_NORM_MIX_POST:V_NORM_MIX_POST + 1, :])


def _prompt_mixer(x, vecs, w_in, wri, w_lo, w_ao, w_out, tab, sink, layer):
    bsz, seq, _ = x.shape
    tc = PROMPT_TILE
    kern = functools.partial(_pmix_kernel, layer=layer, tc=tc)
    tile = pl.BlockSpec((None, tc, D_MODEL), lambda b, t: (b, t, 0))
    per_batch = lambda rows, cols: pl.BlockSpec((None, rows, cols), lambda b, t: (b, 0, 0))
    return pl.pallas_call(
        kern,
        grid=(bsz, seq // tc),
        in_specs=[
            tile,
            _const_spec((None, VEC_ROWS, D_MODEL), (layer, 0, 0)),
            _const_spec((None, D_MODEL, IN_COLS), (layer, 0, 0)),
            _const_spec((None, N_LRU_BLOCKS, LRU_BLOCK, 2 * LRU_BLOCK), (layer, 0, 0, 0)),
            _const_spec((None, D_RNN, D_MODEL), (layer, 0, 0)),
            _const_spec((None, Q_COLS, D_MODEL), (layer, 0, 0)),
            _const_spec((None, D_MODEL, D_MODEL), (layer, 0, 0)),
            _const_spec((2, N_HEADS, WINDOW, 2 * WINDOW), (0, 0, 0, 0)),
            pl.BlockSpec(memory_space=pltpu.SMEM),
        ],
        out_specs=(tile, per_batch(1, D_RNN), per_batch(CONV_LRU - 1, D_RNN),
                   per_batch(WINDOW, KV_COLS), per_batch(WINDOW, KV_COLS)),
        out_shape=(jax.ShapeDtypeStruct((bsz, seq, D_MODEL), F32),
                   jax.ShapeDtypeStruct((bsz, 1, D_RNN), F32),
                   jax.ShapeDtypeStruct((bsz, CONV_LRU - 1, D_RNN), F32),
                   jax.ShapeDtypeStruct((bsz, WINDOW, KV_COLS), F32),
                   jax.ShapeDtypeStruct((bsz, WINDOW, KV_COLS), F32)),
        scratch_shapes=[
            pltpu.VMEM((tc, D_MODEL), BF16),
            pltpu.VMEM((SUBLANES + tc, D_RNN), F32),
            pltpu.VMEM((tc, Q_COLS), BF16),
            pltpu.VMEM((WINDOW + tc, KV_COLS), BF16),
            pltpu.VMEM((WINDOW + tc, KV_COLS), BF16),
            pltpu.VMEM((tc, D_RNN), BF16),
            pltpu.VMEM((tc, Q_COLS), BF16),
            pltpu.VMEM((SUBLANES, D_RNN), F32),
        ],
        compiler_params=pltpu.CompilerParams(
            dimension_semantics=("arbitrary", "arbitrary"), vmem_limit_bytes=VMEM_LIMIT_BYTES),
        name=f"prompt_mixer_l{layer}",
    )(x, vecs, w_in, wri, w_lo, w_ao, w_out, tab, sink)


def _pffn_kernel(x_ref, vec_ref, fcv_ref, w_up_ref, w_down_ref, y_ref, fstate_ref,
                 h_s, u_s, tail_s, act_s, *, tc):
    t = pl.program_id(1)

    @pl.when(t == 0)
    def _():
        tail_s[...] = jnp.zeros(tail_s.shape, F32)

    h_s[...] = _rmsnorm(x_ref[...], vec_ref[V_NORM_FFN_PRE:V_NORM_FFN_PRE + 1, :]).astype(BF16)
    hb = h_s[...]
    for c in range(D_FF // FF_CHUNK):
        conv = []
        for part in range(2):
            cols = slice(part * D_FF + c * FF_CHUNK, part * D_FF + (c + 1) * FF_CHUNK)
            us = u_s.at[part]
            us[0:SUBLANES, :] = tail_s[:, cols]
            us[SUBLANES:SUBLANES + tc, :] = _dot(hb, w_up_ref[:, cols])
            y = fcv_ref[CONV_FF:CONV_FF + 1, cols]
            for j in range(CONV_FF):
                off = SUBLANES - (CONV_FF - 1) + j
                y = y + fcv_ref[j:j + 1, cols] * us[off:off + tc, :]
            conv.append(y)
            tail_s[:, cols] = us[tc:tc + SUBLANES, :]
            fstate_ref[:, cols] = us[SUBLANES + tc - (CONV_FF - 1):SUBLANES + tc, :]
        act_s[:, c * FF_CHUNK:(c + 1) * FF_CHUNK] = (_gelu_tanh(conv[1]) * conv[0]).astype(BF16)
    f = _dot(act_s[...], w_down_ref[...])
    y_ref[...] = x_ref[...] + _rmsnorm(f, vec_ref[V_NORM_FFN_POST:V_NORM_FFN_POST + 1, :])


def _prompt_ffn(x, vecs, fcv, w_up, w_down, layer):
    bsz, seq, _ = x.shape
    tc = PROMPT_TILE
    kern = functools.partial(_pffn_kernel, tc=tc)
    tile = pl.BlockSpec((None, tc, D_MODEL), lambda b, t: (b, t, 0))
    return pl.pallas_call(
        kern,
        grid=(bsz, seq // tc),
        in_specs=[
            tile,
            _const_spec((None, VEC_ROWS, D_MODEL), (layer, 0, 0)),
            _const_spec((None, SUBLANES, 2 * D_FF), (layer, 0, 0)),
            _const_spec((None, D_MODEL, 2 * D_FF), (layer, 0, 0)),
            _const_spec((None, D_FF, D_MODEL), (layer, 0, 0)),
        ],
        out_specs=(tile, pl.BlockSpec((None, CONV_FF - 1, 2 * D_FF), lambda b, t: (b, 0, 0))),
        out_shape=(jax.ShapeDtypeStruct((bsz, seq, D_MODEL), F32),
                   jax.ShapeDtypeStruct((bsz, CONV_FF - 1, 2 * D_FF), F32)),
        scratch_shapes=[
            pltpu.VMEM((tc, D_MODEL), BF16),
            pltpu.VMEM((2, SUBLANES + tc, FF_CHUNK), F32),
            pltpu.VMEM((SUBLANES, 2 * D_FF), F32),
            pltpu.VMEM((tc, D_FF), BF16),
        ],
        compiler_params=pltpu.CompilerParams(
            dimension_semantics=("arbitrary", "arbitrary"), vmem_limit_bytes=VMEM_LIMIT_BYTES),
        name=f"prompt_ffn_l{layer}",
    )(x, vecs, fcv, w_up, w_down)


def _smix_in_kernel(xs_ref, h0_ref, cprev_ref, vec_ref, w_in_ref, wri_ref,
                    q_ref, k_ref, v_ref, lru_ref, gate_ref, hlast_ref, cstate_ref,
                    h_s, xr_s, *, nb, nt):
    for t in range(nt):
        h_s[t * nb:(t + 1) * nb, :] = _rmsnorm(
            xs_ref[:, t * D_MODEL:(t + 1) * D_MODEL], vec_ref[V_NORM_MIX_PRE:V_NORM_MIX_PRE + 1, :]).astype(BF16)
    hb = h_s[...]
    xr_s[...] = _dot(hb, w_in_ref[:, 0:C_Q])
    qf = _dot(hb, w_in_ref[:, C_Q:C_K]) * (HEAD_DIM ** -0.5)
    kf = _dot(hb, w_in_ref[:, C_K:C_V])
    vf = _dot(hb, w_in_ref[:, C_V:C_G])
    for t in range(nt):
        rows = slice(t * nb, (t + 1) * nb)
        q_ref[:, t * Q_COLS:(t + 1) * Q_COLS] = qf[rows].astype(BF16)
        k_ref[:, t * KV_COLS:(t + 1) * KV_COLS] = kf[rows]
        v_ref[:, t * KV_COLS:(t + 1) * KV_COLS] = vf[rows]
    gate_ref[...] = _sigmoid(_dot(hb, w_in_ref[:, C_G:IN_COLS]))

    npre = CONV_LRU - 1
    for t in range(nt - npre, nt):
        cstate_ref[:, (t - (nt - npre)) * D_RNN:(t - (nt - npre) + 1) * D_RNN] = xr_s[t * nb:(t + 1) * nb, :]

    for n in range(N_LRU_BLOCKS):
        cb = slice(n * LRU_BLOCK, (n + 1) * LRU_BLOCK)
        xx = [cprev_ref[:, j * D_RNN + n * LRU_BLOCK:j * D_RNN + (n + 1) * LRU_BLOCK] for j in range(npre)]
        xx += [xr_s[t * nb:(t + 1) * nb, cb] for t in range(nt)]
        xcs = []
        for t in range(nt):
            xc = vec_ref[V_CONV_B:V_CONV_B + 1, cb]
            for j in range(CONV_LRU):
                xc = xc + vec_ref[V_CONV_W + j:V_CONV_W + j + 1, cb] * xx[t + j]
            xcs.append(xc)
        a, ix, mult = _lru_coeffs(jnp.concatenate(xcs, axis=0), vec_ref, wri_ref, n)
        b = mult * ix
        h = h0_ref[:, cb]
        for t in range(nt):
            rows = slice(t * nb, (t + 1) * nb)
            h = a[rows] * h + b[rows]
            lru_ref[rows, cb] = h.astype(BF16)
        hlast_ref[:, cb] = h


def _sample_mixer_in(xs, h0, cprev, vecs, w_in, wri, layer):
    nb = xs.shape[0]
    nt = xs.shape[1] // D_MODEL
    assert PAST_LEN > 0
    kern = functools.partial(_smix_in_kernel, nb=nb, nt=nt)
    whole = lambda shape: pl.BlockSpec(shape, lambda i: (0,) * len(shape))
    return pl.pallas_call(
        kern,
        grid=(1,),
        in_specs=[
            whole((nb, nt * D_MODEL)),
            pl.BlockSpec((None, nb, D_RNN), lambda i: (layer, 0, 0)),
            pl.BlockSpec((None, nb, (CONV_LRU - 1) * D_RNN), lambda i: (layer, 0, 0)),
            _const_spec((None, VEC_ROWS, D_MODEL), (layer, 0, 0)),
            _const_spec((None, D_MODEL, IN_COLS), (layer, 0, 0)),
            _const_spec((None, N_LRU_BLOCKS, LRU_BLOCK, 2 * LRU_BLOCK), (layer, 0, 0, 0)),
        ],
        out_specs=(whole((nb, nt * Q_COLS)), whole((nb, nt * KV_COLS)), whole((nb, nt * KV_COLS)),
                   whole((nt * nb, D_RNN)), whole((nt * nb, 2 * D_MODEL)),
                   whole((nb, D_RNN)), whole((nb, (CONV_LRU - 1) * D_RNN))),
        out_shape=(jax.ShapeDtypeStruct((nb, nt * Q_COLS), BF16),
                   jax.ShapeDtypeStruct((nb, nt * KV_COLS), F32),
                   jax.ShapeDtypeStruct((nb, nt * KV_COLS), F32),
                   jax.ShapeDtypeStruct((nt * nb, D_RNN), BF16),
                   jax.ShapeDtypeStruct((nt * nb, 2 * D_MODEL), F32),
                   jax.ShapeDtypeStruct((nb, D_RNN), F32),
                   jax.ShapeDtypeStruct((nb, (CONV_LRU - 1) * D_RNN), F32)),
        scratch_shapes=[pltpu.VMEM((nt * nb, D_MODEL), BF16), pltpu.VMEM((nt * nb, D_RNN), F32)],
        compiler_params=pltpu.CompilerParams(
            dimension_semantics=("arbitrary",), vmem_limit_bytes=VMEM_LIMIT_BYTES),
        name=f"sample_mixer_in_l{layer}",
    )(xs, h0, cprev, vecs, w_in, wri)


def _sattn_kernel(q_ref, kn_ref, vn_ref, ck_ref, cv_ref, tab_ref, attn_ref, sk_ref, sv_ref,
                  kc_s, vc_s, *, bt, nt):
    pad = SAMPLE_KEYS - WINDOW
    kc_s[WINDOW:SAMPLE_KEYS, :] = jnp.zeros((pad, KV_COLS), F32)
    vc_s[WINDOW:SAMPLE_KEYS, :] = jnp.zeros((pad, KV_COLS), F32)
    row = lax.broadcasted_iota(jnp.int32, (nt * N_HEADS, HEAD_DIM), 0)
    in_group0 = lax.bitwise_and(row, N_HEADS - 1) < GROUP

    def body(b, _):
        kc_s[0:WINDOW, :] = ck_ref[b]
        vc_s[0:WINDOW, :] = cv_ref[b]
        kc_s[WINDOW:WINDOW + nt, :] = kn_ref[b]
        vc_s[WINDOW:WINDOW + nt, :] = vn_ref[b]
        sk_ref[b] = kc_s[nt:nt + WINDOW, :]
        sv_ref[b] = vc_s[nt:nt + WINDOW, :]
        kb = kc_s[...].astype(BF16)
        vb = vc_s[...].astype(BF16)
        qb = q_ref[b]
        outs = []
        for g in range(N_KV):
            kvc = slice(g * HEAD_DIM, (g + 1) * HEAD_DIM)
            s = _dot_nt(qb, kb[:, kvc]) + tab_ref[...]
            m = jnp.max(s, axis=-1, keepdims=True)
            p = jnp.exp(s - m)
            den = jnp.sum(p, axis=-1, keepdims=True)
            outs.append(_dot(p.astype(BF16), vb[:, kvc]) * (1.0 / den))
        attn_ref[b] = jnp.where(in_group0, outs[0], outs[1]).astype(BF16)
        return 0

    lax.fori_loop(0, bt, body, 0)


def _sample_attention(q, kn, vn, cache_k, cache_v, stab, layer):
    nb, rows, _ = q.shape
    nt = rows // N_HEADS
    bt = SAMPLE_BATCH_TILE
    kern = functools.partial(_sattn_kernel, bt=bt, nt=nt)
    cache_spec = pl.BlockSpec((None, bt, WINDOW, KV_COLS), lambda i: (layer, i, 0, 0))
    new_spec = pl.BlockSpec((bt, nt, KV_COLS), lambda i: (i, 0, 0))
    state_spec = pl.BlockSpec((bt, WINDOW, KV_COLS), lambda i: (i, 0, 0))
    q_spec = pl.BlockSpec((bt, rows, HEAD_DIM), lambda i: (i, 0, 0))
    return pl.pallas_call(
        kern,
        grid=(nb // bt,),
        in_specs=[q_spec, new_spec, new_spec, cache_spec, cache_spec,
                  _const_spec((None, rows, SAMPLE_KEYS), (layer, 0, 0))],
        out_specs=(q_spec, state_spec, state_spec),
        out_shape=(jax.ShapeDtypeStruct((nb, rows, HEAD_DIM), BF16),
                   jax.ShapeDtypeStruct((nb, WINDOW, KV_COLS), F32),
                   jax.ShapeDtypeStruct((nb, WINDOW, KV_COLS), F32)),
        scratch_shapes=[pltpu.VMEM((SAMPLE_KEYS, KV_COLS), F32), pltpu.VMEM((SAMPLE_KEYS, KV_COLS), F32)],
        compiler_params=pltpu.CompilerParams(
            dimension_semantics=("arbitrary",), vmem_limit_bytes=VMEM_LIMIT_BYTES),
        name=f"sample_attention_l{layer}",
    )(q, kn, vn, cache_k, cache_v, stab)


def _smix_out_kernel(xs_ref, lru_ref, attn_ref, gate_ref, vec_ref, w_lo_ref, w_ao_ref, w_out_ref,
                     xmid_ref, a_s, *, nb, nt):
    for t in range(nt):
        a_s[t * nb:(t + 1) * nb, :] = attn_ref[:, t * Q_COLS:(t + 1) * Q_COLS]
    merged = (gate_ref[:, 0:D_MODEL] * _dot(lru_ref[...], w_lo_ref[...])
              + gate_ref[:, D_MODEL:2 * D_MODEL] * _dot(a_s[...], w_ao_ref[...]))
    m = _dot(merged.astype(BF16), w_out_ref[...])
    mn = _rmsnorm(m, vec_ref[V_NORM_MIX_POST:V_NORM_MIX_POST + 1, :])
    for t in range(nt):
        rows = slice(t * nb, (t + 1) * nb)
        xmid_ref[rows, :] = xs_ref[:, t * D_MODEL:(t + 1) * D_MODEL] + mn[rows]


def _sample_mixer_out(xs, lru, attn, gates, vecs, w_lo, w_ao, w_out, layer):
    nb = xs.shape[0]
    nt = xs.shape[1] // D_MODEL
    kern = functools.partial(_smix_out_kernel, nb=nb, nt=nt)
    whole = lambda shape: pl.BlockSpec(shape, lambda i: (0,) * len(shape))
    return pl.pallas_call(
        kern,
        grid=(1,),
        in_specs=[
            whole(xs.shape), whole(lru.shape), whole(attn.shape), whole(gates.shape),
            _const_spec((None, VEC_ROWS, D_MODEL), (layer, 0, 0)),
            _const_spec((None, D_RNN, D_MODEL), (layer, 0, 0)),
            _const_spec((None, Q_COLS, D_MODEL), (layer, 0, 0)),
            _const_spec((None, D_MODEL, D_MODEL), (layer, 0, 0)),
        ],
        out_specs=whole((nt * nb, D_MODEL)),
        out_shape=jax.ShapeDtypeStruct((nt * nb, D_MODEL), F32),
        scratch_shapes=[pltpu.VMEM((nt * nb, Q_COLS), BF16)],
        compiler_params=pltpu.CompilerParams(
            dimension_semantics=("arbitrary",), vmem_limit_bytes=VMEM_LIMIT_BYTES),
        name=f"sample_mixer_out_l{layer}",
    )(xs, lru, attn, gates, vecs, w_lo, w_ao, w_out)


def _sffn_kernel(x_ref, vec_ref, fcv_v_ref, fcv_g_ref, pv0_ref, pv1_ref, pg0_ref, pg1_ref,
                 wv_ref, wg_ref, wd_ref, ys_ref, fstate_ref, h_s, acc_s, *, nb, nt):
    c = pl.program_id(0)

    @pl.when(c == 0)
    def _():
        h_s[...] = _rmsnorm(x_ref[...], vec_ref[V_NORM_FFN_PRE:V_NORM_FFN_PRE + 1, :]).astype(BF16)
        acc_s[...] = jnp.zeros(acc_s.shape, F32)

    hb = h_s[...]
    conv = []
    for part, (w_ref, fcv_ref, prev) in enumerate(
            ((wv_ref, fcv_v_ref, (pv0_ref, pv1_ref)), (wg_ref, fcv_g_ref, (pg0_ref, pg1_ref)))):
        u = _dot(hb, w_ref[...])
        uu = [prev[0][...], prev[1][...]] + [u[t * nb:(t + 1) * nb] for t in range(nt)]
        ys = []
        for t in range(nt):
            y = fcv_ref[CONV_FF:CONV_FF + 1, :]
            for j in range(CONV_FF):
                y = y + fcv_ref[j:j + 1, :] * uu[t + j]
            ys.append(y)
        conv.append(jnp.concatenate(ys, axis=0))
        for j in range(CONV_FF - 1):
            fstate_ref[j * 2 + part] = uu[nt + j]
    act = (_gelu_tanh(conv[1]) * conv[0]).astype(BF16)
    acc_s[...] += _dot(act, wd_ref[...])

    @pl.when(c == pl.num_programs(0) - 1)
    def _():
        y = x_ref[...] + _rmsnorm(acc_s[...], vec_ref[V_NORM_FFN_POST:V_NORM_FFN_POST + 1, :])
        for t in range(nt):
            ys_ref[:, t * D_MODEL:(t + 1) * D_MODEL] = y[t * nb:(t + 1) * nb]


def _sample_ffn(xmid, fprev, vecs, fcv, w_up, w_down, layer, nb):
    nt = xmid.shape[0] // nb
    ck = FF_CHUNK
    nck = D_FF // ck
    kern = functools.partial(_sffn_kernel, nb=nb, nt=nt)
    prev_spec = lambda j, part: pl.BlockSpec(
        (None, nb, ck), lambda c, j=j, part=part: (layer, 0, (j * 2 + part) * nck + c))
    fcv_spec = lambda part: pl.BlockSpec((None, SUBLANES, ck), lambda c, part=part: (layer, 0, part * nck + c))
    up_spec = lambda part: pl.BlockSpec((None, D_MODEL, ck), lambda c, part=part: (layer, 0, part * nck + c))
    return pl.pallas_call(
        kern,
        grid=(nck,),
        in_specs=[
            _const_spec((nt * nb, D_MODEL), (0, 0)),
            _const_spec((None, VEC_ROWS, D_MODEL), (layer, 0, 0)),
            fcv_spec(0), fcv_spec(1),
            prev_spec(0, 0), prev_spec(1, 0), prev_spec(0, 1), prev_spec(1, 1),
            up_spec(0), up_spec(1),
            pl.BlockSpec((None, ck, D_MODEL), lambda c: (layer, c, 0)),
        ],
        out_specs=(pl.BlockSpec((nb, nt * D_MODEL), lambda c: (0, 0)),
                   pl.BlockSpec((2 * (CONV_FF - 1), nb, ck), lambda c: (0, 0, c))),
        out_shape=(jax.ShapeDtypeStruct((nb, nt * D_MODEL), F32),
                   jax.ShapeDtypeStruct((2 * (CONV_FF - 1), nb, D_FF), F32)),
        scratch_shapes=[pltpu.VMEM((nt * nb, D_MODEL), BF16), pltpu.VMEM((nt * nb, D_MODEL), F32)],
        compiler_params=pltpu.CompilerParams(
            dimension_semantics=("arbitrary",), vmem_limit_bytes=VMEM_LIMIT_BYTES),
        name=f"sample_ffn_l{layer}",
    )(xmid, vecs, fcv, fcv, fprev, fprev, fprev, fprev, w_up, w_up, w_down)


def kernel(x_prompt, x_sample, state_lru_h, state_lru_conv, cache_win_k, cache_win_v, state_ffn_conv,
           norm_mix_pre, norm_mix_post, norm_ffn_pre, norm_ffn_post, w_in, conv_lru_w, conv_lru_b,
           lru_wr, lru_br, lru_wi, lru_bi, lru_lambda, w_lru_o, w_attn_o, w_out, attn_sink, rel_bias,
           w_up, ffn_conv_w, ffn_conv_b, w_down):
    nb, nt, _ = x_sample.shape
    bp = x_prompt.shape[0]

    row = lambda v: v[:, None, :]
    vecs = jnp.concatenate(
        [row(norm_mix_pre), row(norm_mix_post), row(conv_lru_b), row(lru_br), row(lru_bi), row(lru_lambda),
         conv_lru_w, row(norm_ffn_pre), row(norm_ffn_post),
         jnp.zeros((DEPTH, VEC_ROWS - 12, D_MODEL), F32)], axis=1)
    fcv = jnp.concatenate(
        [ffn_conv_w, row(ffn_conv_b), jnp.zeros((DEPTH, SUBLANES - CONV_FF - 1, 2 * D_FF), F32)], axis=1)
    w_in_b = w_in.astype(BF16)
    wri_b = jnp.concatenate([lru_wr, lru_wi], axis=-1).astype(BF16)
    w_lo_b = w_lru_o.astype(BF16)
    w_ao_b = w_attn_o.astype(BF16)
    w_out_b = w_out.astype(BF16)
    w_up_b = w_up.astype(BF16)
    w_down_b = w_down.astype(BF16)

    ptab, stab = _bias_tables(rel_bias, attn_sink)

    xs = x_sample.reshape(nb, nt * D_MODEL)
    lru_conv_prev = state_lru_conv.reshape(DEPTH, nb, (CONV_LRU - 1) * D_RNN)
    cache_k = cache_win_k.reshape(DEPTH, nb, WINDOW, KV_COLS)
    cache_v = cache_win_v.reshape(DEPTH, nb, WINDOW, KV_COLS)
    ffn_prev = state_ffn_conv.reshape(DEPTH, nb, (CONV_FF - 1) * 2 * D_FF)

    yp = x_prompt
    p_h, p_c, p_k, p_v, p_f = [], [], [], [], []
    s_h, s_c, s_k, s_v, s_f = [], [], [], [], []
    for l in range(DEPTH):
        yp, h1, c1, k1, v1 = _prompt_mixer(yp, vecs, w_in_b, wri_b, w_lo_b, w_ao_b, w_out_b, ptab, attn_sink, l)
        yp, f1 = _prompt_ffn(yp, vecs, fcv, w_up_b, w_down_b, l)
        p_h.append(h1.reshape(bp, D_RNN))
        p_c.append(c1)
        p_k.append(k1.reshape(bp, WINDOW, N_KV, HEAD_DIM))
        p_v.append(v1.reshape(bp, WINDOW, N_KV, HEAD_DIM))
        p_f.append(f1)

        q, kn, vn, lru, gates, h2, c2 = _sample_mixer_in(xs, state_lru_h, lru_conv_prev, vecs, w_in_b, wri_b, l)
        attn, k2, v2 = _sample_attention(
            q.reshape(nb, nt * N_HEADS, HEAD_DIM), kn.reshape(nb, nt, KV_COLS), vn.reshape(nb, nt, KV_COLS),
            cache_k, cache_v, stab, l)
        xmid = _sample_mixer_out(xs, lru, attn.reshape(nb, nt * Q_COLS), gates, vecs, w_lo_b, w_ao_b, w_out_b, l)
        xs, f2 = _sample_ffn(xmid, ffn_prev, vecs, fcv, w_up_b, w_down_b, l, nb)
        s_h.append(h2)
        s_c.append(c2.reshape(nb, CONV_LRU - 1, D_RNN))
        s_k.append(k2.reshape(nb, WINDOW, N_KV, HEAD_DIM))
        s_v.append(v2.reshape(nb, WINDOW, N_KV, HEAD_DIM))
        s_f.append(jnp.transpose(f2.reshape(CONV_FF - 1, 2, nb, D_FF), (2, 0, 1, 3)).reshape(nb, CONV_FF - 1, 2 * D_FF))

    return (yp, xs.reshape(nb, nt, D_MODEL),
            jnp.stack(p_h), jnp.stack(p_c), jnp.stack(p_k), jnp.stack(p_v), jnp.stack(p_f),
            jnp.stack(s_h), jnp.stack(s_c), jnp.stack(s_k), jnp.stack(s_v), jnp.stack(s_f))
```

```python
import functools
import math

import numpy as np
import jax
import jax.numpy as jnp
from jax import lax
from jax.experimental import pallas as pl
from jax.experimental.pallas import tpu as pltpu

D_MODEL = 1024
DEPTH = 2
PAST_LEN = 16384
D_RNN = D_MODEL
N_LRU_BLOCKS = 8
LRU_BLOCK = D_RNN // N_LRU_BLOCKS
CONV_LRU = 4
LRU_C = 8.0
N_HEADS = 8
N_KV = 2
GROUP = N_HEADS // N_KV
HEAD_DIM = D_MODEL // N_HEADS
WINDOW = 128
N_BUCKETS = 32
MAX_EXACT = N_BUCKETS // 2
MAX_DISTANCE = 128
D_FF = 4 * D_MODEL
CONV_FF = 3
EPS = 1e-6
Q_COLS = N_HEADS * HEAD_DIM
KV_COLS = N_KV * HEAD_DIM
IN_COLS = D_RNN + Q_COLS + 2 * KV_COLS + 2 * D_MODEL
C_Q = D_RNN
C_K = C_Q + Q_COLS
C_V = C_K + KV_COLS
C_G = C_V + KV_COLS

F32 = jnp.float32
BF16 = jnp.bfloat16

SUBLANES = 8
LANES = 128
VMEM_LIMIT_BYTES = 56 * 1024 * 1024

V_NORM_MIX_PRE, V_NORM_MIX_POST, V_CONV_B, V_BR, V_BI, V_LAMBDA, V_CONV_W = 0, 1, 2, 3, 4, 5, 6
V_NORM_FFN_PRE, V_NORM_FFN_POST = 10, 11
VEC_ROWS = 16

PROMPT_TILE = 256
FF_CHUNK = 512
SAMPLE_KEYS = 136
SINK_COL = WINDOW + 4
SAMPLE_BATCH_TILE = 16


def _bucket_thresholds():
    d = np.arange(0, 2 * WINDOW)
    nf = np.maximum(d, 1).astype(np.float64)
    large = MAX_EXACT + (np.log(nf / MAX_EXACT) / math.log(MAX_DISTANCE / MAX_EXACT)
                         * (N_BUCKETS - MAX_EXACT)).astype(np.int64)
    bucket = np.where(d < MAX_EXACT, d, np.minimum(large, N_BUCKETS - 1))
    return tuple(int(d[bucket >= b].min()) for b in range(1, N_BUCKETS))


_BUCKET_THRESHOLDS = _bucket_thresholds()


def _dot(a, b):
    return jnp.dot(a, b, preferred_element_type=F32)


def _dot_nt(a, b):
    return lax.dot_general(a, b, (((1,), (1,)), ((), ())), preferred_element_type=F32)


def _rmsnorm(x, g):
    return x * lax.rsqrt(jnp.mean(x * x, axis=-1, keepdims=True) + EPS) * g


def _sigmoid(x):
    return 1.0 / (1.0 + jnp.exp(-x))


def _gelu_tanh(x):
    return 0.5 * x * (1.0 + jnp.tanh(math.sqrt(2.0 / math.pi) * (x + 0.044715 * (x * x * x))))


def _const_spec(block_shape, index):
    return pl.BlockSpec(block_shape, lambda *_: index, pipeline_mode=pl.Buffered(1))


def _with_state_aliases(kern, n_in, in_specs, args, prev_states, first_state_out):
    if prev_states is None:
        return kern, list(in_specs), list(args), {}
    n = len(prev_states)

    def body(*refs):
        return kern(*refs[:n_in], *refs[n_in + n:])

    return (body, list(in_specs) + [pl.BlockSpec(memory_space=pl.ANY)] * n, list(args) + list(prev_states),
            {n_in + i: first_state_out + i for i in range(n)})


def _bucket_of(d):
    n = jnp.maximum(d, 0)
    bucket = jnp.zeros(d.shape, jnp.int32)
    for thr in _BUCKET_THRESHOLDS:
        bucket = bucket + jnp.where(n >= thr, 1, 0)
    return bucket


def _table_kernel(rel_ref, sink_ref, pt_ref, st_ref):
    qi = lax.broadcasted_iota(jnp.int32, (WINDOW, 2 * WINDOW), 0)
    kj = lax.broadcasted_iota(jnp.int32, (WINDOW, 2 * WINDOW), 1)
    d = qi + WINDOW - kj
    bucket = _bucket_of(d)
    in_band = jnp.where(d >= 0, jnp.where(d < WINDOW, 1, 0), 0)
    cur_only = jnp.where(kj >= WINDOW, in_band, 0)
    for h in range(N_HEADS):
        val = jnp.zeros(d.shape, F32)
        for b in range(N_BUCKETS):
            val = jnp.where(bucket == b, rel_ref[b, h], val)
        pt_ref[1, h] = jnp.where(in_band == 1, val, -jnp.inf)
        pt_ref[0, h] = jnp.where(cur_only == 1, val, -jnp.inf)

    r = lax.broadcasted_iota(jnp.int32, (4 * N_HEADS, SAMPLE_KEYS), 0)
    j = lax.broadcasted_iota(jnp.int32, (4 * N_HEADS, SAMPLE_KEYS), 1)
    t = lax.shift_right_logical(r, 3)
    hh = lax.bitwise_and(r, N_HEADS - 1)
    d = t + WINDOW - j
    bucket = _bucket_of(d)
    in_band = jnp.where(d >= 0, jnp.where(d < WINDOW, 1, 0), 0)
    val = jnp.zeros(d.shape, F32)
    for h in range(N_HEADS):
        hval = jnp.zeros(d.shape, F32)
        for b in range(N_BUCKETS):
            hval = jnp.where(bucket == b, rel_ref[b, h], hval)
        val = jnp.where(hh == h, hval, val)
    val = jnp.where(in_band == 1, val, -jnp.inf)
    for l in range(DEPTH):
        sk = jnp.zeros(d.shape, F32)
        for h in range(N_HEADS):
            sk = jnp.where(hh == h, sink_ref[l, h], sk)
        st_ref[l] = jnp.where(j == SINK_COL, sk, val)


def _bias_tables(rel_bias, attn_sink):
    smem = pl.BlockSpec(memory_space=pltpu.SMEM)
    return pl.pallas_call(
        _table_kernel,
        out_shape=(jax.ShapeDtypeStruct((2, N_HEADS, WINDOW, 2 * WINDOW), F32),
                   jax.ShapeDtypeStruct((DEPTH, 4 * N_HEADS, SAMPLE_KEYS), F32)),
        in_specs=[smem, smem],
        name="bias_tables",
    )(rel_bias, attn_sink)


def _lru_coeffs(xc, vec_ref, wri_ref, n):
    cb = slice(n * LRU_BLOCK, (n + 1) * LRU_BLOCK)
    rw = _dot(xc.astype(BF16), wri_ref[n])
    r = _sigmoid(rw[:, :LRU_BLOCK] + vec_ref[V_BR:V_BR + 1, cb])
    i = _sigmoid(rw[:, LRU_BLOCK:] + vec_ref[V_BI:V_BI + 1, cb])
    z = -vec_ref[V_LAMBDA:V_LAMBDA + 1, cb]
    softplus = jnp.maximum(z, 0.0) + jnp.log1p(jnp.exp(-jnp.abs(z)))
    log_a = (-LRU_C * softplus) * r
    a = jnp.exp(log_a)
    mult = jnp.sqrt(jnp.maximum(1.0 - a * a, 0.0))
    return a, i * xc, mult


def _pmix_kernel(x_ref, vec_ref, w_in_ref, wri_ref, w_lo_ref, w_ao_ref, w_out_ref, tab_ref, sink_ref,
                 y_ref, hlast_ref, cstate_ref, kstate_ref, vstate_ref,
                 h_s, xr_s, q_s, k_s, v_s, lru_s, attn_s, hc_s, hl_s, *, layer, tc):
    t = pl.program_id(1)
    batch_row = lax.broadcasted_iota(jnp.int32, (hl_s.shape[0], LRU_BLOCK), 0)

    @pl.when(jnp.logical_and(t == 0, pl.program_id(0) == 0))
    def _():
        hl_s[...] = jnp.zeros(hl_s.shape, F32)

    @pl.when(t == 0)
    def _():
        xr_s[0:SUBLANES, :] = jnp.zeros((SUBLANES, D_RNN), F32)
        k_s[0:WINDOW, :] = jnp.zeros((WINDOW, KV_COLS), BF16)
        v_s[0:WINDOW, :] = jnp.zeros((WINDOW, KV_COLS), BF16)
        hc_s[...] = jnp.zeros(hc_s.shape, F32)

    h_s[...] = _rmsnorm(x_ref[...], vec_ref[V_NORM_MIX_PRE:V_NORM_MIX_PRE + 1, :]).astype(BF16)
    hb = h_s[...]
    xr_s[SUBLANES:SUBLANES + tc, :] = _dot(hb, w_in_ref[:, 0:C_Q])
    q_s[...] = (_dot(hb, w_in_ref[:, C_Q:C_K]) * (HEAD_DIM ** -0.5)).astype(BF16)
    kf = _dot(hb, w_in_ref[:, C_K:C_V])
    vf = _dot(hb, w_in_ref[:, C_V:C_G])
    k_s[WINDOW:WINDOW + tc, :] = kf.astype(BF16)
    v_s[WINDOW:WINDOW + tc, :] = vf.astype(BF16)
    for g in range(N_KV):
        kstate_ref[:, g, :] = kf[tc - WINDOW:tc, g * HEAD_DIM:(g + 1) * HEAD_DIM]
        vstate_ref[:, g, :] = vf[tc - WINDOW:tc, g * HEAD_DIM:(g + 1) * HEAD_DIM]
    cstate_ref[...] = xr_s[SUBLANES + tc - (CONV_LRU - 1):SUBLANES + tc, :]

    sub = lax.broadcasted_iota(jnp.int32, (SUBLANES, LRU_BLOCK), 0)
    seq_start = (sub + t) == 0
    for n in range(N_LRU_BLOCKS):
        cb = slice(n * LRU_BLOCK, (n + 1) * LRU_BLOCK)
        xc = vec_ref[V_CONV_B:V_CONV_B + 1, cb]
        for j in range(CONV_LRU):
            off = SUBLANES - (CONV_LRU - 1) + j
            xc = xc + vec_ref[V_CONV_W + j:V_CONV_W + j + 1, cb] * xr_s[off:off + tc, cb]
        a, ix, mult = _lru_coeffs(xc, vec_ref, wri_ref, n)
        b = mult * ix
        carry = hc_s[0:1, cb]
        hs = []
        for g in range(tc // SUBLANES):
            rows = slice(g * SUBLANES, (g + 1) * SUBLANES)
            ag, bg = a[rows], b[rows]
            if g == 0:
                bg = jnp.where(seq_start, ix[rows], bg)
            for s in (1, 2, 4):
                ash = jnp.where(sub >= s, pltpu.roll(ag, s, 0), 1.0)
                bsh = jnp.where(sub >= s, pltpu.roll(bg, s, 0), 0.0)
                bg = ag * bsh + bg
                ag = ag * ash
            hg = ag * carry + bg
            carry = hg[SUBLANES - 1:SUBLANES, :]
            hs.append(hg)
        lru_s[:, cb] = jnp.concatenate(hs, axis=0).astype(BF16)
        hc_s[0:1, cb] = carry
        hl_s[:, cb] = jnp.where(batch_row == pl.program_id(0), carry, hl_s[:, cb])
    hlast_ref[...] = hl_s[...]

    first = jnp.where(t == 0, 0, 1)
    for j in range(tc // WINDOW):
        qrows = slice(j * WINDOW, (j + 1) * WINDOW)
        krows = slice(j * WINDOW, (j + 2) * WINDOW)
        variant = first if j == 0 else 1
        for g in range(N_KV):
            kvc = slice(g * HEAD_DIM, (g + 1) * HEAD_DIM)
            q4 = jnp.concatenate(
                [q_s[qrows, (g * GROUP + hg) * HEAD_DIM:(g * GROUP + hg + 1) * HEAD_DIM] for hg in range(GROUP)],
                axis=0)
            s = _dot_nt(q4, k_s[krows, kvc])
            ps, invs = [], []
            for hg in range(GROUP):
                head = g * GROUP + hg
                sh = s[hg * WINDOW:(hg + 1) * WINDOW] + tab_ref[variant, head]
                sk = sink_ref[layer, head]
                m = jnp.maximum(jnp.max(sh, axis=-1, keepdims=True), sk)
                p = jnp.exp(sh - m)
                den = jnp.sum(p, axis=-1, keepdims=True) + jnp.exp(sk - m)
                ps.append(p.astype(BF16))
                invs.append(1.0 / den)
            o4 = _dot(jnp.concatenate(ps, axis=0), v_s[krows, kvc])
            for hg in range(GROUP):
                head = g * GROUP + hg
                attn_s[qrows, head * HEAD_DIM:(head + 1) * HEAD_DIM] = (
                    o4[hg * WINDOW:(hg + 1) * WINDOW] * invs[hg]).astype(BF16)

    xr_s[0:SUBLANES, :] = xr_s[tc:tc + SUBLANES, :]
    k_s[0:WINDOW, :] = k_s[tc:tc + WINDOW, :]
    v_s[0:WINDOW, :] = v_s[tc:tc + WINDOW, :]

    g_lru = _sigmoid(_dot(hb, w_in_ref[:, C_G:C_G + D_MODEL]))
    g_att = _sigmoid(_dot(hb, w_in_ref[:, C_G + D_MODEL:IN_COLS]))
    merged = g_lru * _dot(lru_s[...], w_lo_ref[...]) + g_att * _dot(attn_s[...], w_ao_ref[...])
    m = _dot(merged.astype(BF16), w_out_ref[...])
    y_ref[...] = x_ref[...] + _rmsnorm(m, vec_ref[V_NORM_MIX_POST:V_NORM_MIX_POST + 1, :])


def _prompt_mixer(x, vecs, w_in, wri, w_lo, w_ao, w_out, tab, sink, layer, prev_states):
    bsz, seq, _ = x.shape
    tc = PROMPT_TILE
    kern = functools.partial(_pmix_kernel, layer=layer, tc=tc)
    tile = pl.BlockSpec((None, tc, D_MODEL), lambda b, t: (b, t, 0))
    kv_state = pl.BlockSpec((None, None, WINDOW, N_KV, HEAD_DIM), lambda b, t: (layer, b, 0, 0, 0))
    in_specs = [
        tile,
        _const_spec((None, VEC_ROWS, D_MODEL), (layer, 0, 0)),
        _const_spec((None, D_MODEL, IN_COLS), (layer, 0, 0)),
        _const_spec((None, N_LRU_BLOCKS, LRU_BLOCK, 2 * LRU_BLOCK), (layer, 0, 0, 0)),
        _const_spec((None, D_RNN, D_MODEL), (layer, 0, 0)),
        _const_spec((None, Q_COLS, D_MODEL), (layer, 0, 0)),
        _const_spec((None, D_MODEL, D_MODEL), (layer, 0, 0)),
        _const_spec((2, N_HEADS, WINDOW, 2 * WINDOW), (0, 0, 0, 0)),
        pl.BlockSpec(memory_space=pltpu.SMEM),
    ]
    args = [x, vecs, w_in, wri, w_lo, w_ao, w_out, tab, sink]
    kern, in_specs, args, aliases = _with_state_aliases(kern, len(args), in_specs, args, prev_states, 1)
    return pl.pallas_call(
        kern,
        grid=(bsz, seq // tc),
        in_specs=in_specs,
        out_specs=(tile,
                   pl.BlockSpec((None, bsz, D_RNN), lambda b, t: (layer, 0, 0)),
                   pl.BlockSpec((None, None, CONV_LRU - 1, D_RNN), lambda b, t: (layer, b, 0, 0)),
                   kv_state, kv_state),
        out_shape=(jax.ShapeDtypeStruct((bsz, seq, D_MODEL), F32),
                   jax.ShapeDtypeStruct((DEPTH, bsz, D_RNN), F32),
                   jax.ShapeDtypeStruct((DEPTH, bsz, CONV_LRU - 1, D_RNN), F32),
                   jax.ShapeDtypeStruct((DEPTH, bsz, WINDOW, N_KV, HEAD_DIM), F32),
                   jax.ShapeDtypeStruct((DEPTH, bsz, WINDOW, N_KV, HEAD_DIM), F32)),
        input_output_aliases=aliases,
        scratch_shapes=[
            pltpu.VMEM((tc, D_MODEL), BF16),
            pltpu.VMEM((SUBLANES + tc, D_RNN), F32),
            pltpu.VMEM((tc, Q_COLS), BF16),
            pltpu.VMEM((WINDOW + tc, KV_COLS), BF16),
            pltpu.VMEM((WINDOW + tc, KV_COLS), BF16),
            pltpu.VMEM((tc, D_RNN), BF16),
            pltpu.VMEM((tc, Q_COLS), BF16),
            pltpu.VMEM((SUBLANES, D_RNN), F32),
            pltpu.VMEM((bsz, D_RNN), F32),
        ],
        compiler_params=pltpu.CompilerParams(
            dimension_semantics=("arbitrary", "arbitrary"), vmem_limit_bytes=VMEM_LIMIT_BYTES),
        name=f"prompt_mixer_l{layer}",
    )(*args)


def _pffn_kernel(x_ref, vec_ref, fcv_ref, w_up_ref, w_down_ref, y_ref, fstate_ref,
                 h_s, u_s, tail_s, act_s, *, tc):
    t = pl.program_id(1)

    @pl.when(t == 0)
    def _():
        tail_s[...] = jnp.zeros(tail_s.shape, F32)

    h_s[...] = _rmsnorm(x_ref[...], vec_ref[V_NORM_FFN_PRE:V_NORM_FFN_PRE + 1, :]).astype(BF16)
    hb = h_s[...]
    for c in range(D_FF // FF_CHUNK):
        conv = []
        for part in range(2):
            cols = slice(part * D_FF + c * FF_CHUNK, part * D_FF + (c + 1) * FF_CHUNK)
            us = u_s.at[part]
            us[0:SUBLANES, :] = tail_s[:, cols]
            us[SUBLANES:SUBLANES + tc, :] = _dot(hb, w_up_ref[:, cols])
            y = fcv_ref[CONV_FF:CONV_FF + 1, cols]
            for j in range(CONV_FF):
                off = SUBLANES - (CONV_FF - 1) + j
                y = y + fcv_ref[j:j + 1, cols] * us[off:off + tc, :]
            conv.append(y)
            tail_s[:, cols] = us[tc:tc + SUBLANES, :]
            fstate_ref[:, cols] = us[SUBLANES + tc - (CONV_FF - 1):SUBLANES + tc, :]
        act_s[:, c * FF_CHUNK:(c + 1) * FF_CHUNK] = (_gelu_tanh(conv[1]) * conv[0]).astype(BF16)
    f = _dot(act_s[...], w_down_ref[...])
    y_ref[...] = x_ref[...] + _rmsnorm(f, vec_ref[V_NORM_FFN_POST:V_NORM_FFN_POST + 1, :])


def _prompt_ffn(x, vecs, fcv, w_up, w_down, layer, prev_states):
    bsz, seq, _ = x.shape
    tc = PROMPT_TILE
    kern = functools.partial(_pffn_kernel, tc=tc)
    tile = pl.BlockSpec((None, tc, D_MODEL), lambda b, t: (b, t, 0))
    in_specs = [
        tile,
        _const_spec((None, VEC_ROWS, D_MODEL), (layer, 0, 0)),
        _const_spec((None, SUBLANES, 2 * D_FF), (layer, 0, 0)),
        _const_spec((None, D_MODEL, 2 * D_FF), (layer, 0, 0)),
        _const_spec((None, D_FF, D_MODEL), (layer, 0, 0)),
    ]
    args = [x, vecs, fcv, w_up, w_down]
    kern, in_specs, args, aliases = _with_state_aliases(kern, len(args), in_specs, args, prev_states, 1)
    return pl.pallas_call(
        kern,
        grid=(bsz, seq // tc),
        in_specs=in_specs,
        out_specs=(tile, pl.BlockSpec((None, None, CONV_FF - 1, 2 * D_FF), lambda b, t: (layer, b, 0, 0))),
        out_shape=(jax.ShapeDtypeStruct((bsz, seq, D_MODEL), F32),
                   jax.ShapeDtypeStruct((DEPTH, bsz, CONV_FF - 1, 2 * D_FF), F32)),
        input_output_aliases=aliases,
        scratch_shapes=[
            pltpu.VMEM((tc, D_MODEL), BF16),
            pltpu.VMEM((2, SUBLANES + tc, FF_CHUNK), F32),
            pltpu.VMEM((SUBLANES, 2 * D_FF), F32),
            pltpu.VMEM((tc, D_FF), BF16),
        ],
        compiler_params=pltpu.CompilerParams(
            dimension_semantics=("arbitrary", "arbitrary"), vmem_limit_bytes=VMEM_LIMIT_BYTES),
        name=f"prompt_ffn_l{layer}",
    )(*args)


def _smix_in_kernel(xs_ref, h0_ref, cprev_ref, vec_ref, w_in_ref, wri_ref,
                    q_ref, k_ref, v_ref, lru_ref, gate_ref, hlast_ref, cstate_ref,
                    h_s, xr_s, *, nb, nt):
    for t in range(nt):
        h_s[t * nb:(t + 1) * nb, :] = _rmsnorm(
            xs_ref[:, t, :], vec_ref[V_NORM_MIX_PRE:V_NORM_MIX_PRE + 1, :]).astype(BF16)
    hb = h_s[...]
    xr_s[...] = _dot(hb, w_in_ref[:, 0:C_Q])
    qf = _dot(hb, w_in_ref[:, C_Q:C_K]) * (HEAD_DIM ** -0.5)
    kf = _dot(hb, w_in_ref[:, C_K:C_V])
    vf = _dot(hb, w_in_ref[:, C_V:C_G])
    for t in range(nt):
        rows = slice(t * nb, (t + 1) * nb)
        q_ref[:, t * Q_COLS:(t + 1) * Q_COLS] = qf[rows].astype(BF16)
        k_ref[:, t, :] = kf[rows]
        v_ref[:, t, :] = vf[rows]
    gate_ref[...] = _sigmoid(_dot(hb, w_in_ref[:, C_G:IN_COLS]))

    npre = CONV_LRU - 1
    for t in range(nt - npre, nt):
        cstate_ref[:, t - (nt - npre), :] = xr_s[t * nb:(t + 1) * nb, :]

    for n in range(N_LRU_BLOCKS):
        cb = slice(n * LRU_BLOCK, (n + 1) * LRU_BLOCK)
        xx = [cprev_ref[:, j, cb] for j in range(npre)]
        xx += [xr_s[t * nb:(t + 1) * nb, cb] for t in range(nt)]
        xcs = []
        for t in range(nt):
            xc = vec_ref[V_CONV_B:V_CONV_B + 1, cb]
            for j in range(CONV_LRU):
                xc = xc + vec_ref[V_CONV_W + j:V_CONV_W + j + 1, cb] * xx[t + j]
            xcs.append(xc)
        a, ix, mult = _lru_coeffs(jnp.concatenate(xcs, axis=0), vec_ref, wri_ref, n)
        b = mult * ix
        h = h0_ref[:, cb]
        for t in range(nt):
            rows = slice(t * nb, (t + 1) * nb)
            h = a[rows] * h + b[rows]
            lru_ref[rows, cb] = h.astype(BF16)
        hlast_ref[:, cb] = h


def _sample_mixer_in(xs, h0, cprev, vecs, w_in, wri, layer, prev_states):
    nb, nt, _ = xs.shape
    assert PAST_LEN > 0
    kern = functools.partial(_smix_in_kernel, nb=nb, nt=nt)
    whole = lambda shape: pl.BlockSpec(shape, lambda i: (0,) * len(shape))
    conv_state = pl.BlockSpec((None, nb, CONV_LRU - 1, D_RNN), lambda i: (layer, 0, 0, 0))
    in_specs = [
        whole((nb, nt, D_MODEL)),
        pl.BlockSpec((None, nb, D_RNN), lambda i: (layer, 0, 0)),
        conv_state,
        _const_spec((None, VEC_ROWS, D_MODEL), (layer, 0, 0)),
        _const_spec((None, D_MODEL, IN_COLS), (layer, 0, 0)),
        _const_spec((None, N_LRU_BLOCKS, LRU_BLOCK, 2 * LRU_BLOCK), (layer, 0, 0, 0)),
    ]
    args = [xs, h0, cprev, vecs, w_in, wri]
    kern, in_specs, args, aliases = _with_state_aliases(kern, len(args), in_specs, args, prev_states, 5)
    return pl.pallas_call(
        kern,
        grid=(1,),
        in_specs=in_specs,
        out_specs=(whole((nb, nt * Q_COLS)), whole((nb, nt, KV_COLS)), whole((nb, nt, KV_COLS)),
                   whole((nt * nb, D_RNN)), whole((nt * nb, 2 * D_MODEL)),
                   pl.BlockSpec((None, nb, D_RNN), lambda i: (layer, 0, 0)), conv_state),
        out_shape=(jax.ShapeDtypeStruct((nb, nt * Q_COLS), BF16),
                   jax.ShapeDtypeStruct((nb, nt, KV_COLS), F32),
                   jax.ShapeDtypeStruct((nb, nt, KV_COLS), F32),
                   jax.ShapeDtypeStruct((nt * nb, D_RNN), BF16),
                   jax.ShapeDtypeStruct((nt * nb, 2 * D_MODEL), F32),
                   jax.ShapeDtypeStruct((DEPTH, nb, D_RNN), F32),
                   jax.ShapeDtypeStruct((DEPTH, nb, CONV_LRU - 1, D_RNN), F32)),
        input_output_aliases=aliases,
        scratch_shapes=[pltpu.VMEM((nt * nb, D_MODEL), BF16), pltpu.VMEM((nt * nb, D_RNN), F32)],
        compiler_params=pltpu.CompilerParams(
            dimension_semantics=("arbitrary",), vmem_limit_bytes=VMEM_LIMIT_BYTES),
        name=f"sample_mixer_in_l{layer}",
    )(*args)


def _sattn_kernel(q_ref, kn_ref, vn_ref, ck_ref, cv_ref, tab_ref, attn_ref, sk_ref, sv_ref,
                  kc_s, vc_s, *, bt, nt):
    pad = SAMPLE_KEYS - WINDOW
    for g in range(N_KV):
        kc_s[g, WINDOW:SAMPLE_KEYS, :] = jnp.zeros((pad, HEAD_DIM), F32)
        vc_s[g, WINDOW:SAMPLE_KEYS, :] = jnp.zeros((pad, HEAD_DIM), F32)
    row = lax.broadcasted_iota(jnp.int32, (nt * N_HEADS, HEAD_DIM), 0)
    in_group0 = lax.bitwise_and(row, N_HEADS - 1) < GROUP

    def body(b, _):
        qb = q_ref[b]
        outs = []
        for g in range(N_KV):
            kvc = slice(g * HEAD_DIM, (g + 1) * HEAD_DIM)
            kc_s[g, 0:WINDOW, :] = ck_ref[b, :, g, :]
            vc_s[g, 0:WINDOW, :] = cv_ref[b, :, g, :]
            kc_s[g, WINDOW:WINDOW + nt, :] = kn_ref[b, :, kvc]
            vc_s[g, WINDOW:WINDOW + nt, :] = vn_ref[b, :, kvc]
            sk_ref[b, :, g, :] = kc_s[g, nt:nt + WINDOW, :]
            sv_ref[b, :, g, :] = vc_s[g, nt:nt + WINDOW, :]
            s = _dot_nt(qb, kc_s[g].astype(BF16)) + tab_ref[...]
            m = jnp.max(s, axis=-1, keepdims=True)
            p = jnp.exp(s - m)
            den = jnp.sum(p, axis=-1, keepdims=True)
            outs.append(_dot(p.astype(BF16), vc_s[g].astype(BF16)) * (1.0 / den))
        attn_ref[b] = jnp.where(in_group0, outs[0], outs[1]).astype(BF16)
        return 0

    lax.fori_loop(0, bt, body, 0)


def _sample_attention(q, kn, vn, cache_k, cache_v, stab, layer, prev_states):
    nb, rows, _ = q.shape
    nt = rows // N_HEADS
    bt = SAMPLE_BATCH_TILE
    kern = functools.partial(_sattn_kernel, bt=bt, nt=nt)
    cache_spec = pl.BlockSpec((None, bt, WINDOW, N_KV, HEAD_DIM), lambda i: (layer, i, 0, 0, 0))
    new_spec = pl.BlockSpec((bt, nt, KV_COLS), lambda i: (i, 0, 0))
    q_spec = pl.BlockSpec((bt, rows, HEAD_DIM), lambda i: (i, 0, 0))
    in_specs = [q_spec, new_spec, new_spec, cache_spec, cache_spec,
                _const_spec((None, rows, SAMPLE_KEYS), (layer, 0, 0))]
    args = [q, kn, vn, cache_k, cache_v, stab]
    kern, in_specs, args, aliases = _with_state_aliases(kern, len(args), in_specs, args, prev_states, 1)
    return pl.pallas_call(
        kern,
        grid=(nb // bt,),
        in_specs=in_specs,
        out_specs=(q_spec, cache_spec, cache_spec),
        out_shape=(jax.ShapeDtypeStruct((nb, rows, HEAD_DIM), BF16),
                   jax.ShapeDtypeStruct((DEPTH, nb, WINDOW, N_KV, HEAD_DIM), F32),
                   jax.ShapeDtypeStruct((DEPTH, nb, WINDOW, N_KV, HEAD_DIM), F32)),
        input_output_aliases=aliases,
        scratch_shapes=[pltpu.VMEM((N_KV, SAMPLE_KEYS, HEAD_DIM), F32),
                        pltpu.VMEM((N_KV, SAMPLE_KEYS, HEAD_DIM), F32)],
        compiler_params=pltpu.CompilerParams(
            dimension_semantics=("arbitrary",), vmem_limit_bytes=VMEM_LIMIT_BYTES),
        name=f"sample_attention_l{layer}",
    )(*args)


def _smix_out_kernel(xs_ref, lru_ref, attn_ref, gate_ref, vec_ref, w_lo_ref, w_ao_ref, w_out_ref,
                     xmid_ref, a_s, *, nb, nt):
    for t in range(nt):
        a_s[t * nb:(t + 1) * nb, :] = attn_ref[:, t * Q_COLS:(t + 1) * Q_COLS]
    merged = (gate_ref[:, 0:D_MODEL] * _dot(lru_ref[...], w_lo_ref[...])
              + gate_ref[:, D_MODEL:2 * D_MODEL] * _dot(a_s[...], w_ao_ref[...]))
    m = _dot(merged.astype(BF16), w_out_ref[...])
    mn = _rmsnorm(m, vec_ref[V_NORM_MIX_POST:V_NORM_MIX_POST + 1, :])
    for t in range(nt):
        rows = slice(t * nb, (t + 1) * nb)
        xmid_ref[rows, :] = xs_ref[:, t, :] + mn[rows]


def _sample_mixer_out(xs, lru, attn, gates, vecs, w_lo, w_ao, w_out, layer):
    nb, nt, _ = xs.shape
    kern = functools.partial(_smix_out_kernel, nb=nb, nt=nt)
    whole = lambda shape: pl.BlockSpec(shape, lambda i: (0,) * len(shape))
    return pl.pallas_call(
        kern,
        grid=(1,),
        in_specs=[
            whole(xs.shape), whole(lru.shape), whole(attn.shape), whole(gates.shape),
            _const_spec((None, VEC_ROWS, D_MODEL), (layer, 0, 0)),
            _const_spec((None, D_RNN, D_MODEL), (layer, 0, 0)),
            _const_spec((None, Q_COLS, D_MODEL), (layer, 0, 0)),
            _const_spec((None, D_MODEL, D_MODEL), (layer, 0, 0)),
        ],
        out_specs=whole((nt * nb, D_MODEL)),
        out_shape=jax.ShapeDtypeStruct((nt * nb, D_MODEL), F32),
        scratch_shapes=[pltpu.VMEM((nt * nb, Q_COLS), BF16)],
        compiler_params=pltpu.CompilerParams(
            dimension_semantics=("arbitrary",), vmem_limit_bytes=VMEM_LIMIT_BYTES),
        name=f"sample_mixer_out_l{layer}",
    )(xs, lru, attn, gates, vecs, w_lo, w_ao, w_out)


def _sffn_kernel(x_ref, vec_ref, fcv_ref, prev_ref, w_ref, wd_ref, ys_ref, fstate_ref,
                 h_s, val_s, acc_s, *, nb, nt, nck):
    c = pl.program_id(0)

    @pl.when(c == 0)
    def _():
        h_s[...] = _rmsnorm(x_ref[...], vec_ref[V_NORM_FFN_PRE:V_NORM_FFN_PRE + 1, :]).astype(BF16)
        acc_s[...] = jnp.zeros(acc_s.shape, F32)

    u = _dot(h_s[...], w_ref[...])
    uu = [prev_ref[:, j, :] for j in range(CONV_FF - 1)] + [u[t * nb:(t + 1) * nb] for t in range(nt)]
    ys = []
    for t in range(nt):
        y = fcv_ref[CONV_FF:CONV_FF + 1, :]
        for j in range(CONV_FF):
            y = y + fcv_ref[j:j + 1, :] * uu[t + j]
        ys.append(y)
    conv = jnp.concatenate(ys, axis=0)
    for j in range(CONV_FF - 1):
        fstate_ref[:, j, :] = uu[nt + j]

    @pl.when(c < nck)
    def _():
        val_s[c] = conv

    @pl.when(c >= nck)
    def _():
        act = (_gelu_tanh(conv) * val_s[c - nck]).astype(BF16)
        acc_s[...] += _dot(act, wd_ref[...])

    @pl.when(c == 2 * nck - 1)
    def _():
        y = x_ref[...] + _rmsnorm(acc_s[...], vec_ref[V_NORM_FFN_POST:V_NORM_FFN_POST + 1, :])
        for t in range(nt):
            ys_ref[:, t, :] = y[t * nb:(t + 1) * nb]


def _sample_ffn(xmid, fprev, vecs, fcv, w_up, w_down, layer, nb, prev_states):
    nt = xmid.shape[0] // nb
    ck = FF_CHUNK
    nck = D_FF // ck
    kern = functools.partial(_sffn_kernel, nb=nb, nt=nt, nck=nck)
    state_spec = pl.BlockSpec((None, nb, CONV_FF - 1, ck), lambda c: (layer, 0, 0, c))
    in_specs = [
        _const_spec((nt * nb, D_MODEL), (0, 0)),
        _const_spec((None, VEC_ROWS, D_MODEL), (layer, 0, 0)),
        pl.BlockSpec((None, SUBLANES, ck), lambda c: (layer, 0, c)),
        state_spec,
        pl.BlockSpec((None, D_MODEL, ck), lambda c: (layer, 0, c)),
        pl.BlockSpec((None, ck, D_MODEL), lambda c: (layer, jnp.maximum(c - nck, 0), 0)),
    ]
    args = [xmid, vecs, fcv, fprev, w_up, w_down]
    kern, in_specs, args, aliases = _with_state_aliases(kern, len(args), in_specs, args, prev_states, 1)
    return pl.pallas_call(
        kern,
        grid=(2 * nck,),
        in_specs=in_specs,
        out_specs=(pl.BlockSpec((nb, nt, D_MODEL), lambda c: (0, 0, 0)), state_spec),
        out_shape=(jax.ShapeDtypeStruct((nb, nt, D_MODEL), F32),
                   jax.ShapeDtypeStruct((DEPTH, nb, CONV_FF - 1, 2 * D_FF), F32)),
        input_output_aliases=aliases,
        scratch_shapes=[pltpu.VMEM((nt * nb, D_MODEL), BF16),
                        pltpu.VMEM((nck, nt * nb, ck), F32),
                        pltpu.VMEM((nt * nb, D_MODEL), F32)],
        compiler_params=pltpu.CompilerParams(
            dimension_semantics=("arbitrary",), vmem_limit_bytes=VMEM_LIMIT_BYTES),
        name=f"sample_ffn_l{layer}",
    )(*args)


def kernel(x_prompt, x_sample, state_lru_h, state_lru_conv, cache_win_k, cache_win_v, state_ffn_conv,
           norm_mix_pre, norm_mix_post, norm_ffn_pre, norm_ffn_post, w_in, conv_lru_w, conv_lru_b,
           lru_wr, lru_br, lru_wi, lru_bi, lru_lambda, w_lru_o, w_attn_o, w_out, attn_sink, rel_bias,
           w_up, ffn_conv_w, ffn_conv_b, w_down):
    nb, nt, _ = x_sample.shape
    bp = x_prompt.shape[0]

    row = lambda v: v[:, None, :]
    vecs = jnp.concatenate(
        [row(norm_mix_pre), row(norm_mix_post), row(conv_lru_b), row(lru_br), row(lru_bi), row(lru_lambda),
         conv_lru_w, row(norm_ffn_pre), row(norm_ffn_post),
         jnp.zeros((DEPTH, VEC_ROWS - 12, D_MODEL), F32)], axis=1)
    fcv = jnp.concatenate(
        [ffn_conv_w, row(ffn_conv_b), jnp.zeros((DEPTH, SUBLANES - CONV_FF - 1, 2 * D_FF), F32)], axis=1)
    w_in_b = w_in.astype(BF16)
    wri_b = jnp.concatenate([lru_wr, lru_wi], axis=-1).astype(BF16)
    w_lo_b = w_lru_o.astype(BF16)
    w_ao_b = w_attn_o.astype(BF16)
    w_out_b = w_out.astype(BF16)
    w_up_b = w_up.astype(BF16)
    w_down_b = w_down.astype(BF16)

    ptab, stab = _bias_tables(rel_bias, attn_sink)

    yp, xs = x_prompt, x_sample
    p_mix = p_ffn = s_mix = s_att = s_ffn = None
    for l in range(DEPTH):
        yp, *p_mix = _prompt_mixer(yp, vecs, w_in_b, wri_b, w_lo_b, w_ao_b, w_out_b, ptab, attn_sink, l, p_mix)
        yp, *p_ffn = _prompt_ffn(yp, vecs, fcv, w_up_b, w_down_b, l, p_ffn)

        q, kn, vn, lru, gates, *s_mix = _sample_mixer_in(
            xs, state_lru_h, state_lru_conv, vecs, w_in_b, wri_b, l, s_mix)
        attn, *s_att = _sample_attention(
            q.reshape(nb, nt * N_HEADS, HEAD_DIM), kn, vn, cache_win_k, cache_win_v, stab, l, s_att)
        xmid = _sample_mixer_out(xs, lru, attn.reshape(nb, nt * Q_COLS), gates, vecs, w_lo_b, w_ao_b, w_out_b, l)
        xs, *s_ffn = _sample_ffn(xmid, state_ffn_conv, vecs, fcv, w_up_b, w_down_b, l, nb, s_ffn)

    p_h, p_c, p_k, p_v = p_mix
    s_h, s_c = s_mix
    s_k, s_v = s_att
    return (yp, xs, p_h, p_c, p_k, p_v, p_ffn[0], s_h, s_c, s_k, s_v, s_ffn[0])
```

```python
import functools
import math

import numpy as np
import jax
import jax.numpy as jnp
from jax import lax
from jax.experimental import pallas as pl
from jax.experimental.pallas import tpu as pltpu

D_MODEL = 1024
DEPTH = 2
PAST_LEN = 16384
D_RNN = D_MODEL
N_LRU_BLOCKS = 8
LRU_BLOCK = D_RNN // N_LRU_BLOCKS
CONV_LRU = 4
LRU_C = 8.0
N_HEADS = 8
N_KV = 2
GROUP = N_HEADS // N_KV
HEAD_DIM = D_MODEL // N_HEADS
WINDOW = 128
N_BUCKETS = 32
MAX_EXACT = N_BUCKETS // 2
MAX_DISTANCE = 128
D_FF = 4 * D_MODEL
CONV_FF = 3
EPS = 1e-6
Q_COLS = N_HEADS * HEAD_DIM
KV_COLS = N_KV * HEAD_DIM
IN_COLS = D_RNN + Q_COLS + 2 * KV_COLS + 2 * D_MODEL
C_Q = D_RNN
C_K = C_Q + Q_COLS
C_V = C_K + KV_COLS
C_G = C_V + KV_COLS

F32 = jnp.float32
BF16 = jnp.bfloat16

SUBLANES = 8
LANES = 128
VMEM_LIMIT_BYTES = 56 * 1024 * 1024

V_NORM_MIX_PRE, V_NORM_MIX_POST, V_CONV_B, V_BR, V_BI, V_LAMBDA, V_CONV_W = 0, 1, 2, 3, 4, 5, 6
V_NORM_FFN_PRE, V_NORM_FFN_POST = 10, 11
VEC_ROWS = 16

PROMPT_TILE = 256
FF_CHUNK = 512
SAMPLE_KEYS = 136
SINK_COL = WINDOW + 4
SAMPLE_BATCH_TILE = 16
SEG_LEN = WINDOW // SUBLANES


def _bucket_thresholds():
    d = np.arange(0, 2 * WINDOW)
    nf = np.maximum(d, 1).astype(np.float64)
    large = MAX_EXACT + (np.log(nf / MAX_EXACT) / math.log(MAX_DISTANCE / MAX_EXACT)
                         * (N_BUCKETS - MAX_EXACT)).astype(np.int64)
    bucket = np.where(d < MAX_EXACT, d, np.minimum(large, N_BUCKETS - 1))
    return tuple(int(d[bucket >= b].min()) for b in range(1, N_BUCKETS))


_BUCKET_THRESHOLDS = _bucket_thresholds()


def _dot(a, b):
    return jnp.dot(a, b, preferred_element_type=F32)


def _dot_nt(a, b):
    return lax.dot_general(a, b, (((1,), (1,)), ((), ())), preferred_element_type=F32)


def _rmsnorm(x, g):
    return x * lax.rsqrt(jnp.mean(x * x, axis=-1, keepdims=True) + EPS) * g


def _sigmoid(x):
    return 1.0 / (1.0 + jnp.exp(-x))


def _gelu_tanh(x):
    return 0.5 * x * (1.0 + jnp.tanh(math.sqrt(2.0 / math.pi) * (x + 0.044715 * (x * x * x))))


def _const_spec(block_shape, index):
    return pl.BlockSpec(block_shape, lambda *_: index, pipeline_mode=pl.Buffered(1))


def _with_state_aliases(kern, n_in, in_specs, args, prev_states, first_state_out):
    if prev_states is None:
        return kern, list(in_specs), list(args), {}
    n = len(prev_states)

    def body(*refs):
        return kern(*refs[:n_in], *refs[n_in + n:])

    return (body, list(in_specs) + [pl.BlockSpec(memory_space=pl.ANY)] * n, list(args) + list(prev_states),
            {n_in + i: first_state_out + i for i in range(n)})


def _bucket_of(d):
    n = jnp.maximum(d, 0)
    bucket = jnp.zeros(d.shape, jnp.int32)
    for thr in _BUCKET_THRESHOLDS:
        bucket = bucket + jnp.where(n >= thr, 1, 0)
    return bucket


def _block_time(p):
    return lax.bitwise_and(p, SUBLANES - 1) * SEG_LEN + lax.shift_right_logical(p, 3)


def _table_kernel(rel_ref, sink_ref, pt_ref, st_ref):
    qi = lax.broadcasted_iota(jnp.int32, (WINDOW, 2 * WINDOW), 0)
    kj = lax.broadcasted_iota(jnp.int32, (WINDOW, 2 * WINDOW), 1)
    kpos = lax.bitwise_and(kj, WINDOW - 1)
    d = _block_time(qi) + WINDOW - (_block_time(kpos) + (kj - kpos))
    bucket = _bucket_of(d)
    in_band = jnp.where(d >= 0, jnp.where(d < WINDOW, 1, 0), 0)
    cur_only = jnp.where(kj >= WINDOW, in_band, 0)
    for h in range(N_HEADS):
        val = jnp.zeros(d.shape, F32)
        for b in range(N_BUCKETS):
            val = jnp.where(bucket == b, rel_ref[b, h], val)
        pt_ref[1, h] = jnp.where(in_band == 1, val, -jnp.inf)
        pt_ref[0, h] = jnp.where(cur_only == 1, val, -jnp.inf)

    r = lax.broadcasted_iota(jnp.int32, (4 * N_HEADS, SAMPLE_KEYS), 0)
    j = lax.broadcasted_iota(jnp.int32, (4 * N_HEADS, SAMPLE_KEYS), 1)
    t = lax.shift_right_logical(r, 3)
    hh = lax.bitwise_and(r, N_HEADS - 1)
    d = t + WINDOW - j
    bucket = _bucket_of(d)
    in_band = jnp.where(d >= 0, jnp.where(d < WINDOW, 1, 0), 0)
    val = jnp.zeros(d.shape, F32)
    for h in range(N_HEADS):
        hval = jnp.zeros(d.shape, F32)
        for b in range(N_BUCKETS):
            hval = jnp.where(bucket == b, rel_ref[b, h], hval)
        val = jnp.where(hh == h, hval, val)
    val = jnp.where(in_band == 1, val, -jnp.inf)
    for l in range(DEPTH):
        sk = jnp.zeros(d.shape, F32)
        for h in range(N_HEADS):
            sk = jnp.where(hh == h, sink_ref[l, h], sk)
        st_ref[l] = jnp.where(j == SINK_COL, sk, val)


def _bias_tables(rel_bias, attn_sink):
    smem = pl.BlockSpec(memory_space=pltpu.SMEM)
    return pl.pallas_call(
        _table_kernel,
        out_shape=(jax.ShapeDtypeStruct((2, N_HEADS, WINDOW, 2 * WINDOW), F32),
                   jax.ShapeDtypeStruct((DEPTH, 4 * N_HEADS, SAMPLE_KEYS), F32)),
        in_specs=[smem, smem],
        name="bias_tables",
    )(rel_bias, attn_sink)


def _lru_coeffs(xc, vec_ref, wri_ref, n):
    cb = slice(n * LRU_BLOCK, (n + 1) * LRU_BLOCK)
    rw = _dot(xc.astype(BF16), wri_ref[n])
    r = _sigmoid(rw[:, :LRU_BLOCK] + vec_ref[V_BR:V_BR + 1, cb])
    i = _sigmoid(rw[:, LRU_BLOCK:] + vec_ref[V_BI:V_BI + 1, cb])
    z = -vec_ref[V_LAMBDA:V_LAMBDA + 1, cb]
    softplus = jnp.maximum(z, 0.0) + jnp.log1p(jnp.exp(-jnp.abs(z)))
    log_a = (-LRU_C * softplus) * r
    a = jnp.exp(log_a)
    mult = jnp.sqrt(jnp.maximum(1.0 - a * a, 0.0))
    return a, i * xc, mult


def _delayed_groups(groups, prev_tail, ndelay, sub):
    ng = len(groups)
    wrapped = {}
    for i in range(ndelay):
        k = ng - ndelay + i
        wrapped[k] = jnp.where(sub == 0, pltpu.roll(prev_tail[i], 1, 0), pltpu.roll(groups[k], 1, 0))
    return [[groups[k - d] if k >= d else wrapped[ng + k - d] for k in range(ng)] for d in range(1, ndelay + 1)]


def _row_groups(x, block):
    return [x[block * WINDOW + k * SUBLANES:block * WINDOW + (k + 1) * SUBLANES] for k in range(SEG_LEN)]


def _pmix_kernel(x_ref, vec_ref, w_in_ref, wri_ref, w_lo_ref, w_ao_ref, w_out_ref, tab_ref, sink_ref,
                 y_ref, hlast_ref, cstate_ref, kstate_ref, vstate_ref,
                 xp_s, h_s, xr_s, xrc_s, q_s, k_s, v_s, lru_s, attn_s, hc_s, hl_s, *, layer, tc, permute_in):
    t = pl.program_id(1)
    nblk = tc // WINDOW
    ndelay = CONV_LRU - 1
    batch_row = lax.broadcasted_iota(jnp.int32, (hl_s.shape[0], LRU_BLOCK), 0)

    @pl.when(jnp.logical_and(t == 0, pl.program_id(0) == 0))
    def _():
        hl_s[...] = jnp.zeros(hl_s.shape, F32)

    @pl.when(t == 0)
    def _():
        xrc_s[...] = jnp.zeros(xrc_s.shape, F32)
        k_s[0:WINDOW, :] = jnp.zeros((WINDOW, KV_COLS), BF16)
        v_s[0:WINDOW, :] = jnp.zeros((WINDOW, KV_COLS), BF16)
        hc_s[...] = jnp.zeros(hc_s.shape, F32)

    if permute_in:
        for j in range(nblk):
            for k in range(SEG_LEN):
                xp_s[j * WINDOW + k * SUBLANES:j * WINDOW + (k + 1) * SUBLANES, :] = (
                    x_ref[j * SUBLANES:(j + 1) * SUBLANES, k, :])
        x_tile = xp_s
    else:
        x_tile = x_ref

    h_s[...] = _rmsnorm(x_tile[...], vec_ref[V_NORM_MIX_PRE:V_NORM_MIX_PRE + 1, :]).astype(BF16)
    hb = h_s[...]
    xr_s[...] = _dot(hb, w_in_ref[:, 0:C_Q])
    q_s[...] = (_dot(hb, w_in_ref[:, C_Q:C_K]) * (HEAD_DIM ** -0.5)).astype(BF16)
    kf = _dot(hb, w_in_ref[:, C_K:C_V])
    vf = _dot(hb, w_in_ref[:, C_V:C_G])
    k_s[WINDOW:WINDOW + tc, :] = kf.astype(BF16)
    v_s[WINDOW:WINDOW + tc, :] = vf.astype(BF16)

    @pl.when(t == pl.num_programs(1) - 1)
    def _():
        for k in range(SEG_LEN):
            rows = slice(tc - WINDOW + k * SUBLANES, tc - WINDOW + (k + 1) * SUBLANES)
            for g in range(N_KV):
                kstate_ref[:, k, g, :] = kf[rows, g * HEAD_DIM:(g + 1) * HEAD_DIM]
                vstate_ref[:, k, g, :] = vf[rows, g * HEAD_DIM:(g + 1) * HEAD_DIM]

    for i in range(ndelay):
        r = tc - (ndelay - i) * SUBLANES + SUBLANES - 1
        cstate_ref[i:i + 1, :] = xr_s[r:r + 1, :]

    sub = lax.broadcasted_iota(jnp.int32, (SUBLANES, LRU_BLOCK), 0)
    seq_start = (sub + t) == 0
    for n in range(N_LRU_BLOCKS):
        cb = slice(n * LRU_BLOCK, (n + 1) * LRU_BLOCK)
        bias = jnp.broadcast_to(vec_ref[V_CONV_B:V_CONV_B + 1, cb], (SUBLANES, LRU_BLOCK))
        taps = [jnp.broadcast_to(vec_ref[V_CONV_W + j:V_CONV_W + j + 1, cb], (SUBLANES, LRU_BLOCK))
                for j in range(CONV_LRU)]
        xcs = []
        prev_tail = [xrc_s[i * SUBLANES:(i + 1) * SUBLANES, cb] for i in range(ndelay)]
        for j in range(nblk):
            groups = [xr_s[j * WINDOW + k * SUBLANES:j * WINDOW + (k + 1) * SUBLANES, cb] for k in range(SEG_LEN)]
            delayed = _delayed_groups(groups, prev_tail, ndelay, sub)
            for k in range(SEG_LEN):
                xc = bias + taps[CONV_LRU - 1] * groups[k]
                for d in range(1, CONV_LRU):
                    xc = xc + taps[CONV_LRU - 1 - d] * delayed[d - 1][k]
                xcs.append(xc)
            prev_tail = groups[SEG_LEN - ndelay:]
        a, ix, mult = _lru_coeffs(jnp.concatenate(xcs, axis=0), vec_ref, wri_ref, n)
        b = mult * ix

        carry = hc_s[0:1, cb]
        hs = []
        for j in range(nblk):
            ag, bg, ig = _row_groups(a, j), _row_groups(b, j), _row_groups(ix, j)
            if j == 0:
                bg[0] = jnp.where(seq_start, ig[0], bg[0])
            acc_a, acc_b = [ag[0]], [bg[0]]
            for k in range(1, SEG_LEN):
                acc_b.append(ag[k] * acc_b[-1] + bg[k])
                acc_a.append(ag[k] * acc_a[-1])
            seg_a, seg_b = acc_a[-1], acc_b[-1]
            for s in (1, 2, 4):
                ash = jnp.where(sub >= s, pltpu.roll(seg_a, s, 0), 1.0)
                bsh = jnp.where(sub >= s, pltpu.roll(seg_b, s, 0), 0.0)
                seg_b = seg_a * bsh + seg_b
                seg_a = seg_a * ash
            h_end = seg_a * carry + seg_b
            h_in = jnp.where(sub == 0, carry, pltpu.roll(h_end, 1, 0))
            hs += [acc_a[k] * h_in + acc_b[k] for k in range(SEG_LEN)]
            carry = h_end[SUBLANES - 1:SUBLANES, :]
        lru_s[:, cb] = jnp.concatenate(hs, axis=0).astype(BF16)
        hc_s[0:1, cb] = carry
        hl_s[:, cb] = jnp.where(batch_row == pl.program_id(0), carry, hl_s[:, cb])
    hlast_ref[...] = hl_s[...]

    first = jnp.where(t == 0, 0, 1)
    for j in range(tc // WINDOW):
        qrows = slice(j * WINDOW, (j + 1) * WINDOW)
        krows = slice(j * WINDOW, (j + 2) * WINDOW)
        variant = first if j == 0 else 1
        for g in range(N_KV):
            kvc = slice(g * HEAD_DIM, (g + 1) * HEAD_DIM)
            q4 = jnp.concatenate(
                [q_s[qrows, (g * GROUP + hg) * HEAD_DIM:(g * GROUP + hg + 1) * HEAD_DIM] for hg in range(GROUP)],
                axis=0)
            s = _dot_nt(q4, k_s[krows, kvc])
            ps, invs = [], []
            for hg in range(GROUP):
                head = g * GROUP + hg
                sh = s[hg * WINDOW:(hg + 1) * WINDOW] + tab_ref[variant, head]
                sk = sink_ref[layer, head]
                m = jnp.maximum(jnp.max(sh, axis=-1, keepdims=True), sk)
                p = jnp.exp(sh - m)
                den = jnp.sum(p, axis=-1, keepdims=True) + jnp.exp(sk - m)
                ps.append(p.astype(BF16))
                invs.append(1.0 / den)
            o4 = _dot(jnp.concatenate(ps, axis=0), v_s[krows, kvc])
            for hg in range(GROUP):
                head = g * GROUP + hg
                attn_s[qrows, head * HEAD_DIM:(head + 1) * HEAD_DIM] = (
                    o4[hg * WINDOW:(hg + 1) * WINDOW] * invs[hg]).astype(BF16)

    xrc_s[...] = xr_s[tc - ndelay * SUBLANES:tc, :]
    k_s[0:WINDOW, :] = k_s[tc:tc + WINDOW, :]
    v_s[0:WINDOW, :] = v_s[tc:tc + WINDOW, :]

    g_lru = _sigmoid(_dot(hb, w_in_ref[:, C_G:C_G + D_MODEL]))
    g_att = _sigmoid(_dot(hb, w_in_ref[:, C_G + D_MODEL:IN_COLS]))
    merged = g_lru * _dot(lru_s[...], w_lo_ref[...]) + g_att * _dot(attn_s[...], w_ao_ref[...])
    m = _dot(merged.astype(BF16), w_out_ref[...])
    y_ref[...] = x_tile[...] + _rmsnorm(m, vec_ref[V_NORM_MIX_POST:V_NORM_MIX_POST + 1, :])


def _prompt_mixer(x, vecs, w_in, wri, w_lo, w_ao, w_out, tab, sink, layer, prev_states, permute_in):
    bsz, seq, _ = x.shape
    tc = PROMPT_TILE
    kern = functools.partial(_pmix_kernel, layer=layer, tc=tc, permute_in=permute_in)
    tile = pl.BlockSpec((None, tc, D_MODEL), lambda b, t: (b, t, 0))
    kv_state = pl.BlockSpec((None, None, SUBLANES, SEG_LEN, N_KV, HEAD_DIM), lambda b, t: (layer, b, 0, 0, 0, 0))
    if permute_in:
        x = x.reshape(bsz, seq // SEG_LEN, SEG_LEN, D_MODEL)
        x_spec = pl.BlockSpec((None, tc // SEG_LEN, SEG_LEN, D_MODEL), lambda b, t: (b, t, 0, 0))
    else:
        x_spec = tile
    in_specs = [
        x_spec,
        _const_spec((None, VEC_ROWS, D_MODEL), (layer, 0, 0)),
        _const_spec((None, D_MODEL, IN_COLS), (layer, 0, 0)),
        _const_spec((None, N_LRU_BLOCKS, LRU_BLOCK, 2 * LRU_BLOCK), (layer, 0, 0, 0)),
        _const_spec((None, D_RNN, D_MODEL), (layer, 0, 0)),
        _const_spec((None, Q_COLS, D_MODEL), (layer, 0, 0)),
        _const_spec((None, D_MODEL, D_MODEL), (layer, 0, 0)),
        _const_spec((2, N_HEADS, WINDOW, 2 * WINDOW), (0, 0, 0, 0)),
        pl.BlockSpec(memory_space=pltpu.SMEM),
    ]
    args = [x, vecs, w_in, wri, w_lo, w_ao, w_out, tab, sink]
    kern, in_specs, args, aliases = _with_state_aliases(kern, len(args), in_specs, args, prev_states, 1)
    return pl.pallas_call(
        kern,
        grid=(bsz, seq // tc),
        in_specs=in_specs,
        out_specs=(tile,
                   pl.BlockSpec((None, bsz, D_RNN), lambda b, t: (layer, 0, 0)),
                   pl.BlockSpec((None, None, CONV_LRU - 1, D_RNN), lambda b, t: (layer, b, 0, 0)),
                   kv_state, kv_state),
        out_shape=(jax.ShapeDtypeStruct((bsz, seq, D_MODEL), F32),
                   jax.ShapeDtypeStruct((DEPTH, bsz, D_RNN), F32),
                   jax.ShapeDtypeStruct((DEPTH, bsz, CONV_LRU - 1, D_RNN), F32),
                   jax.ShapeDtypeStruct((DEPTH, bsz, SUBLANES, SEG_LEN, N_KV, HEAD_DIM), F32),
                   jax.ShapeDtypeStruct((DEPTH, bsz, SUBLANES, SEG_LEN, N_KV, HEAD_DIM), F32)),
        input_output_aliases=aliases,
        scratch_shapes=[
            pltpu.VMEM((tc if permute_in else SUBLANES, D_MODEL), F32),
            pltpu.VMEM((tc, D_MODEL), BF16),
            pltpu.VMEM((tc, D_RNN), F32),
            pltpu.VMEM(((CONV_LRU - 1) * SUBLANES, D_RNN), F32),
            pltpu.VMEM((tc, Q_COLS), BF16),
            pltpu.VMEM((WINDOW + tc, KV_COLS), BF16),
            pltpu.VMEM((WINDOW + tc, KV_COLS), BF16),
            pltpu.VMEM((tc, D_RNN), BF16),
            pltpu.VMEM((tc, Q_COLS), BF16),
            pltpu.VMEM((SUBLANES, D_RNN), F32),
            pltpu.VMEM((bsz, D_RNN), F32),
        ],
        compiler_params=pltpu.CompilerParams(
            dimension_semantics=("arbitrary", "arbitrary"), vmem_limit_bytes=VMEM_LIMIT_BYTES),
        name=f"prompt_mixer_l{layer}",
    )(*args)


def _pffn_kernel(x_ref, vec_ref, fcv_ref, w_up_ref, w_down_ref, y_ref, fstate_ref,
                 h_s, tail_s, act_s, *, tc, permute_out):
    t = pl.program_id(1)
    nblk = tc // WINDOW
    ndelay = CONV_FF - 1

    @pl.when(t == 0)
    def _():
        tail_s[...] = jnp.zeros(tail_s.shape, F32)

    h_s[...] = _rmsnorm(x_ref[...], vec_ref[V_NORM_FFN_PRE:V_NORM_FFN_PRE + 1, :]).astype(BF16)
    hb = h_s[...]
    sub = lax.broadcasted_iota(jnp.int32, (SUBLANES, FF_CHUNK), 0)
    for c in range(D_FF // FF_CHUNK):
        conv = []
        for part in range(2):
            cols = slice(part * D_FF + c * FF_CHUNK, part * D_FF + (c + 1) * FF_CHUNK)
            u = _dot(hb, w_up_ref[:, cols])
            bias = jnp.broadcast_to(fcv_ref[CONV_FF:CONV_FF + 1, cols], (SUBLANES, FF_CHUNK))
            taps = [jnp.broadcast_to(fcv_ref[j:j + 1, cols], (SUBLANES, FF_CHUNK)) for j in range(CONV_FF)]
            ys = []
            prev_tail = [tail_s[i * SUBLANES:(i + 1) * SUBLANES, cols] for i in range(ndelay)]
            for j in range(nblk):
                groups = _row_groups(u, j)
                delayed = _delayed_groups(groups, prev_tail, ndelay, sub)
                for k in range(SEG_LEN):
                    y = bias + taps[CONV_FF - 1] * groups[k]
                    for d in range(1, CONV_FF):
                        y = y + taps[CONV_FF - 1 - d] * delayed[d - 1][k]
                    ys.append(y)
                prev_tail = groups[SEG_LEN - ndelay:]
            conv.append(jnp.concatenate(ys, axis=0))
            tail_s[:, cols] = u[tc - ndelay * SUBLANES:tc]
            for i in range(ndelay):
                r = tc - (ndelay - i) * SUBLANES + SUBLANES - 1
                fstate_ref[i:i + 1, cols] = u[r:r + 1]
        act_s[:, c * FF_CHUNK:(c + 1) * FF_CHUNK] = (_gelu_tanh(conv[1]) * conv[0]).astype(BF16)
    f = _dot(act_s[...], w_down_ref[...])
    y = x_ref[...] + _rmsnorm(f, vec_ref[V_NORM_FFN_POST:V_NORM_FFN_POST + 1, :])
    if permute_out:
        for j in range(nblk):
            for k in range(SEG_LEN):
                y_ref[j * SUBLANES:(j + 1) * SUBLANES, k, :] = (
                    y[j * WINDOW + k * SUBLANES:j * WINDOW + (k + 1) * SUBLANES])
    else:
        y_ref[...] = y


def _prompt_ffn(x, vecs, fcv, w_up, w_down, layer, prev_states, permute_out):
    bsz, seq, _ = x.shape
    tc = PROMPT_TILE
    kern = functools.partial(_pffn_kernel, tc=tc, permute_out=permute_out)
    tile = pl.BlockSpec((None, tc, D_MODEL), lambda b, t: (b, t, 0))
    if permute_out:
        y_shape = (bsz, seq // SEG_LEN, SEG_LEN, D_MODEL)
        y_spec = pl.BlockSpec((None, tc // SEG_LEN, SEG_LEN, D_MODEL), lambda b, t: (b, t, 0, 0))
    else:
        y_shape, y_spec = (bsz, seq, D_MODEL), tile
    in_specs = [
        tile,
        _const_spec((None, VEC_ROWS, D_MODEL), (layer, 0, 0)),
        _const_spec((None, SUBLANES, 2 * D_FF), (layer, 0, 0)),
        _const_spec((None, D_MODEL, 2 * D_FF), (layer, 0, 0)),
        _const_spec((None, D_FF, D_MODEL), (layer, 0, 0)),
    ]
    args = [x, vecs, fcv, w_up, w_down]
    kern, in_specs, args, aliases = _with_state_aliases(kern, len(args), in_specs, args, prev_states, 1)
    return pl.pallas_call(
        kern,
        grid=(bsz, seq // tc),
        in_specs=in_specs,
        out_specs=(y_spec, pl.BlockSpec((None, None, CONV_FF - 1, 2 * D_FF), lambda b, t: (layer, b, 0, 0))),
        out_shape=(jax.ShapeDtypeStruct(y_shape, F32),
                   jax.ShapeDtypeStruct((DEPTH, bsz, CONV_FF - 1, 2 * D_FF), F32)),
        input_output_aliases=aliases,
        scratch_shapes=[
            pltpu.VMEM((tc, D_MODEL), BF16),
            pltpu.VMEM(((CONV_FF - 1) * SUBLANES, 2 * D_FF), F32),
            pltpu.VMEM((tc, D_FF), BF16),
        ],
        compiler_params=pltpu.CompilerParams(
            dimension_semantics=("arbitrary", "arbitrary"), vmem_limit_bytes=VMEM_LIMIT_BYTES),
        name=f"prompt_ffn_l{layer}",
    )(*args)


def _smix_in_kernel(xs_ref, h0_ref, cprev_ref, vec_ref, w_in_ref, wri_ref,
                    q_ref, k_ref, v_ref, lru_ref, gate_ref, hlast_ref, cstate_ref,
                    h_s, xr_s, *, nb, nt):
    for t in range(nt):
        h_s[t * nb:(t + 1) * nb, :] = _rmsnorm(
            xs_ref[:, t, :], vec_ref[V_NORM_MIX_PRE:V_NORM_MIX_PRE + 1, :]).astype(BF16)
    hb = h_s[...]
    xr_s[...] = _dot(hb, w_in_ref[:, 0:C_Q])
    qf = _dot(hb, w_in_ref[:, C_Q:C_K]) * (HEAD_DIM ** -0.5)
    kf = _dot(hb, w_in_ref[:, C_K:C_V])
    vf = _dot(hb, w_in_ref[:, C_V:C_G])
    for t in range(nt):
        rows = slice(t * nb, (t + 1) * nb)
        q_ref[:, t * Q_COLS:(t + 1) * Q_COLS] = qf[rows].astype(BF16)
        k_ref[:, t, :] = kf[rows]
        v_ref[:, t, :] = vf[rows]
    gate_ref[...] = _sigmoid(_dot(hb, w_in_ref[:, C_G:IN_COLS]))

    npre = CONV_LRU - 1
    for t in range(nt - npre, nt):
        cstate_ref[:, t - (nt - npre), :] = xr_s[t * nb:(t + 1) * nb, :]

    for n in range(N_LRU_BLOCKS):
        cb = slice(n * LRU_BLOCK, (n + 1) * LRU_BLOCK)
        xx = [cprev_ref[:, j, cb] for j in range(npre)]
        xx += [xr_s[t * nb:(t + 1) * nb, cb] for t in range(nt)]
        xcs = []
        for t in range(nt):
            xc = vec_ref[V_CONV_B:V_CONV_B + 1, cb]
            for j in range(CONV_LRU):
                xc = xc + vec_ref[V_CONV_W + j:V_CONV_W + j + 1, cb] * xx[t + j]
            xcs.append(xc)
        a, ix, mult = _lru_coeffs(jnp.concatenate(xcs, axis=0), vec_ref, wri_ref, n)
        b = mult * ix
        h = h0_ref[:, cb]
        for t in range(nt):
            rows = slice(t * nb, (t + 1) * nb)
            h = a[rows] * h + b[rows]
            lru_ref[rows, cb] = h.astype(BF16)
        hlast_ref[:, cb] = h


def _sample_mixer_in(xs, h0, cprev, vecs, w_in, wri, layer, prev_states):
    nb, nt, _ = xs.shape
    assert PAST_LEN > 0
    kern = functools.partial(_smix_in_kernel, nb=nb, nt=nt)
    whole = lambda shape: pl.BlockSpec(shape, lambda i: (0,) * len(shape))
    conv_state = pl.BlockSpec((None, nb, CONV_LRU - 1, D_RNN), lambda i: (layer, 0, 0, 0))
    in_specs = [
        whole((nb, nt, D_MODEL)),
        pl.BlockSpec((None, nb, D_RNN), lambda i: (layer, 0, 0)),
        conv_state,
        _const_spec((None, VEC_ROWS, D_MODEL), (layer, 0, 0)),
        _const_spec((None, D_MODEL, IN_COLS), (layer, 0, 0)),
        _const_spec((None, N_LRU_BLOCKS, LRU_BLOCK, 2 * LRU_BLOCK), (layer, 0, 0, 0)),
    ]
    args = [xs, h0, cprev, vecs, w_in, wri]
    kern, in_specs, args, aliases = _with_state_aliases(kern, len(args), in_specs, args, prev_states, 5)
    return pl.pallas_call(
        kern,
        grid=(1,),
        in_specs=in_specs,
        out_specs=(whole((nb, nt * Q_COLS)), whole((nb, nt, KV_COLS)), whole((nb, nt, KV_COLS)),
                   whole((nt * nb, D_RNN)), whole((nt * nb, 2 * D_MODEL)),
                   pl.BlockSpec((None, nb, D_RNN), lambda i: (layer, 0, 0)), conv_state),
        out_shape=(jax.ShapeDtypeStruct((nb, nt * Q_COLS), BF16),
                   jax.ShapeDtypeStruct((nb, nt, KV_COLS), F32),
                   jax.ShapeDtypeStruct((nb, nt, KV_COLS), F32),
                   jax.ShapeDtypeStruct((nt * nb, D_RNN), BF16),
                   jax.ShapeDtypeStruct((nt * nb, 2 * D_MODEL), F32),
                   jax.ShapeDtypeStruct((DEPTH, nb, D_RNN), F32),
                   jax.ShapeDtypeStruct((DEPTH, nb, CONV_LRU - 1, D_RNN), F32)),
        input_output_aliases=aliases,
        scratch_shapes=[pltpu.VMEM((nt * nb, D_MODEL), BF16), pltpu.VMEM((nt * nb, D_RNN), F32)],
        compiler_params=pltpu.CompilerParams(
            dimension_semantics=("arbitrary",), vmem_limit_bytes=VMEM_LIMIT_BYTES),
        name=f"sample_mixer_in_l{layer}",
    )(*args)


def _sattn_kernel(q_ref, kn_ref, vn_ref, ck_ref, cv_ref, tab_ref, attn_ref, sk_ref, sv_ref,
                  kc_s, vc_s, *, bt, nt):
    pad = SAMPLE_KEYS - WINDOW
    for g in range(N_KV):
        kc_s[g, WINDOW:SAMPLE_KEYS, :] = jnp.zeros((pad, HEAD_DIM), F32)
        vc_s[g, WINDOW:SAMPLE_KEYS, :] = jnp.zeros((pad, HEAD_DIM), F32)
    row = lax.broadcasted_iota(jnp.int32, (nt * N_HEADS, HEAD_DIM), 0)
    in_group0 = lax.bitwise_and(row, N_HEADS - 1) < GROUP

    def body(b, _):
        qb = q_ref[b]
        outs = []
        for g in range(N_KV):
            kvc = slice(g * HEAD_DIM, (g + 1) * HEAD_DIM)
            kc_s[g, 0:WINDOW, :] = ck_ref[b, :, g, :]
            vc_s[g, 0:WINDOW, :] = cv_ref[b, :, g, :]
            kc_s[g, WINDOW:WINDOW + nt, :] = kn_ref[b, :, kvc]
            vc_s[g, WINDOW:WINDOW + nt, :] = vn_ref[b, :, kvc]
            sk_ref[b, :, g, :] = kc_s[g, nt:nt + WINDOW, :]
            sv_ref[b, :, g, :] = vc_s[g, nt:nt + WINDOW, :]
            s = _dot_nt(qb, kc_s[g].astype(BF16)) + tab_ref[...]
            m = jnp.max(s, axis=-1, keepdims=True)
            p = jnp.exp(s - m)
            den = jnp.sum(p, axis=-1, keepdims=True)
            outs.append(_dot(p.astype(BF16), vc_s[g].astype(BF16)) * (1.0 / den))
        attn_ref[b] = jnp.where(in_group0, outs[0], outs[1]).astype(BF16)
        return 0

    lax.fori_loop(0, bt, body, 0)


def _sample_attention(q, kn, vn, cache_k, cache_v, stab, layer, prev_states):
    nb, rows, _ = q.shape
    nt = rows // N_HEADS
    bt = SAMPLE_BATCH_TILE
    kern = functools.partial(_sattn_kernel, bt=bt, nt=nt)
    cache_spec = pl.BlockSpec((None, bt, WINDOW, N_KV, HEAD_DIM), lambda i: (layer, i, 0, 0, 0))
    new_spec = pl.BlockSpec((bt, nt, KV_COLS), lambda i: (i, 0, 0))
    q_spec = pl.BlockSpec((bt, rows, HEAD_DIM), lambda i: (i, 0, 0))
    in_specs = [q_spec, new_spec, new_spec, cache_spec, cache_spec,
                _const_spec((None, rows, SAMPLE_KEYS), (layer, 0, 0))]
    args = [q, kn, vn, cache_k, cache_v, stab]
    kern, in_specs, args, aliases = _with_state_aliases(kern, len(args), in_specs, args, prev_states, 1)
    return pl.pallas_call(
        kern,
        grid=(nb // bt,),
        in_specs=in_specs,
        out_specs=(q_spec, cache_spec, cache_spec),
        out_shape=(jax.ShapeDtypeStruct((nb, rows, HEAD_DIM), BF16),
                   jax.ShapeDtypeStruct((DEPTH, nb, WINDOW, N_KV, HEAD_DIM), F32),
                   jax.ShapeDtypeStruct((DEPTH, nb, WINDOW, N_KV, HEAD_DIM), F32)),
        input_output_aliases=aliases,
        scratch_shapes=[pltpu.VMEM((N_KV, SAMPLE_KEYS, HEAD_DIM), F32),
                        pltpu.VMEM((N_KV, SAMPLE_KEYS, HEAD_DIM), F32)],
        compiler_params=pltpu.CompilerParams(
            dimension_semantics=("arbitrary",), vmem_limit_bytes=VMEM_LIMIT_BYTES),
        name=f"sample_attention_l{layer}",
    )(*args)


def _smix_out_kernel(xs_ref, lru_ref, attn_ref, gate_ref, vec_ref, w_lo_ref, w_ao_ref, w_out_ref,
                     xmid_ref, a_s, *, nb, nt):
    for t in range(nt):
        a_s[t * nb:(t + 1) * nb, :] = attn_ref[:, t * Q_COLS:(t + 1) * Q_COLS]
    merged = (gate_ref[:, 0:D_MODEL] * _dot(lru_ref[...], w_lo_ref[...])
              + gate_ref[:, D_MODEL:2 * D_MODEL] * _dot(a_s[...], w_ao_ref[...]))
    m = _dot(merged.astype(BF16), w_out_ref[...])
    mn = _rmsnorm(m, vec_ref[V_NORM_MIX_POST:V_NORM_MIX_POST + 1, :])
    for t in range(nt):
        rows = slice(t * nb, (t + 1) * nb)
        xmid_ref[rows, :] = xs_ref[:, t, :] + mn[rows]


def _sample_mixer_out(xs, lru, attn, gates, vecs, w_lo, w_ao, w_out, layer):
    nb, nt, _ = xs.shape
    kern = functools.partial(_smix_out_kernel, nb=nb, nt=nt)
    whole = lambda shape: pl.BlockSpec(shape, lambda i: (0,) * len(shape))
    return pl.pallas_call(
        kern,
        grid=(1,),
        in_specs=[
            whole(xs.shape), whole(lru.shape), whole(attn.shape), whole(gates.shape),
            _const_spec((None, VEC_ROWS, D_MODEL), (layer, 0, 0)),
            _const_spec((None, D_RNN, D_MODEL), (layer, 0, 0)),
            _const_spec((None, Q_COLS, D_MODEL), (layer, 0, 0)),
            _const_spec((None, D_MODEL, D_MODEL), (layer, 0, 0)),
        ],
        out_specs=whole((nt * nb, D_MODEL)),
        out_shape=jax.ShapeDtypeStruct((nt * nb, D_MODEL), F32),
        scratch_shapes=[pltpu.VMEM((nt * nb, Q_COLS), BF16)],
        compiler_params=pltpu.CompilerParams(
            dimension_semantics=("arbitrary",), vmem_limit_bytes=VMEM_LIMIT_BYTES),
        name=f"sample_mixer_out_l{layer}",
    )(xs, lru, attn, gates, vecs, w_lo, w_ao, w_out)


def _sffn_kernel(x_ref, vec_ref, fcv_ref, prev_ref, w_ref, wd_ref, ys_ref, fstate_ref,
                 h_s, val_s, acc_s, *, nb, nt, nck):
    c = pl.program_id(0)

    @pl.when(c == 0)
    def _():
        h_s[...] = _rmsnorm(x_ref[...], vec_ref[V_NORM_FFN_PRE:V_NORM_FFN_PRE + 1, :]).astype(BF16)
        acc_s[...] = jnp.zeros(acc_s.shape, F32)

    u = _dot(h_s[...], w_ref[...])
    uu = [prev_ref[:, j, :] for j in range(CONV_FF - 1)] + [u[t * nb:(t + 1) * nb] for t in range(nt)]
    ys = []
    for t in range(nt):
        y = fcv_ref[CONV_FF:CONV_FF + 1, :]
        for j in range(CONV_FF):
            y = y + fcv_ref[j:j + 1, :] * uu[t + j]
        ys.append(y)
    conv = jnp.concatenate(ys, axis=0)
    for j in range(CONV_FF - 1):
        fstate_ref[:, j, :] = uu[nt + j]

    @pl.when(c < nck)
    def _():
        val_s[c] = conv

    @pl.when(c >= nck)
    def _():
        act = (_gelu_tanh(conv) * val_s[c - nck]).astype(BF16)
        acc_s[...] += _dot(act, wd_ref[...])

    @pl.when(c == 2 * nck - 1)
    def _():
        y = x_ref[...] + _rmsnorm(acc_s[...], vec_ref[V_NORM_FFN_POST:V_NORM_FFN_POST + 1, :])
        for t in range(nt):
            ys_ref[:, t, :] = y[t * nb:(t + 1) * nb]


def _sample_ffn(xmid, fprev, vecs, fcv, w_up, w_down, layer, nb, prev_states):
    nt = xmid.shape[0] // nb
    ck = FF_CHUNK
    nck = D_FF // ck
    kern = functools.partial(_sffn_kernel, nb=nb, nt=nt, nck=nck)
    state_spec = pl.BlockSpec((None, nb, CONV_FF - 1, ck), lambda c: (layer, 0, 0, c))
    in_specs = [
        _const_spec((nt * nb, D_MODEL), (0, 0)),
        _const_spec((None, VEC_ROWS, D_MODEL), (layer, 0, 0)),
        pl.BlockSpec((None, SUBLANES, ck), lambda c: (layer, 0, c)),
        state_spec,
        pl.BlockSpec((None, D_MODEL, ck), lambda c: (layer, 0, c)),
        pl.BlockSpec((None, ck, D_MODEL), lambda c: (layer, jnp.maximum(c - nck, 0), 0)),
    ]
    args = [xmid, vecs, fcv, fprev, w_up, w_down]
    kern, in_specs, args, aliases = _with_state_aliases(kern, len(args), in_specs, args, prev_states, 1)
    return pl.pallas_call(
        kern,
        grid=(2 * nck,),
        in_specs=in_specs,
        out_specs=(pl.BlockSpec((nb, nt, D_MODEL), lambda c: (0, 0, 0)), state_spec),
        out_shape=(jax.ShapeDtypeStruct((nb, nt, D_MODEL), F32),
                   jax.ShapeDtypeStruct((DEPTH, nb, CONV_FF - 1, 2 * D_FF), F32)),
        input_output_aliases=aliases,
        scratch_shapes=[pltpu.VMEM((nt * nb, D_MODEL), BF16),
                        pltpu.VMEM((nck, nt * nb, ck), F32),
                        pltpu.VMEM((nt * nb, D_MODEL), F32)],
        compiler_params=pltpu.CompilerParams(
            dimension_semantics=("arbitrary",), vmem_limit_bytes=VMEM_LIMIT_BYTES),
        name=f"sample_ffn_l{layer}",
    )(*args)


def kernel(x_prompt, x_sample, state_lru_h, state_lru_conv, cache_win_k, cache_win_v, state_ffn_conv,
           norm_mix_pre, norm_mix_post, norm_ffn_pre, norm_ffn_post, w_in, conv_lru_w, conv_lru_b,
           lru_wr, lru_br, lru_wi, lru_bi, lru_lambda, w_lru_o, w_attn_o, w_out, attn_sink, rel_bias,
           w_up, ffn_conv_w, ffn_conv_b, w_down):
    nb, nt, _ = x_sample.shape
    bp = x_prompt.shape[0]

    row = lambda v: v[:, None, :]
    vecs = jnp.concatenate(
        [row(norm_mix_pre), row(norm_mix_post), row(conv_lru_b), row(lru_br), row(lru_bi), row(lru_lambda),
         conv_lru_w, row(norm_ffn_pre), row(norm_ffn_post),
         jnp.zeros((DEPTH, VEC_ROWS - 12, D_MODEL), F32)], axis=1)
    fcv = jnp.concatenate(
        [ffn_conv_w, row(ffn_conv_b), jnp.zeros((DEPTH, SUBLANES - CONV_FF - 1, 2 * D_FF), F32)], axis=1)
    w_in_b = w_in.astype(BF16)
    wri_b = jnp.concatenate([lru_wr, lru_wi], axis=-1).astype(BF16)
    w_lo_b = w_lru_o.astype(BF16)
    w_ao_b = w_attn_o.astype(BF16)
    w_out_b = w_out.astype(BF16)
    w_up_b = w_up.astype(BF16)
    w_down_b = w_down.astype(BF16)

    ptab, stab = _bias_tables(rel_bias, attn_sink)

    yp, xs = x_prompt, x_sample
    p_mix = p_ffn = s_mix = s_att = s_ffn = None
    for l in range(DEPTH):
        yp, *p_mix = _prompt_mixer(yp, vecs, w_in_b, wri_b, w_lo_b, w_ao_b, w_out_b, ptab, attn_sink, l, p_mix,
                                   permute_in=(l == 0))
        yp, *p_ffn = _prompt_ffn(yp, vecs, fcv, w_up_b, w_down_b, l, p_ffn, permute_out=(l == DEPTH - 1))

        q, kn, vn, lru, gates, *s_mix = _sample_mixer_in(
            xs, state_lru_h, state_lru_conv, vecs, w_in_b, wri_b, l, s_mix)
        attn, *s_att = _sample_attention(
            q.reshape(nb, nt * N_HEADS, HEAD_DIM), kn, vn, cache_win_k, cache_win_v, stab, l, s_att)
        xmid = _sample_mixer_out(xs, lru, attn.reshape(nb, nt * Q_COLS), gates, vecs, w_lo_b, w_ao_b, w_out_b, l)
        xs, *s_ffn = _sample_ffn(xmid, state_ffn_conv, vecs, fcv, w_up_b, w_down_b, l, nb, s_ffn)

    p_h, p_c, p_k, p_v = p_mix
    s_h, s_c = s_mix
    s_k, s_v = s_att
    kv_shape = (DEPTH, bp, WINDOW, N_KV, HEAD_DIM)
    return (yp.reshape(x_prompt.shape), xs, p_h, p_c, p_k.reshape(kv_shape), p_v.reshape(kv_shape), p_ffn[0],
            s_h, s_c, s_k, s_v, s_ffn[0])
```

```python
import functools
import math

import numpy as np
import jax
import jax.numpy as jnp
from jax import lax
from jax.experimental import pallas as pl
from jax.experimental.pallas import tpu as pltpu

D_MODEL = 1024
DEPTH = 2
PAST_LEN = 16384
D_RNN = D_MODEL
N_LRU_BLOCKS = 8
LRU_BLOCK = D_RNN // N_LRU_BLOCKS
CONV_LRU = 4
LRU_C = 8.0
N_HEADS = 8
N_KV = 2
GROUP = N_HEADS // N_KV
HEAD_DIM = D_MODEL // N_HEADS
WINDOW = 128
N_BUCKETS = 32
MAX_EXACT = N_BUCKETS // 2
MAX_DISTANCE = 128
D_FF = 4 * D_MODEL
CONV_FF = 3
EPS = 1e-6
Q_COLS = N_HEADS * HEAD_DIM
KV_COLS = N_KV * HEAD_DIM
IN_COLS = D_RNN + Q_COLS + 2 * KV_COLS + 2 * D_MODEL
C_Q = D_RNN
C_K = C_Q + Q_COLS
C_V = C_K + KV_COLS
C_G = C_V + KV_COLS

F32 = jnp.float32
BF16 = jnp.bfloat16

SUBLANES = 8
LANES = 128
VMEM_LIMIT_BYTES = 56 * 1024 * 1024

V_NORM_MIX_PRE, V_NORM_MIX_POST, V_CONV_B, V_BR, V_BI, V_LAMBDA, V_CONV_W = 0, 1, 2, 3, 4, 5, 6
V_NORM_FFN_PRE, V_NORM_FFN_POST = 10, 11
VEC_ROWS = 16

PROMPT_TILE = 512
PROJ_PIECE = 2 * KV_COLS
assert (C_K - C_Q) % PROJ_PIECE == 0 and (IN_COLS - C_G) % PROJ_PIECE == 0
FF_CHUNK = 512
SAMPLE_KEYS = 136
SINK_COL = WINDOW + 4
SAMPLE_BATCH_TILE = 16
SEG_LEN = WINDOW // SUBLANES


def _bucket_thresholds():
    d = np.arange(0, 2 * WINDOW)
    nf = np.maximum(d, 1).astype(np.float64)
    large = MAX_EXACT + (np.log(nf / MAX_EXACT) / math.log(MAX_DISTANCE / MAX_EXACT)
                         * (N_BUCKETS - MAX_EXACT)).astype(np.int64)
    bucket = np.where(d < MAX_EXACT, d, np.minimum(large, N_BUCKETS - 1))
    return tuple(int(d[bucket >= b].min()) for b in range(1, N_BUCKETS))


_BUCKET_THRESHOLDS = _bucket_thresholds()


def _dot(a, b):
    return jnp.dot(a, b, preferred_element_type=F32)


def _dot_nt(a, b):
    return lax.dot_general(a, b, (((1,), (1,)), ((), ())), preferred_element_type=F32)


def _rmsnorm(x, g):
    return x * lax.rsqrt(jnp.mean(x * x, axis=-1, keepdims=True) + EPS) * g


def _sigmoid(x):
    return 1.0 / (1.0 + jnp.exp(-x))


def _gelu_tanh(x):
    return 0.5 * x * (1.0 + jnp.tanh(math.sqrt(2.0 / math.pi) * (x + 0.044715 * (x * x * x))))


def _const_spec(block_shape, index):
    return pl.BlockSpec(block_shape, lambda *_: index, pipeline_mode=pl.Buffered(1))


def _with_state_aliases(kern, n_in, in_specs, args, prev_states, first_state_out):
    if prev_states is None:
        return kern, list(in_specs), list(args), {}
    n = len(prev_states)

    def body(*refs):
        return kern(*refs[:n_in], *refs[n_in + n:])

    return (body, list(in_specs) + [pl.BlockSpec(memory_space=pl.ANY)] * n, list(args) + list(prev_states),
            {n_in + i: first_state_out + i for i in range(n)})


def _bucket_of(d):
    n = jnp.maximum(d, 0)
    bucket = jnp.zeros(d.shape, jnp.int32)
    for thr in _BUCKET_THRESHOLDS:
        bucket = bucket + jnp.where(n >= thr, 1, 0)
    return bucket


def _block_time(p):
    return lax.bitwise_and(p, SUBLANES - 1) * SEG_LEN + lax.shift_right_logical(p, 3)


def _table_kernel(rel_ref, sink_ref, pt_ref, st_ref):
    qi = lax.broadcasted_iota(jnp.int32, (WINDOW, 2 * WINDOW), 0)
    kj = lax.broadcasted_iota(jnp.int32, (WINDOW, 2 * WINDOW), 1)
    kpos = lax.bitwise_and(kj, WINDOW - 1)
    d = _block_time(qi) + WINDOW - (_block_time(kpos) + (kj - kpos))
    bucket = _bucket_of(d)
    in_band = jnp.where(d >= 0, jnp.where(d < WINDOW, 1, 0), 0)
    cur_only = jnp.where(kj >= WINDOW, in_band, 0)
    for h in range(N_HEADS):
        val = jnp.zeros(d.shape, F32)
        for b in range(N_BUCKETS):
            val = jnp.where(bucket == b, rel_ref[b, h], val)
        pt_ref[1, h] = jnp.where(in_band == 1, val, -jnp.inf)
        pt_ref[0, h] = jnp.where(cur_only == 1, val, -jnp.inf)

    r = lax.broadcasted_iota(jnp.int32, (4 * N_HEADS, SAMPLE_KEYS), 0)
    j = lax.broadcasted_iota(jnp.int32, (4 * N_HEADS, SAMPLE_KEYS), 1)
    t = lax.shift_right_logical(r, 3)
    hh = lax.bitwise_and(r, N_HEADS - 1)
    d = t + WINDOW - j
    bucket = _bucket_of(d)
    in_band = jnp.where(d >= 0, jnp.where(d < WINDOW, 1, 0), 0)
    val = jnp.zeros(d.shape, F32)
    for h in range(N_HEADS):
        hval = jnp.zeros(d.shape, F32)
        for b in range(N_BUCKETS):
            hval = jnp.where(bucket == b, rel_ref[b, h], hval)
        val = jnp.where(hh == h, hval, val)
    val = jnp.where(in_band == 1, val, -jnp.inf)
    for l in range(DEPTH):
        sk = jnp.zeros(d.shape, F32)
        for h in range(N_HEADS):
            sk = jnp.where(hh == h, sink_ref[l, h], sk)
        st_ref[l] = jnp.where(j == SINK_COL, sk, val)


def _bias_tables(rel_bias, attn_sink):
    smem = pl.BlockSpec(memory_space=pltpu.SMEM)
    return pl.pallas_call(
        _table_kernel,
        out_shape=(jax.ShapeDtypeStruct((2, N_HEADS, WINDOW, 2 * WINDOW), F32),
                   jax.ShapeDtypeStruct((DEPTH, 4 * N_HEADS, SAMPLE_KEYS), F32)),
        in_specs=[smem, smem],
        name="bias_tables",
    )(rel_bias, attn_sink)


def _lru_coeffs(xc, vec_ref, wri_ref, n):
    cb = slice(n * LRU_BLOCK, (n + 1) * LRU_BLOCK)
    rw = _dot(xc.astype(BF16), wri_ref[n])
    r = _sigmoid(rw[:, :LRU_BLOCK] + vec_ref[V_BR:V_BR + 1, cb])
    i = _sigmoid(rw[:, LRU_BLOCK:] + vec_ref[V_BI:V_BI + 1, cb])
    z = -vec_ref[V_LAMBDA:V_LAMBDA + 1, cb]
    softplus = jnp.maximum(z, 0.0) + jnp.log1p(jnp.exp(-jnp.abs(z)))
    log_a = (-LRU_C * softplus) * r
    a = jnp.exp(log_a)
    mult = jnp.sqrt(jnp.maximum(1.0 - a * a, 0.0))
    return a, i * xc, mult


def _delayed_groups(groups, prev_tail, ndelay, sub):
    ng = len(groups)
    wrapped = {}
    for i in range(ndelay):
        k = ng - ndelay + i
        wrapped[k] = jnp.where(sub == 0, pltpu.roll(prev_tail[i], 1, 0), pltpu.roll(groups[k], 1, 0))
    return [[groups[k - d] if k >= d else wrapped[ng + k - d] for k in range(ng)] for d in range(1, ndelay + 1)]


def _row_groups(x, block):
    return [x[block * WINDOW + k * SUBLANES:block * WINDOW + (k + 1) * SUBLANES] for k in range(SEG_LEN)]


def _pmix_kernel(x_ref, vec_ref, w_in_ref, wri_ref, w_lo_ref, w_ao_ref, w_out_ref, tab_ref, sink_ref,
                 y_ref, hlast_ref, cstate_ref, kstate_ref, vstate_ref,
                 xp_s, h_s, xr_s, xrc_s, q_s, k_s, v_s, kvf_s, g_s, lru_s, attn_s, hc_s, hl_s, *, layer, tc, permute_in):
    t = pl.program_id(1)
    nblk = tc // WINDOW
    ndelay = CONV_LRU - 1
    n_pieces = (IN_COLS - C_Q) // PROJ_PIECE
    batch_row = lax.broadcasted_iota(jnp.int32, (hl_s.shape[0], LRU_BLOCK), 0)

    @pl.when(jnp.logical_and(t == 0, pl.program_id(0) == 0))
    def _():
        hl_s[...] = jnp.zeros(hl_s.shape, F32)

    @pl.when(t == 0)
    def _():
        xrc_s[...] = jnp.zeros(xrc_s.shape, F32)
        k_s[0:WINDOW, :] = jnp.zeros((WINDOW, KV_COLS), BF16)
        v_s[0:WINDOW, :] = jnp.zeros((WINDOW, KV_COLS), BF16)
        hc_s[...] = jnp.zeros(hc_s.shape, F32)

    if permute_in:
        for j in range(nblk):
            for k in range(SEG_LEN):
                xp_s[j * WINDOW + k * SUBLANES:j * WINDOW + (k + 1) * SUBLANES, :] = (
                    x_ref[j * SUBLANES:(j + 1) * SUBLANES, k, :])
        x_tile = xp_s
    else:
        x_tile = x_ref

    h_s[...] = _rmsnorm(x_tile[...], vec_ref[V_NORM_MIX_PRE:V_NORM_MIX_PRE + 1, :]).astype(BF16)
    hb = h_s[...]
    xr_s[...] = _dot(hb, w_in_ref[:, 0:C_Q])

    def project_piece(i):
        c0 = C_Q + i * PROJ_PIECE
        z = _dot(hb, w_in_ref[:, c0:c0 + PROJ_PIECE])
        if c0 < C_K:
            q_s[:, c0 - C_Q:c0 - C_Q + PROJ_PIECE] = (z * (HEAD_DIM ** -0.5)).astype(BF16)
        elif c0 == C_K:
            k_s[WINDOW:WINDOW + tc, :] = z[:, 0:KV_COLS].astype(BF16)
            v_s[WINDOW:WINDOW + tc, :] = z[:, KV_COLS:2 * KV_COLS].astype(BF16)
            kvf_s[0] = z[tc - WINDOW:tc, 0:KV_COLS]
            kvf_s[1] = z[tc - WINDOW:tc, KV_COLS:2 * KV_COLS]
        else:
            g_s[:, c0 - C_G:c0 - C_G + PROJ_PIECE] = _sigmoid(z)

    for i in range(ndelay):
        r = tc - (ndelay - i) * SUBLANES + SUBLANES - 1
        cstate_ref[i:i + 1, :] = xr_s[r:r + 1, :]

    sub = lax.broadcasted_iota(jnp.int32, (SUBLANES, LRU_BLOCK), 0)
    seq_start = (sub + t) == 0
    for n in range(N_LRU_BLOCKS):
        cb = slice(n * LRU_BLOCK, (n + 1) * LRU_BLOCK)
        bias = jnp.broadcast_to(vec_ref[V_CONV_B:V_CONV_B + 1, cb], (SUBLANES, LRU_BLOCK))
        taps = [jnp.broadcast_to(vec_ref[V_CONV_W + j:V_CONV_W + j + 1, cb], (SUBLANES, LRU_BLOCK))
                for j in range(CONV_LRU)]
        xcs = []
        prev_tail = [xrc_s[i * SUBLANES:(i + 1) * SUBLANES, cb] for i in range(ndelay)]
        for j in range(nblk):
            groups = [xr_s[j * WINDOW + k * SUBLANES:j * WINDOW + (k + 1) * SUBLANES, cb] for k in range(SEG_LEN)]
            delayed = _delayed_groups(groups, prev_tail, ndelay, sub)
            for k in range(SEG_LEN):
                xc = bias + taps[CONV_LRU - 1] * groups[k]
                for d in range(1, CONV_LRU):
                    xc = xc + taps[CONV_LRU - 1 - d] * delayed[d - 1][k]
                xcs.append(xc)
            prev_tail = groups[SEG_LEN - ndelay:]
        a, ix, mult = _lru_coeffs(jnp.concatenate(xcs, axis=0), vec_ref, wri_ref, n)
        b = mult * ix

        carry = hc_s[0:1, cb]
        hs = []
        for j in range(nblk):
            ag, bg, ig = _row_groups(a, j), _row_groups(b, j), _row_groups(ix, j)
            if j == 0:
                bg[0] = jnp.where(seq_start, ig[0], bg[0])
            acc_a, acc_b = [ag[0]], [bg[0]]
            for k in range(1, SEG_LEN):
                acc_b.append(ag[k] * acc_b[-1] + bg[k])
                acc_a.append(ag[k] * acc_a[-1])
            seg_a, seg_b = acc_a[-1], acc_b[-1]
            for s in (1, 2, 4):
                ash = jnp.where(sub >= s, pltpu.roll(seg_a, s, 0), 1.0)
                bsh = jnp.where(sub >= s, pltpu.roll(seg_b, s, 0), 0.0)
                seg_b = seg_a * bsh + seg_b
                seg_a = seg_a * ash
            h_end = seg_a * carry + seg_b
            h_in = jnp.where(sub == 0, carry, pltpu.roll(h_end, 1, 0))
            hs += [acc_a[k] * h_in + acc_b[k] for k in range(SEG_LEN)]
            carry = h_end[SUBLANES - 1:SUBLANES, :]
        lru_s[:, cb] = jnp.concatenate(hs, axis=0).astype(BF16)
        hc_s[0:1, cb] = carry
        hl_s[:, cb] = jnp.where(batch_row == pl.program_id(0), carry, hl_s[:, cb])
        if n < n_pieces:
            project_piece(n)
    for i in range(N_LRU_BLOCKS, n_pieces):
        project_piece(i)
    hlast_ref[...] = hl_s[...]

    first = jnp.where(t == 0, 0, 1)

    def scores(j, g):
        q4 = jnp.concatenate(
            [q_s[j * WINDOW:(j + 1) * WINDOW, (g * GROUP + hg) * HEAD_DIM:(g * GROUP + hg + 1) * HEAD_DIM]
             for hg in range(GROUP)], axis=0)
        return _dot_nt(q4, k_s[j * WINDOW:(j + 2) * WINDOW, g * HEAD_DIM:(g + 1) * HEAD_DIM])

    def attend(j, g, s):
        variant = first if j == 0 else 1
        ps, invs = [], []
        for hg in range(GROUP):
            head = g * GROUP + hg
            sh = s[hg * WINDOW:(hg + 1) * WINDOW] + tab_ref[variant, head]
            sk = sink_ref[layer, head]
            m = jnp.maximum(jnp.max(sh, axis=-1, keepdims=True), sk)
            p = jnp.exp(sh - m)
            den = jnp.sum(p, axis=-1, keepdims=True) + jnp.exp(sk - m)
            ps.append(p.astype(BF16))
            invs.append(1.0 / den)
        o4 = _dot(jnp.concatenate(ps, axis=0),
                  v_s[j * WINDOW:(j + 2) * WINDOW, g * HEAD_DIM:(g + 1) * HEAD_DIM])
        for hg in range(GROUP):
            head = g * GROUP + hg
            attn_s[j * WINDOW:(j + 1) * WINDOW, head * HEAD_DIM:(head + 1) * HEAD_DIM] = (
                o4[hg * WINDOW:(hg + 1) * WINDOW] * invs[hg]).astype(BF16)

    pairs = [(j, g) for j in range(nblk) for g in range(N_KV)]
    s_next = scores(*pairs[0])
    for i, (j, g) in enumerate(pairs):
        s_cur = s_next
        if i + 1 < len(pairs):
            s_next = scores(*pairs[i + 1])
        attend(j, g, s_cur)

    xrc_s[...] = xr_s[tc - ndelay * SUBLANES:tc, :]
    k_s[0:WINDOW, :] = k_s[tc:tc + WINDOW, :]
    v_s[0:WINDOW, :] = v_s[tc:tc + WINDOW, :]

    merged = (g_s[:, 0:D_MODEL] * _dot(lru_s[...], w_lo_ref[...])
              + g_s[:, D_MODEL:2 * D_MODEL] * _dot(attn_s[...], w_ao_ref[...]))
    m = _dot(merged.astype(BF16), w_out_ref[...])
    y_ref[...] = x_tile[...] + _rmsnorm(m, vec_ref[V_NORM_MIX_POST:V_NORM_MIX_POST + 1, :])

    @pl.when(t == pl.num_programs(1) - 1)
    def _():
        for k in range(SEG_LEN):
            rows = slice(k * SUBLANES, (k + 1) * SUBLANES)
            for g in range(N_KV):
                kstate_ref[:, k, g, :] = kvf_s[0, rows, g * HEAD_DIM:(g + 1) * HEAD_DIM]
                vstate_ref[:, k, g, :] = kvf_s[1, rows, g * HEAD_DIM:(g + 1) * HEAD_DIM]


def _prompt_mixer(x, vecs, w_in, wri, w_lo, w_ao, w_out, tab, sink, layer, prev_states, permute_in):
    bsz, seq, _ = x.shape
    tc = PROMPT_TILE
    kern = functools.partial(_pmix_kernel, layer=layer, tc=tc, permute_in=permute_in)
    tile = pl.BlockSpec((None, tc, D_MODEL), lambda b, t: (b, t, 0))
    kv_state = pl.BlockSpec((None, None, SUBLANES, SEG_LEN, N_KV, HEAD_DIM), lambda b, t: (layer, b, 0, 0, 0, 0))
    if permute_in:
        x = x.reshape(bsz, seq // SEG_LEN, SEG_LEN, D_MODEL)
        x_spec = pl.BlockSpec((None, tc // SEG_LEN, SEG_LEN, D_MODEL), lambda b, t: (b, t, 0, 0))
    else:
        x_spec = tile
    in_specs = [
        x_spec,
        _const_spec((None, VEC_ROWS, D_MODEL), (layer, 0, 0)),
        _const_spec((None, D_MODEL, IN_COLS), (layer, 0, 0)),
        _const_spec((None, N_LRU_BLOCKS, LRU_BLOCK, 2 * LRU_BLOCK), (layer, 0, 0, 0)),
        _const_spec((None, D_RNN, D_MODEL), (layer, 0, 0)),
        _const_spec((None, Q_COLS, D_MODEL), (layer, 0, 0)),
        _const_spec((None, D_MODEL, D_MODEL), (layer, 0, 0)),
        _const_spec((2, N_HEADS, WINDOW, 2 * WINDOW), (0, 0, 0, 0)),
        pl.BlockSpec(memory_space=pltpu.SMEM),
    ]
    args = [x, vecs, w_in, wri, w_lo, w_ao, w_out, tab, sink]
    kern, in_specs, args, aliases = _with_state_aliases(kern, len(args), in_specs, args, prev_states, 1)
    return pl.pallas_call(
        kern,
        grid=(bsz, seq // tc),
        in_specs=in_specs,
        out_specs=(tile,
                   pl.BlockSpec((None, bsz, D_RNN), lambda b, t: (layer, 0, 0)),
                   pl.BlockSpec((None, None, CONV_LRU - 1, D_RNN), lambda b, t: (layer, b, 0, 0)),
                   kv_state, kv_state),
        out_shape=(jax.ShapeDtypeStruct((bsz, seq, D_MODEL), F32),
                   jax.ShapeDtypeStruct((DEPTH, bsz, D_RNN), F32),
                   jax.ShapeDtypeStruct((DEPTH, bsz, CONV_LRU - 1, D_RNN), F32),
                   jax.ShapeDtypeStruct((DEPTH, bsz, SUBLANES, SEG_LEN, N_KV, HEAD_DIM), F32),
                   jax.ShapeDtypeStruct((DEPTH, bsz, SUBLANES, SEG_LEN, N_KV, HEAD_DIM), F32)),
        input_output_aliases=aliases,
        scratch_shapes=[
            pltpu.VMEM((tc if permute_in else SUBLANES, D_MODEL), F32),
            pltpu.VMEM((tc, D_MODEL), BF16),
            pltpu.VMEM((tc, D_RNN), F32),
            pltpu.VMEM(((CONV_LRU - 1) * SUBLANES, D_RNN), F32),
            pltpu.VMEM((tc, Q_COLS), BF16),
            pltpu.VMEM((WINDOW + tc, KV_COLS), BF16),
            pltpu.VMEM((WINDOW + tc, KV_COLS), BF16),
            pltpu.VMEM((2, WINDOW, KV_COLS), F32),
            pltpu.VMEM((tc, 2 * D_MODEL), F32),
            pltpu.VMEM((tc, D_RNN), BF16),
            pltpu.VMEM((tc, Q_COLS), BF16),
            pltpu.VMEM((SUBLANES, D_RNN), F32),
            pltpu.VMEM((bsz, D_RNN), F32),
        ],
        compiler_params=pltpu.CompilerParams(
            dimension_semantics=("arbitrary", "arbitrary"), vmem_limit_bytes=VMEM_LIMIT_BYTES),
        name=f"prompt_mixer_l{layer}",
    )(*args)


def _pffn_kernel(x_ref, vec_ref, fcv_ref, w_up_ref, w_down_ref, y_ref, fstate_ref,
                 h_s, tail_s, acc_s, *, tc, permute_out):
    t = pl.program_id(1)
    nblk = tc // WINDOW
    ndelay = CONV_FF - 1

    @pl.when(t == 0)
    def _():
        tail_s[...] = jnp.zeros(tail_s.shape, F32)

    h_s[...] = _rmsnorm(x_ref[...], vec_ref[V_NORM_FFN_PRE:V_NORM_FFN_PRE + 1, :]).astype(BF16)
    hb = h_s[...]
    sub = lax.broadcasted_iota(jnp.int32, (SUBLANES, FF_CHUNK), 0)
    n_chunks = D_FF // FF_CHUNK

    def chunk_cols(c, part):
        return slice(part * D_FF + c * FF_CHUNK, part * D_FF + (c + 1) * FF_CHUNK)

    def up_project(c):
        return [_dot(hb, w_up_ref[:, chunk_cols(c, part)]) for part in range(2)]

    u_next = up_project(0)
    for c in range(n_chunks):
        u_cur = u_next
        if c + 1 < n_chunks:
            u_next = up_project(c + 1)
        conv = []
        for part in range(2):
            cols = chunk_cols(c, part)
            u = u_cur[part]
            bias = jnp.broadcast_to(fcv_ref[CONV_FF:CONV_FF + 1, cols], (SUBLANES, FF_CHUNK))
            taps = [jnp.broadcast_to(fcv_ref[j:j + 1, cols], (SUBLANES, FF_CHUNK)) for j in range(CONV_FF)]
            ys = []
            prev_tail = [tail_s[i * SUBLANES:(i + 1) * SUBLANES, cols] for i in range(ndelay)]
            for j in range(nblk):
                groups = _row_groups(u, j)
                delayed = _delayed_groups(groups, prev_tail, ndelay, sub)
                for k in range(SEG_LEN):
                    y = bias + taps[CONV_FF - 1] * groups[k]
                    for d in range(1, CONV_FF):
                        y = y + taps[CONV_FF - 1 - d] * delayed[d - 1][k]
                    ys.append(y)
                prev_tail = groups[SEG_LEN - ndelay:]
            conv.append(jnp.concatenate(ys, axis=0))
            tail_s[:, cols] = u[tc - ndelay * SUBLANES:tc]
            for i in range(ndelay):
                r = tc - (ndelay - i) * SUBLANES + SUBLANES - 1
                fstate_ref[i:i + 1, cols] = u[r:r + 1]
        act = (_gelu_tanh(conv[1]) * conv[0]).astype(BF16)
        part_f = _dot(act, w_down_ref[c * FF_CHUNK:(c + 1) * FF_CHUNK, :])
        if c == 0:
            acc_s[...] = part_f
        else:
            acc_s[...] += part_f
    y = x_ref[...] + _rmsnorm(acc_s[...], vec_ref[V_NORM_FFN_POST:V_NORM_FFN_POST + 1, :])
    if permute_out:
        for j in range(nblk):
            for k in range(SEG_LEN):
                y_ref[j * SUBLANES:(j + 1) * SUBLANES, k, :] = (
                    y[j * WINDOW + k * SUBLANES:j * WINDOW + (k + 1) * SUBLANES])
    else:
        y_ref[...] = y


def _prompt_ffn(x, vecs, fcv, w_up, w_down, layer, prev_states, permute_out):
    bsz, seq, _ = x.shape
    tc = PROMPT_TILE
    kern = functools.partial(_pffn_kernel, tc=tc, permute_out=permute_out)
    tile = pl.BlockSpec((None, tc, D_MODEL), lambda b, t: (b, t, 0))
    if permute_out:
        y_shape = (bsz, seq // SEG_LEN, SEG_LEN, D_MODEL)
        y_spec = pl.BlockSpec((None, tc // SEG_LEN, SEG_LEN, D_MODEL), lambda b, t: (b, t, 0, 0))
    else:
        y_shape, y_spec = (bsz, seq, D_MODEL), tile
    in_specs = [
        tile,
        _const_spec((None, VEC_ROWS, D_MODEL), (layer, 0, 0)),
        _const_spec((None, SUBLANES, 2 * D_FF), (layer, 0, 0)),
        _const_spec((None, D_MODEL, 2 * D_FF), (layer, 0, 0)),
        _const_spec((None, D_FF, D_MODEL), (layer, 0, 0)),
    ]
    args = [x, vecs, fcv, w_up, w_down]
    kern, in_specs, args, aliases = _with_state_aliases(kern, len(args), in_specs, args, prev_states, 1)
    return pl.pallas_call(
        kern,
        grid=(bsz, seq // tc),
        in_specs=in_specs,
        out_specs=(y_spec, pl.BlockSpec((None, None, CONV_FF - 1, 2 * D_FF), lambda b, t: (layer, b, 0, 0))),
        out_shape=(jax.ShapeDtypeStruct(y_shape, F32),
                   jax.ShapeDtypeStruct((DEPTH, bsz, CONV_FF - 1, 2 * D_FF), F32)),
        input_output_aliases=aliases,
        scratch_shapes=[
            pltpu.VMEM((tc, D_MODEL), BF16),
            pltpu.VMEM(((CONV_FF - 1) * SUBLANES, 2 * D_FF), F32),
            pltpu.VMEM((tc, D_MODEL), F32),
        ],
        compiler_params=pltpu.CompilerParams(
            dimension_semantics=("arbitrary", "arbitrary"), vmem_limit_bytes=VMEM_LIMIT_BYTES),
        name=f"prompt_ffn_l{layer}",
    )(*args)


def _smix_in_kernel(xs_ref, h0_ref, cprev_ref, vec_ref, w_in_ref, wri_ref,
                    q_ref, k_ref, v_ref, lru_ref, gate_ref, hlast_ref, cstate_ref,
                    h_s, xr_s, *, nb, nt):
    for t in range(nt):
        h_s[t * nb:(t + 1) * nb, :] = _rmsnorm(
            xs_ref[:, t, :], vec_ref[V_NORM_MIX_PRE:V_NORM_MIX_PRE + 1, :]).astype(BF16)
    hb = h_s[...]
    xr_s[...] = _dot(hb, w_in_ref[:, 0:C_Q])
    qf = _dot(hb, w_in_ref[:, C_Q:C_K]) * (HEAD_DIM ** -0.5)
    kf = _dot(hb, w_in_ref[:, C_K:C_V])
    vf = _dot(hb, w_in_ref[:, C_V:C_G])
    for t in range(nt):
        rows = slice(t * nb, (t + 1) * nb)
        q_ref[:, t * Q_COLS:(t + 1) * Q_COLS] = qf[rows].astype(BF16)
        k_ref[:, t, :] = kf[rows]
        v_ref[:, t, :] = vf[rows]
    gate_ref[...] = _sigmoid(_dot(hb, w_in_ref[:, C_G:IN_COLS]))

    npre = CONV_LRU - 1
    for t in range(nt - npre, nt):
        cstate_ref[:, t - (nt - npre), :] = xr_s[t * nb:(t + 1) * nb, :]

    for n in range(N_LRU_BLOCKS):
        cb = slice(n * LRU_BLOCK, (n + 1) * LRU_BLOCK)
        xx = [cprev_ref[:, j, cb] for j in range(npre)]
        xx += [xr_s[t * nb:(t + 1) * nb, cb] for t in range(nt)]
        xcs = []
        for t in range(nt):
            xc = vec_ref[V_CONV_B:V_CONV_B + 1, cb]
            for j in range(CONV_LRU):
                xc = xc + vec_ref[V_CONV_W + j:V_CONV_W + j + 1, cb] * xx[t + j]
            xcs.append(xc)
        a, ix, mult = _lru_coeffs(jnp.concatenate(xcs, axis=0), vec_ref, wri_ref, n)
        b = mult * ix
        h = h0_ref[:, cb]
        for t in range(nt):
            rows = slice(t * nb, (t + 1) * nb)
            h = a[rows] * h + b[rows]
            lru_ref[rows, cb] = h.astype(BF16)
        hlast_ref[:, cb] = h


def _sample_mixer_in(xs, h0, cprev, vecs, w_in, wri, layer, prev_states):
    nb, nt, _ = xs.shape
    assert PAST_LEN > 0
    kern = functools.partial(_smix_in_kernel, nb=nb, nt=nt)
    whole = lambda shape: pl.BlockSpec(shape, lambda i: (0,) * len(shape))
    conv_state = pl.BlockSpec((None, nb, CONV_LRU - 1, D_RNN), lambda i: (layer, 0, 0, 0))
    in_specs = [
        whole((nb, nt, D_MODEL)),
        pl.BlockSpec((None, nb, D_RNN), lambda i: (layer, 0, 0)),
        conv_state,
        _const_spec((None, VEC_ROWS, D_MODEL), (layer, 0, 0)),
        _const_spec((None, D_MODEL, IN_COLS), (layer, 0, 0)),
        _const_spec((None, N_LRU_BLOCKS, LRU_BLOCK, 2 * LRU_BLOCK), (layer, 0, 0, 0)),
    ]
    args = [xs, h0, cprev, vecs, w_in, wri]
    kern, in_specs, args, aliases = _with_state_aliases(kern, len(args), in_specs, args, prev_states, 5)
    return pl.pallas_call(
        kern,
        grid=(1,),
        in_specs=in_specs,
        out_specs=(whole((nb, nt * Q_COLS)), whole((nb, nt, KV_COLS)), whole((nb, nt, KV_COLS)),
                   whole((nt * nb, D_RNN)), whole((nt * nb, 2 * D_MODEL)),
                   pl.BlockSpec((None, nb, D_RNN), lambda i: (layer, 0, 0)), conv_state),
        out_shape=(jax.ShapeDtypeStruct((nb, nt * Q_COLS), BF16),
                   jax.ShapeDtypeStruct((nb, nt, KV_COLS), F32),
                   jax.ShapeDtypeStruct((nb, nt, KV_COLS), F32),
                   jax.ShapeDtypeStruct((nt * nb, D_RNN), BF16),
                   jax.ShapeDtypeStruct((nt * nb, 2 * D_MODEL), F32),
                   jax.ShapeDtypeStruct((DEPTH, nb, D_RNN), F32),
                   jax.ShapeDtypeStruct((DEPTH, nb, CONV_LRU - 1, D_RNN), F32)),
        input_output_aliases=aliases,
        scratch_shapes=[pltpu.VMEM((nt * nb, D_MODEL), BF16), pltpu.VMEM((nt * nb, D_RNN), F32)],
        compiler_params=pltpu.CompilerParams(
            dimension_semantics=("arbitrary",), vmem_limit_bytes=VMEM_LIMIT_BYTES),
        name=f"sample_mixer_in_l{layer}",
    )(*args)


def _sattn_kernel(q_ref, kn_ref, vn_ref, ck_ref, cv_ref, tab_ref, attn_ref, sk_ref, sv_ref,
                  kc_s, vc_s, *, bt, nt):
    pad = SAMPLE_KEYS - WINDOW
    for g in range(N_KV):
        kc_s[g, WINDOW:SAMPLE_KEYS, :] = jnp.zeros((pad, HEAD_DIM), F32)
        vc_s[g, WINDOW:SAMPLE_KEYS, :] = jnp.zeros((pad, HEAD_DIM), F32)
    row = lax.broadcasted_iota(jnp.int32, (nt * N_HEADS, HEAD_DIM), 0)
    in_group0 = lax.bitwise_and(row, N_HEADS - 1) < GROUP

    def body(b, _):
        qb = q_ref[b]
        outs = []
        for g in range(N_KV):
            kvc = slice(g * HEAD_DIM, (g + 1) * HEAD_DIM)
            kc_s[g, 0:WINDOW, :] = ck_ref[b, :, g, :]
            vc_s[g, 0:WINDOW, :] = cv_ref[b, :, g, :]
            kc_s[g, WINDOW:WINDOW + nt, :] = kn_ref[b, :, kvc]
            vc_s[g, WINDOW:WINDOW + nt, :] = vn_ref[b, :, kvc]
            sk_ref[b, :, g, :] = kc_s[g, nt:nt + WINDOW, :]
            sv_ref[b, :, g, :] = vc_s[g, nt:nt + WINDOW, :]
            s = _dot_nt(qb, kc_s[g].astype(BF16)) + tab_ref[...]
            m = jnp.max(s, axis=-1, keepdims=True)
            p = jnp.exp(s - m)
            den = jnp.sum(p, axis=-1, keepdims=True)
            outs.append(_dot(p.astype(BF16), vc_s[g].astype(BF16)) * (1.0 / den))
        attn_ref[b] = jnp.where(in_group0, outs[0], outs[1]).astype(BF16)
        return 0

    lax.fori_loop(0, bt, body, 0)


def _sample_attention(q, kn, vn, cache_k, cache_v, stab, layer, prev_states):
    nb, rows, _ = q.shape
    nt = rows // N_HEADS
    bt = SAMPLE_BATCH_TILE
    kern = functools.partial(_sattn_kernel, bt=bt, nt=nt)
    cache_spec = pl.BlockSpec((None, bt, WINDOW, N_KV, HEAD_DIM), lambda i: (layer, i, 0, 0, 0))
    new_spec = pl.BlockSpec((bt, nt, KV_COLS), lambda i: (i, 0, 0))
    q_spec = pl.BlockSpec((bt, rows, HEAD_DIM), lambda i: (i, 0, 0))
    in_specs = [q_spec, new_spec, new_spec, cache_spec, cache_spec,
                _const_spec((None, rows, SAMPLE_KEYS), (layer, 0, 0))]
    args = [q, kn, vn, cache_k, cache_v, stab]
    kern, in_specs, args, aliases = _with_state_aliases(kern, len(args), in_specs, args, prev_states, 1)
    return pl.pallas_call(
        kern,
        grid=(nb // bt,),
        in_specs=in_specs,
        out_specs=(q_spec, cache_spec, cache_spec),
        out_shape=(jax.ShapeDtypeStruct((nb, rows, HEAD_DIM), BF16),
                   jax.ShapeDtypeStruct((DEPTH, nb, WINDOW, N_KV, HEAD_DIM), F32),
                   jax.ShapeDtypeStruct((DEPTH, nb, WINDOW, N_KV, HEAD_DIM), F32)),
        input_output_aliases=aliases,
        scratch_shapes=[pltpu.VMEM((N_KV, SAMPLE_KEYS, HEAD_DIM), F32),
                        pltpu.VMEM((N_KV, SAMPLE_KEYS, HEAD_DIM), F32)],
        compiler_params=pltpu.CompilerParams(
            dimension_semantics=("arbitrary",), vmem_limit_bytes=VMEM_LIMIT_BYTES),
        name=f"sample_attention_l{layer}",
    )(*args)


def _smix_out_kernel(xs_ref, lru_ref, attn_ref, gate_ref, vec_ref, w_lo_ref, w_ao_ref, w_out_ref,
                     xmid_ref, a_s, *, nb, nt):
    for t in range(nt):
        a_s[t * nb:(t + 1) * nb, :] = attn_ref[:, t * Q_COLS:(t + 1) * Q_COLS]
    merged = (gate_ref[:, 0:D_MODEL] * _dot(lru_ref[...], w_lo_ref[...])
              + gate_ref[:, D_MODEL:2 * D_MODEL] * _dot(a_s[...], w_ao_ref[...]))
    m = _dot(merged.astype(BF16), w_out_ref[...])
    mn = _rmsnorm(m, vec_ref[V_NORM_MIX_POST:V_NORM_MIX_POST + 1, :])
    for t in range(nt):
        rows = slice(t * nb, (t + 1) * nb)
        xmid_ref[rows, :] = xs_ref[:, t, :] + mn[rows]


def _sample_mixer_out(xs, lru, attn, gates, vecs, w_lo, w_ao, w_out, layer):
    nb, nt, _ = xs.shape
    kern = functools.partial(_smix_out_kernel, nb=nb, nt=nt)
    whole = lambda shape: pl.BlockSpec(shape, lambda i: (0,) * len(shape))
    return pl.pallas_call(
        kern,
        grid=(1,),
        in_specs=[
            whole(xs.shape), whole(lru.shape), whole(attn.shape), whole(gates.shape),
            _const_spec((None, VEC_ROWS, D_MODEL), (layer, 0, 0)),
            _const_spec((None, D_RNN, D_MODEL), (layer, 0, 0)),
            _const_spec((None, Q_COLS, D_MODEL), (layer, 0, 0)),
            _const_spec((None, D_MODEL, D_MODEL), (layer, 0, 0)),
        ],
        out_specs=whole((nt * nb, D_MODEL)),
        out_shape=jax.ShapeDtypeStruct((nt * nb, D_MODEL), F32),
        scratch_shapes=[pltpu.VMEM((nt * nb, Q_COLS), BF16)],
        compiler_params=pltpu.CompilerParams(
            dimension_semantics=("arbitrary",), vmem_limit_bytes=VMEM_LIMIT_BYTES),
        name=f"sample_mixer_out_l{layer}",
    )(xs, lru, attn, gates, vecs, w_lo, w_ao, w_out)


def _sffn_kernel(x_ref, vec_ref, fcv_ref, prev_ref, w_ref, wd_ref, ys_ref, fstate_ref,
                 h_s, val_s, acc_s, *, nb, nt, nck):
    c = pl.program_id(0)

    @pl.when(c == 0)
    def _():
        h_s[...] = _rmsnorm(x_ref[...], vec_ref[V_NORM_FFN_PRE:V_NORM_FFN_PRE + 1, :]).astype(BF16)
        acc_s[...] = jnp.zeros(acc_s.shape, F32)

    u = _dot(h_s[...], w_ref[...])
    uu = [prev_ref[:, j, :] for j in range(CONV_FF - 1)] + [u[t * nb:(t + 1) * nb] for t in range(nt)]
    ys = []
    for t in range(nt):
        y = fcv_ref[CONV_FF:CONV_FF + 1, :]
        for j in range(CONV_FF):
            y = y + fcv_ref[j:j + 1, :] * uu[t + j]
        ys.append(y)
    conv = jnp.concatenate(ys, axis=0)
    for j in range(CONV_FF - 1):
        fstate_ref[:, j, :] = uu[nt + j]

    @pl.when(c < nck)
    def _():
        val_s[c] = conv

    @pl.when(c >= nck)
    def _():
        act = (_gelu_tanh(conv) * val_s[c - nck]).astype(BF16)
        acc_s[...] += _dot(act, wd_ref[...])

    @pl.when(c == 2 * nck - 1)
    def _():
        y = x_ref[...] + _rmsnorm(acc_s[...], vec_ref[V_NORM_FFN_POST:V_NORM_FFN_POST + 1, :])
        for t in range(nt):
            ys_ref[:, t, :] = y[t * nb:(t + 1) * nb]


def _sample_ffn(xmid, fprev, vecs, fcv, w_up, w_down, layer, nb, prev_states):
    nt = xmid.shape[0] // nb
    ck = FF_CHUNK
    nck = D_FF // ck
    kern = functools.partial(_sffn_kernel, nb=nb, nt=nt, nck=nck)
    state_spec = pl.BlockSpec((None, nb, CONV_FF - 1, ck), lambda c: (layer, 0, 0, c))
    in_specs = [
        _const_spec((nt * nb, D_MODEL), (0, 0)),
        _const_spec((None, VEC_ROWS, D_MODEL), (layer, 0, 0)),
        pl.BlockSpec((None, SUBLANES, ck), lambda c: (layer, 0, c)),
        state_spec,
        pl.BlockSpec((None, D_MODEL, ck), lambda c: (layer, 0, c)),
        pl.BlockSpec((None, ck, D_MODEL), lambda c: (layer, jnp.maximum(c - nck, 0), 0)),
    ]
    args = [xmid, vecs, fcv, fprev, w_up, w_down]
    kern, in_specs, args, aliases = _with_state_aliases(kern, len(args), in_specs, args, prev_states, 1)
    return pl.pallas_call(
        kern,
        grid=(2 * nck,),
        in_specs=in_specs,
        out_specs=(pl.BlockSpec((nb, nt, D_MODEL), lambda c: (0, 0, 0)), state_spec),
        out_shape=(jax.ShapeDtypeStruct((nb, nt, D_MODEL), F32),
                   jax.ShapeDtypeStruct((DEPTH, nb, CONV_FF - 1, 2 * D_FF), F32)),
        input_output_aliases=aliases,
        scratch_shapes=[pltpu.VMEM((nt * nb, D_MODEL), BF16),
                        pltpu.VMEM((nck, nt * nb, ck), F32),
                        pltpu.VMEM((nt * nb, D_MODEL), F32)],
        compiler_params=pltpu.CompilerParams(
            dimension_semantics=("arbitrary",), vmem_limit_bytes=VMEM_LIMIT_BYTES),
        name=f"sample_ffn_l{layer}",
    )(*args)


def kernel(x_prompt, x_sample, state_lru_h, state_lru_conv, cache_win_k, cache_win_v, state_ffn_conv,
           norm_mix_pre, norm_mix_post, norm_ffn_pre, norm_ffn_post, w_in, conv_lru_w, conv_lru_b,
           lru_wr, lru_br, lru_wi, lru_bi, lru_lambda, w_lru_o, w_attn_o, w_out, attn_sink, rel_bias,
           w_up, ffn_conv_w, ffn_conv_b, w_down):
    nb, nt, _ = x_sample.shape
    bp = x_prompt.shape[0]

    row = lambda v: v[:, None, :]
    vecs = jnp.concatenate(
        [row(norm_mix_pre), row(norm_mix_post), row(conv_lru_b), row(lru_br), row(lru_bi), row(lru_lambda),
         conv_lru_w, row(norm_ffn_pre), row(norm_ffn_post),
         jnp.zeros((DEPTH, VEC_ROWS - 12, D_MODEL), F32)], axis=1)
    fcv = jnp.concatenate(
        [ffn_conv_w, row(ffn_conv_b), jnp.zeros((DEPTH, SUBLANES - CONV_FF - 1, 2 * D_FF), F32)], axis=1)
    w_in_b = w_in.astype(BF16)
    wri_b = jnp.concatenate([lru_wr, lru_wi], axis=-1).astype(BF16)
    w_lo_b = w_lru_o.astype(BF16)
    w_ao_b = w_attn_o.astype(BF16)
    w_out_b = w_out.astype(BF16)
    w_up_b = w_up.astype(BF16)
    w_down_b = w_down.astype(BF16)

    ptab, stab = _bias_tables(rel_bias, attn_sink)

    yp, xs = x_prompt, x_sample
    p_mix = p_ffn = s_mix = s_att = s_ffn = None
    for l in range(DEPTH):
        yp, *p_mix = _prompt_mixer(yp, vecs, w_in_b, wri_b, w_lo_b, w_ao_b, w_out_b, ptab, attn_sink, l, p_mix,
                                   permute_in=(l == 0))
        yp, *p_ffn = _prompt_ffn(yp, vecs, fcv, w_up_b, w_down_b, l, p_ffn, permute_out=(l == DEPTH - 1))

        q, kn, vn, lru, gates, *s_mix = _sample_mixer_in(
            xs, state_lru_h, state_lru_conv, vecs, w_in_b, wri_b, l, s_mix)
        attn, *s_att = _sample_attention(
            q.reshape(nb, nt * N_HEADS, HEAD_DIM), kn, vn, cache_win_k, cache_win_v, stab, l, s_att)
        xmid = _sample_mixer_out(xs, lru, attn.reshape(nb, nt * Q_COLS), gates, vecs, w_lo_b, w_ao_b, w_out_b, l)
        xs, *s_ffn = _sample_ffn(xmid, state_ffn_conv, vecs, fcv, w_up_b, w_down_b, l, nb, s_ffn)

    p_h, p_c, p_k, p_v = p_mix
    s_h, s_c = s_mix
    s_k, s_v = s_att
    kv_shape = (DEPTH, bp, WINDOW, N_KV, HEAD_DIM)
    return (yp.reshape(x_prompt.shape), xs, p_h, p_c, p_k.reshape(kv_shape), p_v.reshape(kv_shape), p_ffn[0],
            s_h, s_c, s_k, s_v, s_ffn[0])
```

```python
import functools
import math

import numpy as np
import jax
import jax.numpy as jnp
from jax import lax
from jax.experimental import pallas as pl
from jax.experimental.pallas import tpu as pltpu

D_MODEL = 1024
DEPTH = 2
PAST_LEN = 16384
D_RNN = D_MODEL
N_LRU_BLOCKS = 8
LRU_BLOCK = D_RNN // N_LRU_BLOCKS
CONV_LRU = 4
LRU_C = 8.0
N_HEADS = 8
N_KV = 2
GROUP = N_HEADS // N_KV
HEAD_DIM = D_MODEL // N_HEADS
WINDOW = 128
N_BUCKETS = 32
MAX_EXACT = N_BUCKETS // 2
MAX_DISTANCE = 128
D_FF = 4 * D_MODEL
CONV_FF = 3
EPS = 1e-6
Q_COLS = N_HEADS * HEAD_DIM
KV_COLS = N_KV * HEAD_DIM
IN_COLS = D_RNN + Q_COLS + 2 * KV_COLS + 2 * D_MODEL
C_Q = D_RNN
C_K = C_Q + Q_COLS
C_V = C_K + KV_COLS
C_G = C_V + KV_COLS

F32 = jnp.float32
BF16 = jnp.bfloat16

SUBLANES = 8
LANES = 128
VMEM_LIMIT_BYTES = 56 * 1024 * 1024

V_NORM_MIX_PRE, V_NORM_MIX_POST, V_CONV_B, V_BR, V_BI, V_LAMBDA, V_CONV_W = 0, 1, 2, 3, 4, 5, 6
V_NORM_FFN_PRE, V_NORM_FFN_POST = 10, 11
VEC_ROWS = 16

PROMPT_TILE = 512
WEIGHT_PAD = LANES
PROJ_PIECE = 2 * KV_COLS
assert (C_K - C_Q) % PROJ_PIECE == 0 and (IN_COLS - C_G) % PROJ_PIECE == 0
FF_CHUNK = 512
SAMPLE_KEYS = N_KV * (WINDOW + 4) + SUBLANES
SINK_COL = N_KV * (WINDOW + 4)
SAMPLE_BATCH_TILE = 16
SAMPLE_ATTN_UNROLL = 8
SEG_LEN = WINDOW // SUBLANES


def _bucket_thresholds():
    d = np.arange(0, 2 * WINDOW)
    nf = np.maximum(d, 1).astype(np.float64)
    large = MAX_EXACT + (np.log(nf / MAX_EXACT) / math.log(MAX_DISTANCE / MAX_EXACT)
                         * (N_BUCKETS - MAX_EXACT)).astype(np.int64)
    bucket = np.where(d < MAX_EXACT, d, np.minimum(large, N_BUCKETS - 1))
    return tuple(int(d[bucket >= b].min()) for b in range(1, N_BUCKETS))


_BUCKET_THRESHOLDS = _bucket_thresholds()


def _dot(a, b):
    return jnp.dot(a, b, preferred_element_type=F32)


def _dot_nt(a, b):
    return lax.dot_general(a, b, (((1,), (1,)), ((), ())), preferred_element_type=F32)


def _rmsnorm(x, g):
    return x * lax.rsqrt(jnp.mean(x * x, axis=-1, keepdims=True) + EPS) * g


def _sigmoid(x):
    return 1.0 / (1.0 + jnp.exp(-x))


def _gelu_tanh_doubled(x):
    c = math.sqrt(2.0 / math.pi)
    t = jnp.tanh(x * (c + (c * 0.044715) * (x * x)))
    return x + x * t


def _const_spec(block_shape, index):
    return pl.BlockSpec(block_shape, lambda *_: index, pipeline_mode=pl.Buffered(1))


def _with_state_aliases(kern, n_in, in_specs, args, prev_states, first_state_out):
    if prev_states is None:
        return kern, list(in_specs), list(args), {}
    n = len(prev_states)

    def body(*refs):
        return kern(*refs[:n_in], *refs[n_in + n:])

    return (body, list(in_specs) + [pl.BlockSpec(memory_space=pl.ANY)] * n, list(args) + list(prev_states),
            {n_in + i: first_state_out + i for i in range(n)})


def _bucket_of(d):
    n = jnp.maximum(d, 0)
    bucket = jnp.zeros(d.shape, jnp.int32)
    for thr in _BUCKET_THRESHOLDS:
        bucket = bucket + jnp.where(n >= thr, 1, 0)
    return bucket


def _block_time(p):
    return lax.bitwise_and(p, SUBLANES - 1) * SEG_LEN + lax.shift_right_logical(p, 3)


def _table_kernel(rel_ref, sink_ref, pt_ref, st_ref):
    qi = lax.broadcasted_iota(jnp.int32, (WINDOW, 2 * WINDOW), 0)
    kj = lax.broadcasted_iota(jnp.int32, (WINDOW, 2 * WINDOW), 1)
    kpos = lax.bitwise_and(kj, WINDOW - 1)
    d = _block_time(qi) + WINDOW - (_block_time(kpos) + (kj - kpos))
    bucket = _bucket_of(d)
    in_band = jnp.where(d >= 0, jnp.where(d < WINDOW, 1, 0), 0)
    cur_only = jnp.where(kj >= WINDOW, in_band, 0)
    for h in range(N_HEADS):
        val = jnp.zeros(d.shape, F32)
        for b in range(N_BUCKETS):
            val = jnp.where(bucket == b, rel_ref[b, h], val)
        pt_ref[1, h] = jnp.where(in_band == 1, val, -jnp.inf)
        pt_ref[0, h] = jnp.where(cur_only == 1, val, -jnp.inf)

    r = lax.broadcasted_iota(jnp.int32, (4 * N_HEADS, SAMPLE_KEYS), 0)
    j = lax.broadcasted_iota(jnp.int32, (4 * N_HEADS, SAMPLE_KEYS), 1)
    t = lax.shift_right_logical(r, 3)
    hh = lax.bitwise_and(r, N_HEADS - 1)
    d = t + WINDOW - lax.shift_right_logical(j, 1)
    bucket = _bucket_of(d)
    in_band = jnp.where(d >= 0, jnp.where(d < WINDOW, 1, 0), 0)
    in_band = jnp.where(lax.bitwise_and(j, N_KV - 1) == lax.shift_right_logical(hh, 2), in_band, 0)
    val = jnp.zeros(d.shape, F32)
    for h in range(N_HEADS):
        hval = jnp.zeros(d.shape, F32)
        for b in range(N_BUCKETS):
            hval = jnp.where(bucket == b, rel_ref[b, h], hval)
        val = jnp.where(hh == h, hval, val)
    val = jnp.where(in_band == 1, val, -jnp.inf)
    for l in range(DEPTH):
        sk = jnp.zeros(d.shape, F32)
        for h in range(N_HEADS):
            sk = jnp.where(hh == h, sink_ref[l, h], sk)
        st_ref[l] = jnp.where(j == SINK_COL, sk, val)


def _bias_tables(rel_bias, attn_sink):
    smem = pl.BlockSpec(memory_space=pltpu.SMEM)
    return pl.pallas_call(
        _table_kernel,
        out_shape=(jax.ShapeDtypeStruct((2, N_HEADS, WINDOW, 2 * WINDOW), F32),
                   jax.ShapeDtypeStruct((DEPTH, 4 * N_HEADS, SAMPLE_KEYS), F32)),
        in_specs=[smem, smem],
        name="bias_tables",
    )(rel_bias, attn_sink)


def _lru_coeffs(xc, vec_ref, wri_ref, n):
    cb = slice(n * LRU_BLOCK, (n + 1) * LRU_BLOCK)
    rw = _dot(xc.astype(BF16), wri_ref[n])
    r = _sigmoid(rw[:, :LRU_BLOCK] + vec_ref[V_BR:V_BR + 1, cb])
    i = _sigmoid(rw[:, LRU_BLOCK:] + vec_ref[V_BI:V_BI + 1, cb])
    z = -vec_ref[V_LAMBDA:V_LAMBDA + 1, cb]
    softplus = jnp.maximum(z, 0.0) + jnp.log1p(jnp.exp(-jnp.abs(z)))
    log_a = (-LRU_C * softplus) * r
    a = jnp.exp(log_a)
    mult = jnp.sqrt(jnp.maximum(1.0 - a * a, 0.0))
    return a, i * xc, mult


def _delayed_groups(groups, prev_tail, ndelay, sub):
    ng = len(groups)
    wrapped = {}
    for i in range(ndelay):
        k = ng - ndelay + i
        wrapped[k] = jnp.where(sub == 0, pltpu.roll(prev_tail[i], 1, 0), pltpu.roll(groups[k], 1, 0))
    return [[groups[k - d] if k >= d else wrapped[ng + k - d] for k in range(ng)] for d in range(1, ndelay + 1)]


def _row_groups(x, block):
    return [x[block * WINDOW + k * SUBLANES:block * WINDOW + (k + 1) * SUBLANES] for k in range(SEG_LEN)]


def _pmix_kernel(x_ref, vec_ref, w_in_ref, wri_ref, w_lo_ref, w_ao_ref, w_out_ref, tab_ref, sink_ref,
                 y_ref, hlast_ref, cstate_ref, kstate_ref, vstate_ref,
                 xp_s, h_s, xr_s, xrc_s, q_s, k_s, v_s, kvf_s, g_s, lru_s, attn_s, hc_s, hl_s, *, layer, tc, permute_in):
    t = pl.program_id(1)
    nblk = tc // WINDOW
    ndelay = CONV_LRU - 1
    n_pieces = (IN_COLS - C_Q) // PROJ_PIECE
    batch_row = lax.broadcasted_iota(jnp.int32, (hl_s.shape[0], LRU_BLOCK), 0)

    @pl.when(jnp.logical_and(t == 0, pl.program_id(0) == 0))
    def _():
        hl_s[...] = jnp.zeros(hl_s.shape, F32)

    @pl.when(t == 0)
    def _():
        xrc_s[...] = jnp.zeros(xrc_s.shape, F32)
        k_s[0:WINDOW, :] = jnp.zeros((WINDOW, KV_COLS), BF16)
        v_s[0:WINDOW, :] = jnp.zeros((WINDOW, KV_COLS), BF16)
        hc_s[...] = jnp.zeros(hc_s.shape, F32)

    if permute_in:
        for j in range(nblk):
            for k in range(SEG_LEN):
                xp_s[j * WINDOW + k * SUBLANES:j * WINDOW + (k + 1) * SUBLANES, :] = (
                    x_ref[j * SUBLANES:(j + 1) * SUBLANES, k, :])
        x_tile = xp_s
    else:
        x_tile = x_ref

    h_s[...] = _rmsnorm(x_tile[...], vec_ref[V_NORM_MIX_PRE:V_NORM_MIX_PRE + 1, :]).astype(BF16)
    hb = h_s[...]
    xr_s[...] = _dot(hb, w_in_ref[:, 0:C_Q])

    def project_piece(i):
        c0 = C_Q + i * PROJ_PIECE
        z = _dot(hb, w_in_ref[:, c0:c0 + PROJ_PIECE])
        if c0 < C_K:
            q_s[:, c0 - C_Q:c0 - C_Q + PROJ_PIECE] = (z * (HEAD_DIM ** -0.5)).astype(BF16)
        elif c0 == C_K:
            k_s[WINDOW:WINDOW + tc, :] = z[:, 0:KV_COLS].astype(BF16)
            v_s[WINDOW:WINDOW + tc, :] = z[:, KV_COLS:2 * KV_COLS].astype(BF16)
            kvf_s[0] = z[tc - WINDOW:tc, 0:KV_COLS]
            kvf_s[1] = z[tc - WINDOW:tc, KV_COLS:2 * KV_COLS]
        else:
            g_s[:, c0 - C_G:c0 - C_G + PROJ_PIECE] = _sigmoid(z)

    for i in range(ndelay):
        r = tc - (ndelay - i) * SUBLANES + SUBLANES - 1
        cstate_ref[i:i + 1, :] = xr_s[r:r + 1, :]

    sub = lax.broadcasted_iota(jnp.int32, (SUBLANES, LRU_BLOCK), 0)
    seq_start = (sub + t) == 0
    for n in range(N_LRU_BLOCKS):
        cb = slice(n * LRU_BLOCK, (n + 1) * LRU_BLOCK)
        bias = jnp.broadcast_to(vec_ref[V_CONV_B:V_CONV_B + 1, cb], (SUBLANES, LRU_BLOCK))
        taps = [jnp.broadcast_to(vec_ref[V_CONV_W + j:V_CONV_W + j + 1, cb], (SUBLANES, LRU_BLOCK))
                for j in range(CONV_LRU)]
        xcs = []
        prev_tail = [xrc_s[i * SUBLANES:(i + 1) * SUBLANES, cb] for i in range(ndelay)]
        for j in range(nblk):
            groups = [xr_s[j * WINDOW + k * SUBLANES:j * WINDOW + (k + 1) * SUBLANES, cb] for k in range(SEG_LEN)]
            delayed = _delayed_groups(groups, prev_tail, ndelay, sub)
            for k in range(SEG_LEN):
                xc = bias + taps[CONV_LRU - 1] * groups[k]
                for d in range(1, CONV_LRU):
                    xc = xc + taps[CONV_LRU - 1 - d] * delayed[d - 1][k]
                xcs.append(xc)
            prev_tail = groups[SEG_LEN - ndelay:]
        a, ix, mult = _lru_coeffs(jnp.concatenate(xcs, axis=0), vec_ref, wri_ref, n)
        b = mult * ix

        carry = hc_s[0:1, cb]
        hs = []
        for j in range(nblk):
            ag, bg, ig = _row_groups(a, j), _row_groups(b, j), _row_groups(ix, j)
            if j == 0:
                bg[0] = jnp.where(seq_start, ig[0], bg[0])
            acc_a, acc_b = [ag[0]], [bg[0]]
            for k in range(1, SEG_LEN):
                acc_b.append(ag[k] * acc_b[-1] + bg[k])
                acc_a.append(ag[k] * acc_a[-1])
            seg_a, seg_b = acc_a[-1], acc_b[-1]
            for s in (1, 2, 4):
                ash = jnp.where(sub >= s, pltpu.roll(seg_a, s, 0), 1.0)
                bsh = jnp.where(sub >= s, pltpu.roll(seg_b, s, 0), 0.0)
                seg_b = seg_a * bsh + seg_b
                seg_a = seg_a * ash
            h_end = seg_a * carry + seg_b
            h_in = jnp.where(sub == 0, carry, pltpu.roll(h_end, 1, 0))
            hs += [acc_a[k] * h_in + acc_b[k] for k in range(SEG_LEN)]
            carry = h_end[SUBLANES - 1:SUBLANES, :]
        lru_s[:, cb] = jnp.concatenate(hs, axis=0).astype(BF16)
        hc_s[0:1, cb] = carry
        hl_s[:, cb] = jnp.where(batch_row == pl.program_id(0), carry, hl_s[:, cb])
        if n < n_pieces:
            project_piece(n)
    for i in range(N_LRU_BLOCKS, n_pieces):
        project_piece(i)
    hlast_ref[...] = hl_s[...]

    first = jnp.where(t == 0, 0, 1)

    def scores(j, g):
        q4 = jnp.concatenate(
            [q_s[j * WINDOW:(j + 1) * WINDOW, (g * GROUP + hg) * HEAD_DIM:(g * GROUP + hg + 1) * HEAD_DIM]
             for hg in range(GROUP)], axis=0)
        return _dot_nt(q4, k_s[j * WINDOW:(j + 2) * WINDOW, g * HEAD_DIM:(g + 1) * HEAD_DIM])

    def attend(j, g, s):
        variant = first if j == 0 else 1
        ps, invs = [], []
        for hg in range(GROUP):
            head = g * GROUP + hg
            sh = s[hg * WINDOW:(hg + 1) * WINDOW] + tab_ref[variant, head]
            sk = sink_ref[layer, head]
            m = jnp.maximum(jnp.max(sh, axis=-1, keepdims=True), sk)
            p = jnp.exp(sh - m)
            den = jnp.sum(p, axis=-1, keepdims=True) + jnp.exp(sk - m)
            ps.append(p.astype(BF16))
            invs.append(1.0 / den)
        o4 = _dot(jnp.concatenate(ps, axis=0),
                  v_s[j * WINDOW:(j + 2) * WINDOW, g * HEAD_DIM:(g + 1) * HEAD_DIM])
        for hg in range(GROUP):
            head = g * GROUP + hg
            attn_s[j * WINDOW:(j + 1) * WINDOW, head * HEAD_DIM:(head + 1) * HEAD_DIM] = (
                o4[hg * WINDOW:(hg + 1) * WINDOW] * invs[hg]).astype(BF16)

    pairs = [(j, g) for j in range(nblk) for g in range(N_KV)]
    s_next = scores(*pairs[0])
    for i, (j, g) in enumerate(pairs):
        s_cur = s_next
        if i + 1 < len(pairs):
            s_next = scores(*pairs[i + 1])
        attend(j, g, s_cur)

    xrc_s[...] = xr_s[tc - ndelay * SUBLANES:tc, :]
    k_s[0:WINDOW, :] = k_s[tc:tc + WINDOW, :]
    v_s[0:WINDOW, :] = v_s[tc:tc + WINDOW, :]

    merged = (g_s[:, 0:D_MODEL] * _dot(lru_s[...], w_lo_ref[...])
              + g_s[:, D_MODEL:2 * D_MODEL] * _dot(attn_s[...], w_ao_ref[...]))
    m = _dot(merged.astype(BF16), w_out_ref[...])
    y_ref[...] = x_tile[...] + _rmsnorm(m, vec_ref[V_NORM_MIX_POST:V_NORM_MIX_POST + 1, :])

    @pl.when(t == pl.num_programs(1) - 1)
    def _():
        for k in range(SEG_LEN):
            rows = slice(k * SUBLANES, (k + 1) * SUBLANES)
            for g in range(N_KV):
                kstate_ref[:, k, g, :] = kvf_s[0, rows, g * HEAD_DIM:(g + 1) * HEAD_DIM]
                vstate_ref[:, k, g, :] = kvf_s[1, rows, g * HEAD_DIM:(g + 1) * HEAD_DIM]


def _prompt_mixer(x, vecs, w_in, wri, w_lo, w_ao, w_out, tab, sink, layer, prev_states, permute_in):
    bsz, seq, _ = x.shape
    tc = PROMPT_TILE
    kern = functools.partial(_pmix_kernel, layer=layer, tc=tc, permute_in=permute_in)
    tile = pl.BlockSpec((None, tc, D_MODEL), lambda b, t: (b, t, 0))
    kv_state = pl.BlockSpec((None, None, SUBLANES, SEG_LEN, N_KV, HEAD_DIM), lambda b, t: (layer, b, 0, 0, 0, 0))
    if permute_in:
        x = x.reshape(bsz, seq // SEG_LEN, SEG_LEN, D_MODEL)
        x_spec = pl.BlockSpec((None, tc // SEG_LEN, SEG_LEN, D_MODEL), lambda b, t: (b, t, 0, 0))
    else:
        x_spec = tile
    in_specs = [
        x_spec,
        _const_spec((None, VEC_ROWS, D_MODEL), (layer, 0, 0)),
        _const_spec((None, D_MODEL, IN_COLS), (layer, 0, 0)),
        _const_spec((None, N_LRU_BLOCKS, LRU_BLOCK, 2 * LRU_BLOCK), (layer, 0, 0, 0)),
        _const_spec((None, D_RNN, D_MODEL), (layer, 0, 0)),
        _const_spec((None, Q_COLS, D_MODEL), (layer, 0, 0)),
        _const_spec((None, D_MODEL, D_MODEL), (layer, 0, 0)),
        _const_spec((2, N_HEADS, WINDOW, 2 * WINDOW), (0, 0, 0, 0)),
        pl.BlockSpec(memory_space=pltpu.SMEM),
    ]
    args = [x, vecs, w_in, wri, w_lo, w_ao, w_out, tab, sink]
    kern, in_specs, args, aliases = _with_state_aliases(kern, len(args), in_specs, args, prev_states, 1)
    return pl.pallas_call(
        kern,
        grid=(bsz, seq // tc),
        in_specs=in_specs,
        out_specs=(tile,
                   pl.BlockSpec((None, bsz, D_RNN), lambda b, t: (layer, 0, 0)),
                   pl.BlockSpec((None, None, CONV_LRU - 1, D_RNN), lambda b, t: (layer, b, 0, 0)),
                   kv_state, kv_state),
        out_shape=(jax.ShapeDtypeStruct((bsz, seq, D_MODEL), F32),
                   jax.ShapeDtypeStruct((DEPTH, bsz, D_RNN), F32),
                   jax.ShapeDtypeStruct((DEPTH, bsz, CONV_LRU - 1, D_RNN), F32),
                   jax.ShapeDtypeStruct((DEPTH, bsz, SUBLANES, SEG_LEN, N_KV, HEAD_DIM), F32),
                   jax.ShapeDtypeStruct((DEPTH, bsz, SUBLANES, SEG_LEN, N_KV, HEAD_DIM), F32)),
        input_output_aliases=aliases,
        scratch_shapes=[
            pltpu.VMEM((tc if permute_in else SUBLANES, D_MODEL), F32),
            pltpu.VMEM((tc, D_MODEL), BF16),
            pltpu.VMEM((tc, D_RNN), F32),
            pltpu.VMEM(((CONV_LRU - 1) * SUBLANES, D_RNN), F32),
            pltpu.VMEM((tc, Q_COLS), BF16),
            pltpu.VMEM((WINDOW + tc, KV_COLS), BF16),
            pltpu.VMEM((WINDOW + tc, KV_COLS), BF16),
            pltpu.VMEM((2, WINDOW, KV_COLS), F32),
            pltpu.VMEM((tc, 2 * D_MODEL), F32),
            pltpu.VMEM((tc, D_RNN), BF16),
            pltpu.VMEM((tc, Q_COLS), BF16),
            pltpu.VMEM((SUBLANES, D_RNN), F32),
            pltpu.VMEM((bsz, D_RNN), F32),
        ],
        compiler_params=pltpu.CompilerParams(
            dimension_semantics=("arbitrary", "arbitrary"), vmem_limit_bytes=VMEM_LIMIT_BYTES),
        name=f"prompt_mixer_l{layer}",
    )(*args)


def _pffn_kernel(x_ref, vec_ref, fcv_ref, w_up_ref, w_down_ref, y_ref, fstate_ref,
                 h_s, tail_s, acc_s, *, tc, permute_out):
    t = pl.program_id(1)
    nblk = tc // WINDOW
    ndelay = CONV_FF - 1

    @pl.when(t == 0)
    def _():
        tail_s[...] = jnp.zeros(tail_s.shape, F32)

    h_s[...] = _rmsnorm(x_ref[...], vec_ref[V_NORM_FFN_PRE:V_NORM_FFN_PRE + 1, :]).astype(BF16)
    hb = h_s[...]
    sub = lax.broadcasted_iota(jnp.int32, (SUBLANES, FF_CHUNK), 0)
    n_chunks = D_FF // FF_CHUNK

    def chunk_cols(c, part):
        return slice(part * D_FF + c * FF_CHUNK, part * D_FF + (c + 1) * FF_CHUNK)

    def up_project(c):
        return [_dot(hb, w_up_ref[:, chunk_cols(c, part)]) for part in range(2)]

    u_next = up_project(0)
    for c in range(n_chunks):
        u_cur = u_next
        if c + 1 < n_chunks:
            u_next = up_project(c + 1)
        conv = []
        for part in range(2):
            cols = chunk_cols(c, part)
            u = u_cur[part]
            bias = jnp.broadcast_to(fcv_ref[CONV_FF:CONV_FF + 1, cols], (SUBLANES, FF_CHUNK))
            taps = [jnp.broadcast_to(fcv_ref[j:j + 1, cols], (SUBLANES, FF_CHUNK)) for j in range(CONV_FF)]
            ys = []
            prev_tail = [tail_s[i * SUBLANES:(i + 1) * SUBLANES, cols] for i in range(ndelay)]
            for j in range(nblk):
                groups = _row_groups(u, j)
                delayed = _delayed_groups(groups, prev_tail, ndelay, sub)
                for k in range(SEG_LEN):
                    y = bias + taps[CONV_FF - 1] * groups[k]
                    for d in range(1, CONV_FF):
                        y = y + taps[CONV_FF - 1 - d] * delayed[d - 1][k]
                    ys.append(y)
                prev_tail = groups[SEG_LEN - ndelay:]
            conv.append(jnp.concatenate(ys, axis=0))
            tail_s[:, cols] = u[tc - ndelay * SUBLANES:tc]
            for i in range(ndelay):
                r = tc - (ndelay - i) * SUBLANES + SUBLANES - 1
                fstate_ref[i:i + 1, cols] = u[r:r + 1]
        act = (_gelu_tanh_doubled(conv[1]) * conv[0]).astype(BF16)
        part_f = _dot(act, w_down_ref[c * FF_CHUNK:(c + 1) * FF_CHUNK, 0:D_MODEL])
        if c == 0:
            acc_s[...] = part_f
        else:
            acc_s[...] += part_f
    y = x_ref[...] + _rmsnorm(acc_s[...], vec_ref[V_NORM_FFN_POST:V_NORM_FFN_POST + 1, :])
    if permute_out:
        for j in range(nblk):
            for k in range(SEG_LEN):
                y_ref[j * SUBLANES:(j + 1) * SUBLANES, k, :] = (
                    y[j * WINDOW + k * SUBLANES:j * WINDOW + (k + 1) * SUBLANES])
    else:
        y_ref[...] = y


def _prompt_ffn(x, vecs, fcv, w_up, w_down, layer, prev_states, permute_out):
    bsz, seq, _ = x.shape
    tc = PROMPT_TILE
    kern = functools.partial(_pffn_kernel, tc=tc, permute_out=permute_out)
    tile = pl.BlockSpec((None, tc, D_MODEL), lambda b, t: (b, t, 0))
    if permute_out:
        y_shape = (bsz, seq // SEG_LEN, SEG_LEN, D_MODEL)
        y_spec = pl.BlockSpec((None, tc // SEG_LEN, SEG_LEN, D_MODEL), lambda b, t: (b, t, 0, 0))
    else:
        y_shape, y_spec = (bsz, seq, D_MODEL), tile
    in_specs = [
        tile,
        _const_spec((None, VEC_ROWS, D_MODEL), (layer, 0, 0)),
        _const_spec((None, SUBLANES, 2 * D_FF), (layer, 0, 0)),
        _const_spec((None, D_MODEL, 2 * D_FF + WEIGHT_PAD), (layer, 0, 0)),
        _const_spec((None, D_FF, D_MODEL + WEIGHT_PAD), (layer, 0, 0)),
    ]
    args = [x, vecs, fcv, w_up, w_down]
    kern, in_specs, args, aliases = _with_state_aliases(kern, len(args), in_specs, args, prev_states, 1)
    return pl.pallas_call(
        kern,
        grid=(bsz, seq // tc),
        in_specs=in_specs,
        out_specs=(y_spec, pl.BlockSpec((None, None, CONV_FF - 1, 2 * D_FF), lambda b, t: (layer, b, 0, 0))),
        out_shape=(jax.ShapeDtypeStruct(y_shape, F32),
                   jax.ShapeDtypeStruct((DEPTH, bsz, CONV_FF - 1, 2 * D_FF), F32)),
        input_output_aliases=aliases,
        scratch_shapes=[
            pltpu.VMEM((tc, D_MODEL), BF16),
            pltpu.VMEM(((CONV_FF - 1) * SUBLANES, 2 * D_FF), F32),
            pltpu.VMEM((tc, D_MODEL), F32),
        ],
        compiler_params=pltpu.CompilerParams(
            dimension_semantics=("arbitrary", "arbitrary"), vmem_limit_bytes=VMEM_LIMIT_BYTES),
        name=f"prompt_ffn_l{layer}",
    )(*args)


def _smix_in_kernel(xs_ref, h0_ref, cprev_ref, vec_ref, w_in_ref, wri_ref,
                    q_ref, k_ref, v_ref, lru_ref, gate_ref, hlast_ref, cstate_ref,
                    h_s, xr_s, *, nb, nt):
    for t in range(nt):
        h_s[t * nb:(t + 1) * nb, :] = _rmsnorm(
            xs_ref[:, t, :], vec_ref[V_NORM_MIX_PRE:V_NORM_MIX_PRE + 1, :]).astype(BF16)
    hb = h_s[...]
    xr_s[...] = _dot(hb, w_in_ref[:, 0:C_Q])
    qf = _dot(hb, w_in_ref[:, C_Q:C_K]) * (HEAD_DIM ** -0.5)
    kf = _dot(hb, w_in_ref[:, C_K:C_V])
    vf = _dot(hb, w_in_ref[:, C_V:C_G])
    for t in range(nt):
        rows = slice(t * nb, (t + 1) * nb)
        q_ref[:, t * Q_COLS:(t + 1) * Q_COLS] = qf[rows].astype(BF16)
        for g in range(N_KV):
            k_ref[:, t * N_KV + g, :] = kf[rows, g * HEAD_DIM:(g + 1) * HEAD_DIM]
            v_ref[:, t * N_KV + g, :] = vf[rows, g * HEAD_DIM:(g + 1) * HEAD_DIM]
    gate_ref[...] = _sigmoid(_dot(hb, w_in_ref[:, C_G:IN_COLS]))

    npre = CONV_LRU - 1
    for t in range(nt - npre, nt):
        cstate_ref[:, t - (nt - npre), :] = xr_s[t * nb:(t + 1) * nb, :]

    for n in range(N_LRU_BLOCKS):
        cb = slice(n * LRU_BLOCK, (n + 1) * LRU_BLOCK)
        xx = [cprev_ref[:, j, cb] for j in range(npre)]
        xx += [xr_s[t * nb:(t + 1) * nb, cb] for t in range(nt)]
        xcs = []
        for t in range(nt):
            xc = vec_ref[V_CONV_B:V_CONV_B + 1, cb]
            for j in range(CONV_LRU):
                xc = xc + vec_ref[V_CONV_W + j:V_CONV_W + j + 1, cb] * xx[t + j]
            xcs.append(xc)
        a, ix, mult = _lru_coeffs(jnp.concatenate(xcs, axis=0), vec_ref, wri_ref, n)
        b = mult * ix
        h = h0_ref[:, cb]
        for t in range(nt):
            rows = slice(t * nb, (t + 1) * nb)
            h = a[rows] * h + b[rows]
            lru_ref[rows, cb] = h.astype(BF16)
        hlast_ref[:, cb] = h


def _sample_mixer_in(xs, h0, cprev, vecs, w_in, wri, layer, prev_states):
    nb, nt, _ = xs.shape
    assert PAST_LEN > 0
    kern = functools.partial(_smix_in_kernel, nb=nb, nt=nt)
    whole = lambda shape: pl.BlockSpec(shape, lambda i: (0,) * len(shape))
    conv_state = pl.BlockSpec((None, nb, CONV_LRU - 1, D_RNN), lambda i: (layer, 0, 0, 0))
    in_specs = [
        whole((nb, nt, D_MODEL)),
        pl.BlockSpec((None, nb, D_RNN), lambda i: (layer, 0, 0)),
        conv_state,
        _const_spec((None, VEC_ROWS, D_MODEL), (layer, 0, 0)),
        _const_spec((None, D_MODEL, IN_COLS), (layer, 0, 0)),
        _const_spec((None, N_LRU_BLOCKS, LRU_BLOCK, 2 * LRU_BLOCK), (layer, 0, 0, 0)),
    ]
    args = [xs, h0, cprev, vecs, w_in, wri]
    kern, in_specs, args, aliases = _with_state_aliases(kern, len(args), in_specs, args, prev_states, 5)
    return pl.pallas_call(
        kern,
        grid=(1,),
        in_specs=in_specs,
        out_specs=(whole((nb, nt * Q_COLS)), whole((nb, nt * N_KV, HEAD_DIM)), whole((nb, nt * N_KV, HEAD_DIM)),
                   whole((nt * nb, D_RNN)), whole((nt * nb, 2 * D_MODEL)),
                   pl.BlockSpec((None, nb, D_RNN), lambda i: (layer, 0, 0)), conv_state),
        out_shape=(jax.ShapeDtypeStruct((nb, nt * Q_COLS), BF16),
                   jax.ShapeDtypeStruct((nb, nt * N_KV, HEAD_DIM), F32),
                   jax.ShapeDtypeStruct((nb, nt * N_KV, HEAD_DIM), F32),
                   jax.ShapeDtypeStruct((nt * nb, D_RNN), BF16),
                   jax.ShapeDtypeStruct((nt * nb, 2 * D_MODEL), F32),
                   jax.ShapeDtypeStruct((DEPTH, nb, D_RNN), F32),
                   jax.ShapeDtypeStruct((DEPTH, nb, CONV_LRU - 1, D_RNN), F32)),
        input_output_aliases=aliases,
        scratch_shapes=[pltpu.VMEM((nt * nb, D_MODEL), BF16), pltpu.VMEM((nt * nb, D_RNN), F32)],
        compiler_params=pltpu.CompilerParams(
            dimension_semantics=("arbitrary",), vmem_limit_bytes=VMEM_LIMIT_BYTES),
        name=f"sample_mixer_in_l{layer}",
    )(*args)


def _sattn_kernel(q_ref, kn_ref, vn_ref, ck_ref, cv_ref, tab_ref, attn_ref, sk_ref, sv_ref,
                  kc_s, vc_s, *, bt, nt):
    n_cache = WINDOW * N_KV
    n_new = nt * N_KV
    for u in range(SAMPLE_ATTN_UNROLL):
        kc_s[u, n_cache + n_new:SAMPLE_KEYS, :] = jnp.zeros((SAMPLE_KEYS - n_cache - n_new, HEAD_DIM), F32)
        vc_s[u, n_cache + n_new:SAMPLE_KEYS, :] = jnp.zeros((SAMPLE_KEYS - n_cache - n_new, HEAD_DIM), F32)

    def body(i, _):
        seqs = [i * SAMPLE_ATTN_UNROLL + u for u in range(SAMPLE_ATTN_UNROLL)]
        scores = []
        for u, b in enumerate(seqs):
            for c_ref, n_ref, s_ref, scr in ((ck_ref, kn_ref, sk_ref, kc_s.at[u]), (cv_ref, vn_ref, sv_ref, vc_s.at[u])):
                scr[0:n_cache, :] = c_ref[b]
                scr[n_cache:n_cache + n_new, :] = n_ref[b]
                s_ref[b, 0:n_cache - n_new, :] = c_ref[b, n_new:n_cache, :]
                s_ref[b, n_cache - n_new:n_cache, :] = n_ref[b]
            scores.append(_dot_nt(q_ref[b], kc_s[u].astype(BF16)))
        probs = []
        for s in scores:
            s = s + tab_ref[...]
            p = jnp.exp(s - jnp.max(s, axis=-1, keepdims=True))
            probs.append((p.astype(BF16), 1.0 / jnp.sum(p, axis=-1, keepdims=True)))
        for u, b in enumerate(seqs):
            p, inv = probs[u]
            attn_ref[b] = (_dot(p, vc_s[u].astype(BF16)) * inv).astype(BF16)
        return 0

    lax.fori_loop(0, bt // SAMPLE_ATTN_UNROLL, body, 0)


def _sample_attention(q, kn, vn, cache_k, cache_v, stab, layer, prev_states):
    nb, rows, _ = q.shape
    nt = rows // N_HEADS
    bt = SAMPLE_BATCH_TILE
    kern = functools.partial(_sattn_kernel, bt=bt, nt=nt)
    cache_spec = pl.BlockSpec((None, bt, WINDOW * N_KV, HEAD_DIM), lambda i: (layer, i, 0, 0))
    new_spec = pl.BlockSpec((bt, nt * N_KV, HEAD_DIM), lambda i: (i, 0, 0))
    q_spec = pl.BlockSpec((bt, rows, HEAD_DIM), lambda i: (i, 0, 0))
    in_specs = [q_spec, new_spec, new_spec, cache_spec, cache_spec,
                _const_spec((None, rows, SAMPLE_KEYS), (layer, 0, 0))]
    args = [q, kn, vn, cache_k, cache_v, stab]
    kern, in_specs, args, aliases = _with_state_aliases(kern, len(args), in_specs, args, prev_states, 1)
    return pl.pallas_call(
        kern,
        grid=(nb // bt,),
        in_specs=in_specs,
        out_specs=(q_spec, cache_spec, cache_spec),
        out_shape=(jax.ShapeDtypeStruct((nb, rows, HEAD_DIM), BF16),
                   jax.ShapeDtypeStruct((DEPTH, nb, WINDOW * N_KV, HEAD_DIM), F32),
                   jax.ShapeDtypeStruct((DEPTH, nb, WINDOW * N_KV, HEAD_DIM), F32)),
        input_output_aliases=aliases,
        scratch_shapes=[pltpu.VMEM((SAMPLE_ATTN_UNROLL, SAMPLE_KEYS, HEAD_DIM), F32),
                        pltpu.VMEM((SAMPLE_ATTN_UNROLL, SAMPLE_KEYS, HEAD_DIM), F32)],
        compiler_params=pltpu.CompilerParams(
            dimension_semantics=("arbitrary",), vmem_limit_bytes=VMEM_LIMIT_BYTES),
        name=f"sample_attention_l{layer}",
    )(*args)


def _smix_out_kernel(xs_ref, lru_ref, attn_ref, gate_ref, vec_ref, w_lo_ref, w_ao_ref, w_out_ref,
                     xmid_ref, a_s, *, nb, nt):
    for t in range(nt):
        a_s[t * nb:(t + 1) * nb, :] = attn_ref[:, t * Q_COLS:(t + 1) * Q_COLS]
    merged = (gate_ref[:, 0:D_MODEL] * _dot(lru_ref[...], w_lo_ref[...])
              + gate_ref[:, D_MODEL:2 * D_MODEL] * _dot(a_s[...], w_ao_ref[...]))
    m = _dot(merged.astype(BF16), w_out_ref[...])
    mn = _rmsnorm(m, vec_ref[V_NORM_MIX_POST:V_NORM_MIX_POST + 1, :])
    for t in range(nt):
        rows = slice(t * nb, (t + 1) * nb)
        xmid_ref[rows, :] = xs_ref[:, t, :] + mn[rows]


def _sample_mixer_out(xs, lru, attn, gates, vecs, w_lo, w_ao, w_out, layer):
    nb, nt, _ = xs.shape
    kern = functools.partial(_smix_out_kernel, nb=nb, nt=nt)
    whole = lambda shape: pl.BlockSpec(shape, lambda i: (0,) * len(shape))
    return pl.pallas_call(
        kern,
        grid=(1,),
        in_specs=[
            whole(xs.shape), whole(lru.shape), whole(attn.shape), whole(gates.shape),
            _const_spec((None, VEC_ROWS, D_MODEL), (layer, 0, 0)),
            _const_spec((None, D_RNN, D_MODEL), (layer, 0, 0)),
            _const_spec((None, Q_COLS, D_MODEL), (layer, 0, 0)),
            _const_spec((None, D_MODEL, D_MODEL), (layer, 0, 0)),
        ],
        out_specs=whole((nt * nb, D_MODEL)),
        out_shape=jax.ShapeDtypeStruct((nt * nb, D_MODEL), F32),
        scratch_shapes=[pltpu.VMEM((nt * nb, Q_COLS), BF16)],
        compiler_params=pltpu.CompilerParams(
            dimension_semantics=("arbitrary",), vmem_limit_bytes=VMEM_LIMIT_BYTES),
        name=f"sample_mixer_out_l{layer}",
    )(xs, lru, attn, gates, vecs, w_lo, w_ao, w_out)


def _sffn_kernel(x_ref, vec_ref, fcv_ref, prev_ref, w_ref, wd_ref, ys_ref, fstate_ref,
                 h_s, val_s, acc_s, *, nb, nt, nck):
    c = pl.program_id(0)

    @pl.when(c == 0)
    def _():
        h_s[...] = _rmsnorm(x_ref[...], vec_ref[V_NORM_FFN_PRE:V_NORM_FFN_PRE + 1, :]).astype(BF16)
        acc_s[...] = jnp.zeros(acc_s.shape, F32)

    u = _dot(h_s[...], w_ref[...])
    uu = [prev_ref[:, j, :] for j in range(CONV_FF - 1)] + [u[t * nb:(t + 1) * nb] for t in range(nt)]
    ys = []
    for t in range(nt):
        y = fcv_ref[CONV_FF:CONV_FF + 1, :]
        for j in range(CONV_FF):
            y = y + fcv_ref[j:j + 1, :] * uu[t + j]
        ys.append(y)
    conv = jnp.concatenate(ys, axis=0)
    for j in range(CONV_FF - 1):
        fstate_ref[:, j, :] = uu[nt + j]

    @pl.when(c < nck)
    def _():
        val_s[c] = conv

    @pl.when(c >= nck)
    def _():
        act = (_gelu_tanh_doubled(conv) * val_s[c - nck]).astype(BF16)
        acc_s[...] += _dot(act, wd_ref[...])

    @pl.when(c == 2 * nck - 1)
    def _():
        y = x_ref[...] + _rmsnorm(acc_s[...], vec_ref[V_NORM_FFN_POST:V_NORM_FFN_POST + 1, :])
        for t in range(nt):
            ys_ref[:, t, :] = y[t * nb:(t + 1) * nb]


def _sample_ffn(xmid, fprev, vecs, fcv, w_up, w_down, layer, nb, prev_states):
    nt = xmid.shape[0] // nb
    ck = FF_CHUNK
    nck = D_FF // ck
    kern = functools.partial(_sffn_kernel, nb=nb, nt=nt, nck=nck)
    state_spec = pl.BlockSpec((None, nb, CONV_FF - 1, ck), lambda c: (layer, 0, 0, c))
    in_specs = [
        _const_spec((nt * nb, D_MODEL), (0, 0)),
        _const_spec((None, VEC_ROWS, D_MODEL), (layer, 0, 0)),
        pl.BlockSpec((None, SUBLANES, ck), lambda c: (layer, 0, c)),
        state_spec,
        pl.BlockSpec((None, D_MODEL, ck), lambda c: (layer, 0, c)),
        pl.BlockSpec((None, ck, D_MODEL), lambda c: (layer, jnp.maximum(c - nck, 0), 0)),
    ]
    args = [xmid, vecs, fcv, fprev, w_up, w_down]
    kern, in_specs, args, aliases = _with_state_aliases(kern, len(args), in_specs, args, prev_states, 1)
    return pl.pallas_call(
        kern,
        grid=(2 * nck,),
        in_specs=in_specs,
        out_specs=(pl.BlockSpec((nb, nt, D_MODEL), lambda c: (0, 0, 0)), state_spec),
        out_shape=(jax.ShapeDtypeStruct((nb, nt, D_MODEL), F32),
                   jax.ShapeDtypeStruct((DEPTH, nb, CONV_FF - 1, 2 * D_FF), F32)),
        input_output_aliases=aliases,
        scratch_shapes=[pltpu.VMEM((nt * nb, D_MODEL), BF16),
                        pltpu.VMEM((nck, nt * nb, ck), F32),
                        pltpu.VMEM((nt * nb, D_MODEL), F32)],
        compiler_params=pltpu.CompilerParams(
            dimension_semantics=("arbitrary",), vmem_limit_bytes=VMEM_LIMIT_BYTES),
        name=f"sample_ffn_l{layer}",
    )(*args)


def kernel(x_prompt, x_sample, state_lru_h, state_lru_conv, cache_win_k, cache_win_v, state_ffn_conv,
           norm_mix_pre, norm_mix_post, norm_ffn_pre, norm_ffn_post, w_in, conv_lru_w, conv_lru_b,
           lru_wr, lru_br, lru_wi, lru_bi, lru_lambda, w_lru_o, w_attn_o, w_out, attn_sink, rel_bias,
           w_up, ffn_conv_w, ffn_conv_b, w_down):
    nb, nt, _ = x_sample.shape
    bp = x_prompt.shape[0]

    row = lambda v: v[:, None, :]
    vecs = jnp.concatenate(
        [row(norm_mix_pre), row(norm_mix_post), row(conv_lru_b), row(lru_br), row(lru_bi), row(lru_lambda),
         conv_lru_w, row(norm_ffn_pre), row(norm_ffn_post),
         jnp.zeros((DEPTH, VEC_ROWS - 12, D_MODEL), F32)], axis=1)
    fcv = jnp.concatenate(
        [ffn_conv_w, row(ffn_conv_b), jnp.zeros((DEPTH, SUBLANES - CONV_FF - 1, 2 * D_FF), F32)], axis=1)
    fcv = fcv * jnp.where(jnp.arange(2 * D_FF) < D_FF, 0.5, 1.0).astype(F32)
    w_in_b = w_in.astype(BF16)
    wri_b = jnp.concatenate([lru_wr, lru_wi], axis=-1).astype(BF16)
    w_lo_b = w_lru_o.astype(BF16)
    w_ao_b = w_attn_o.astype(BF16)
    w_out_b = w_out.astype(BF16)
    pad_cols = lambda w: jnp.pad(w.astype(BF16), ((0, 0), (0, 0), (0, WEIGHT_PAD)))
    w_up_b = pad_cols(w_up)
    w_down_b = pad_cols(w_down)

    ptab, stab = _bias_tables(rel_bias, attn_sink)

    yp, xs = x_prompt, x_sample
    p_mix = p_ffn = s_mix = s_att = s_ffn = None
    cache_rows = lambda c: c.reshape(DEPTH, nb, WINDOW * N_KV, HEAD_DIM)
    for l in range(DEPTH):
        yp, *p_mix = _prompt_mixer(yp, vecs, w_in_b, wri_b, w_lo_b, w_ao_b, w_out_b, ptab, attn_sink, l, p_mix,
                                   permute_in=(l == 0))
        yp, *p_ffn = _prompt_ffn(yp, vecs, fcv, w_up_b, w_down_b, l, p_ffn, permute_out=(l == DEPTH - 1))

        q, kn, vn, lru, gates, *s_mix = _sample_mixer_in(
            xs, state_lru_h, state_lru_conv, vecs, w_in_b, wri_b, l, s_mix)
        attn, *s_att = _sample_attention(
            q.reshape(nb, nt * N_HEADS, HEAD_DIM), kn, vn, cache_rows(cache_win_k), cache_rows(cache_win_v),
            stab, l, s_att)
        xmid = _sample_mixer_out(xs, lru, attn.reshape(nb, nt * Q_COLS), gates, vecs, w_lo_b, w_ao_b, w_out_b, l)
        xs, *s_ffn = _sample_ffn(xmid, state_ffn_conv, vecs, fcv, w_up_b, w_down_b, l, nb, s_ffn)

    p_h, p_c, p_k, p_v = p_mix
    s_h, s_c = s_mix
    s_k, s_v = s_att
    kv_shape = (DEPTH, bp, WINDOW, N_KV, HEAD_DIM)
    return (yp.reshape(x_prompt.shape), xs, p_h, p_c, p_k.reshape(kv_shape), p_v.reshape(kv_shape), p_ffn[0],
            s_h, s_c, s_k.reshape(cache_win_k.shape), s_v.reshape(cache_win_v.shape), s_ffn[0])
```

```python
import functools
import math

import numpy as np
import jax
import jax.numpy as jnp
from jax import lax
from jax.experimental import pallas as pl
from jax.experimental.pallas import tpu as pltpu

D_MODEL = 1024
DEPTH = 2
PAST_LEN = 16384
D_RNN = D_MODEL
N_LRU_BLOCKS = 8
LRU_BLOCK = D_RNN // N_LRU_BLOCKS
CONV_LRU = 4
LRU_C = 8.0
N_HEADS = 8
N_KV = 2
GROUP = N_HEADS // N_KV
HEAD_DIM = D_MODEL // N_HEADS
WINDOW = 128
N_BUCKETS = 32
MAX_EXACT = N_BUCKETS // 2
MAX_DISTANCE = 128
D_FF = 4 * D_MODEL
CONV_FF = 3
EPS = 1e-6
Q_COLS = N_HEADS * HEAD_DIM
KV_COLS = N_KV * HEAD_DIM
IN_COLS = D_RNN + Q_COLS + 2 * KV_COLS + 2 * D_MODEL
C_Q = D_RNN
C_K = C_Q + Q_COLS
C_V = C_K + KV_COLS
C_G = C_V + KV_COLS

F32 = jnp.float32
BF16 = jnp.bfloat16

SUBLANES = 8
LANES = 128
VMEM_LIMIT_BYTES = 56 * 1024 * 1024

V_NORM_MIX_PRE, V_NORM_MIX_POST, V_CONV_B, V_BR, V_BI, V_LAMBDA, V_CONV_W = 0, 1, 2, 3, 4, 5, 6
V_NORM_FFN_PRE, V_NORM_FFN_POST = 10, 11
VEC_ROWS = 16

PROMPT_TILE = 512
PROJ_PIECE = 2 * KV_COLS
assert (C_K - C_Q) % PROJ_PIECE == 0 and (IN_COLS - C_G) % PROJ_PIECE == 0
PROMPT_FF_CHUNK = 1024
FF_CHUNK = 1024
SAMPLE_KEYS = N_KV * (WINDOW + 4) + SUBLANES
SINK_COL = N_KV * (WINDOW + 4)
SAMPLE_BATCH_TILE = 16
SAMPLE_ATTN_UNROLL = 8
SEG_LEN = WINDOW // SUBLANES


def _bucket_thresholds():
    d = np.arange(0, 2 * WINDOW)
    nf = np.maximum(d, 1).astype(np.float64)
    large = MAX_EXACT + (np.log(nf / MAX_EXACT) / math.log(MAX_DISTANCE / MAX_EXACT)
                         * (N_BUCKETS - MAX_EXACT)).astype(np.int64)
    bucket = np.where(d < MAX_EXACT, d, np.minimum(large, N_BUCKETS - 1))
    return tuple(int(d[bucket >= b].min()) for b in range(1, N_BUCKETS))


_BUCKET_THRESHOLDS = _bucket_thresholds()


def _dot(a, b):
    return jnp.dot(a, b, preferred_element_type=F32)


def _dot_nt(a, b):
    return lax.dot_general(a, b, (((1,), (1,)), ((), ())), preferred_element_type=F32)


def _rmsnorm(x, g):
    return x * lax.rsqrt(jnp.mean(x * x, axis=-1, keepdims=True) + EPS) * g


def _sigmoid(x):
    return 1.0 / (1.0 + jnp.exp(-x))


def _gelu_tanh_doubled(x):
    c = math.sqrt(2.0 / math.pi)
    t = jnp.tanh(x * (c + (c * 0.044715) * (x * x)))
    return x + x * t


def _const_spec(block_shape, index):
    return pl.BlockSpec(block_shape, lambda *_: index, pipeline_mode=pl.Buffered(1))


def _with_state_aliases(kern, n_in, in_specs, args, prev_states, first_state_out):
    if prev_states is None:
        return kern, list(in_specs), list(args), {}
    n = len(prev_states)

    def body(*refs):
        return kern(*refs[:n_in], *refs[n_in + n:])

    return (body, list(in_specs) + [pl.BlockSpec(memory_space=pl.ANY)] * n, list(args) + list(prev_states),
            {n_in + i: first_state_out + i for i in range(n)})


def _bucket_of(d):
    n = jnp.maximum(d, 0)
    bucket = jnp.zeros(d.shape, jnp.int32)
    for thr in _BUCKET_THRESHOLDS:
        bucket = bucket + jnp.where(n >= thr, 1, 0)
    return bucket


def _block_time(p):
    return lax.bitwise_and(p, SUBLANES - 1) * SEG_LEN + lax.shift_right_logical(p, 3)


def _table_kernel(rel_ref, sink_ref, pt_ref, st_ref):
    qi = lax.broadcasted_iota(jnp.int32, (WINDOW, 2 * WINDOW), 0)
    kj = lax.broadcasted_iota(jnp.int32, (WINDOW, 2 * WINDOW), 1)
    kpos = lax.bitwise_and(kj, WINDOW - 1)
    d = _block_time(qi) + WINDOW - (_block_time(kpos) + (kj - kpos))
    bucket = _bucket_of(d)
    in_band = jnp.where(d >= 0, jnp.where(d < WINDOW, 1, 0), 0)
    cur_only = jnp.where(kj >= WINDOW, in_band, 0)
    for h in range(N_HEADS):
        val = jnp.zeros(d.shape, F32)
        for b in range(N_BUCKETS):
            val = jnp.where(bucket == b, rel_ref[b, h], val)
        pt_ref[1, h] = jnp.where(in_band == 1, val, -jnp.inf)
        pt_ref[0, h] = jnp.where(cur_only == 1, val, -jnp.inf)

    r = lax.broadcasted_iota(jnp.int32, (4 * N_HEADS, SAMPLE_KEYS), 0)
    j = lax.broadcasted_iota(jnp.int32, (4 * N_HEADS, SAMPLE_KEYS), 1)
    t = lax.shift_right_logical(r, 3)
    hh = lax.bitwise_and(r, N_HEADS - 1)
    d = t + WINDOW - lax.shift_right_logical(j, 1)
    bucket = _bucket_of(d)
    in_band = jnp.where(d >= 0, jnp.where(d < WINDOW, 1, 0), 0)
    in_band = jnp.where(lax.bitwise_and(j, N_KV - 1) == lax.shift_right_logical(hh, 2), in_band, 0)
    val = jnp.zeros(d.shape, F32)
    for h in range(N_HEADS):
        hval = jnp.zeros(d.shape, F32)
        for b in range(N_BUCKETS):
            hval = jnp.where(bucket == b, rel_ref[b, h], hval)
        val = jnp.where(hh == h, hval, val)
    val = jnp.where(in_band == 1, val, -jnp.inf)
    for l in range(DEPTH):
        sk = jnp.zeros(d.shape, F32)
        for h in range(N_HEADS):
            sk = jnp.where(hh == h, sink_ref[l, h], sk)
        st_ref[l] = jnp.where(j == SINK_COL, sk, val)


def _bias_tables(rel_bias, attn_sink):
    smem = pl.BlockSpec(memory_space=pltpu.SMEM)
    return pl.pallas_call(
        _table_kernel,
        out_shape=(jax.ShapeDtypeStruct((2, N_HEADS, WINDOW, 2 * WINDOW), F32),
                   jax.ShapeDtypeStruct((DEPTH, 4 * N_HEADS, SAMPLE_KEYS), F32)),
        in_specs=[smem, smem],
        name="bias_tables",
    )(rel_bias, attn_sink)


def _lru_coeffs(xc, vec_ref, wri_ref, n):
    cb = slice(n * LRU_BLOCK, (n + 1) * LRU_BLOCK)
    rw = _dot(xc.astype(BF16), wri_ref[n])
    r = _sigmoid(rw[:, :LRU_BLOCK] + vec_ref[V_BR:V_BR + 1, cb])
    i = _sigmoid(rw[:, LRU_BLOCK:] + vec_ref[V_BI:V_BI + 1, cb])
    z = -vec_ref[V_LAMBDA:V_LAMBDA + 1, cb]
    softplus = jnp.maximum(z, 0.0) + jnp.log1p(jnp.exp(-jnp.abs(z)))
    log_a = (-LRU_C * softplus) * r
    a = jnp.exp(log_a)
    mult = jnp.sqrt(jnp.maximum(1.0 - a * a, 0.0))
    return a, i * xc, mult


def _delayed_groups(groups, prev_tail, ndelay, sub):
    ng = len(groups)
    wrapped = {}
    for i in range(ndelay):
        k = ng - ndelay + i
        wrapped[k] = jnp.where(sub == 0, pltpu.roll(prev_tail[i], 1, 0), pltpu.roll(groups[k], 1, 0))
    return [[groups[k - d] if k >= d else wrapped[ng + k - d] for k in range(ng)] for d in range(1, ndelay + 1)]


def _row_groups(x, block):
    return [x[block * WINDOW + k * SUBLANES:block * WINDOW + (k + 1) * SUBLANES] for k in range(SEG_LEN)]


def _pmix_kernel(x_ref, vec_ref, w_in_ref, wri_ref, w_lo_ref, w_ao_ref, w_out_ref, tab_ref, sink_ref,
                 y_ref, hlast_ref, cstate_ref, kstate_ref, vstate_ref,
                 xp_s, h_s, xr_s, xrc_s, q_s, k_s, v_s, kvf_s, g_s, lru_s, attn_s, hc_s, hl_s, *, layer, tc, permute_in):
    t = pl.program_id(1)
    nblk = tc // WINDOW
    ndelay = CONV_LRU - 1
    n_pieces = (IN_COLS - C_Q) // PROJ_PIECE
    batch_row = lax.broadcasted_iota(jnp.int32, (hl_s.shape[0], LRU_BLOCK), 0)

    @pl.when(jnp.logical_and(t == 0, pl.program_id(0) == 0))
    def _():
        hl_s[...] = jnp.zeros(hl_s.shape, F32)

    @pl.when(t == 0)
    def _():
        xrc_s[...] = jnp.zeros(xrc_s.shape, F32)
        k_s[0:WINDOW, :] = jnp.zeros((WINDOW, KV_COLS), BF16)
        v_s[0:WINDOW, :] = jnp.zeros((WINDOW, KV_COLS), BF16)
        hc_s[...] = jnp.zeros(hc_s.shape, F32)

    if permute_in:
        for j in range(nblk):
            for k in range(SEG_LEN):
                xp_s[j * WINDOW + k * SUBLANES:j * WINDOW + (k + 1) * SUBLANES, :] = (
                    x_ref[j * SUBLANES:(j + 1) * SUBLANES, k, :])
        x_tile = xp_s
    else:
        x_tile = x_ref

    h_s[...] = _rmsnorm(x_tile[...], vec_ref[V_NORM_MIX_PRE:V_NORM_MIX_PRE + 1, :]).astype(BF16)
    hb = h_s[...]
    xr_s[...] = _dot(hb, w_in_ref[:, 0:C_Q])

    def project_piece(i):
        c0 = C_Q + i * PROJ_PIECE
        z = _dot(hb, w_in_ref[:, c0:c0 + PROJ_PIECE])
        if c0 < C_K:
            q_s[:, c0 - C_Q:c0 - C_Q + PROJ_PIECE] = (z * (HEAD_DIM ** -0.5)).astype(BF16)
        elif c0 == C_K:
            k_s[WINDOW:WINDOW + tc, :] = z[:, 0:KV_COLS].astype(BF16)
            v_s[WINDOW:WINDOW + tc, :] = z[:, KV_COLS:2 * KV_COLS].astype(BF16)
            kvf_s[0] = z[tc - WINDOW:tc, 0:KV_COLS]
            kvf_s[1] = z[tc - WINDOW:tc, KV_COLS:2 * KV_COLS]
        else:
            g_s[:, c0 - C_G:c0 - C_G + PROJ_PIECE] = _sigmoid(z)

    for i in range(ndelay):
        r = tc - (ndelay - i) * SUBLANES + SUBLANES - 1
        cstate_ref[i:i + 1, :] = xr_s[r:r + 1, :]

    sub = lax.broadcasted_iota(jnp.int32, (SUBLANES, LRU_BLOCK), 0)
    seq_start = (sub + t) == 0
    for n in range(N_LRU_BLOCKS):
        cb = slice(n * LRU_BLOCK, (n + 1) * LRU_BLOCK)
        bias = jnp.broadcast_to(vec_ref[V_CONV_B:V_CONV_B + 1, cb], (SUBLANES, LRU_BLOCK))
        taps = [jnp.broadcast_to(vec_ref[V_CONV_W + j:V_CONV_W + j + 1, cb], (SUBLANES, LRU_BLOCK))
                for j in range(CONV_LRU)]
        xcs = []
        prev_tail = [xrc_s[i * SUBLANES:(i + 1) * SUBLANES, cb] for i in range(ndelay)]
        for j in range(nblk):
            groups = [xr_s[j * WINDOW + k * SUBLANES:j * WINDOW + (k + 1) * SUBLANES, cb] for k in range(SEG_LEN)]
            delayed = _delayed_groups(groups, prev_tail, ndelay, sub)
            for k in range(SEG_LEN):
                xc = bias + taps[CONV_LRU - 1] * groups[k]
                for d in range(1, CONV_LRU):
                    xc = xc + taps[CONV_LRU - 1 - d] * delayed[d - 1][k]
                xcs.append(xc)
            prev_tail = groups[SEG_LEN - ndelay:]
        a, ix, mult = _lru_coeffs(jnp.concatenate(xcs, axis=0), vec_ref, wri_ref, n)
        b = mult * ix

        carry = hc_s[0:1, cb]
        hs = []
        for j in range(nblk):
            ag, bg, ig = _row_groups(a, j), _row_groups(b, j), _row_groups(ix, j)
            if j == 0:
                bg[0] = jnp.where(seq_start, ig[0], bg[0])
            acc_a, acc_b = [ag[0]], [bg[0]]
            for k in range(1, SEG_LEN):
                acc_b.append(ag[k] * acc_b[-1] + bg[k])
                acc_a.append(ag[k] * acc_a[-1])
            seg_a, seg_b = acc_a[-1], acc_b[-1]
            for s in (1, 2, 4):
                ash = jnp.where(sub >= s, pltpu.roll(seg_a, s, 0), 1.0)
                bsh = jnp.where(sub >= s, pltpu.roll(seg_b, s, 0), 0.0)
                seg_b = seg_a * bsh + seg_b
                seg_a = seg_a * ash
            h_end = seg_a * carry + seg_b
            h_in = jnp.where(sub == 0, carry, pltpu.roll(h_end, 1, 0))
            hs += [acc_a[k] * h_in + acc_b[k] for k in range(SEG_LEN)]
            carry = h_end[SUBLANES - 1:SUBLANES, :]
        lru_s[:, cb] = jnp.concatenate(hs, axis=0).astype(BF16)
        hc_s[0:1, cb] = carry
        hl_s[:, cb] = jnp.where(batch_row == pl.program_id(0), carry, hl_s[:, cb])
        if n < n_pieces:
            project_piece(n)
    for i in range(N_LRU_BLOCKS, n_pieces):
        project_piece(i)
    hlast_ref[...] = hl_s[...]

    first = jnp.where(t == 0, 0, 1)

    def scores(j, g):
        q4 = jnp.concatenate(
            [q_s[j * WINDOW:(j + 1) * WINDOW, (g * GROUP + hg) * HEAD_DIM:(g * GROUP + hg + 1) * HEAD_DIM]
             for hg in range(GROUP)], axis=0)
        return _dot_nt(q4, k_s[j * WINDOW:(j + 2) * WINDOW, g * HEAD_DIM:(g + 1) * HEAD_DIM])

    def attend(j, g, s):
        variant = first if j == 0 else 1
        ps, invs = [], []
        for hg in range(GROUP):
            head = g * GROUP + hg
            sh = s[hg * WINDOW:(hg + 1) * WINDOW] + tab_ref[variant, head]
            sk = sink_ref[layer, head]
            m = jnp.maximum(jnp.max(sh, axis=-1, keepdims=True), sk)
            p = jnp.exp(sh - m)
            den = jnp.sum(p, axis=-1, keepdims=True) + jnp.exp(sk - m)
            ps.append(p.astype(BF16))
            invs.append(1.0 / den)
        o4 = _dot(jnp.concatenate(ps, axis=0),
                  v_s[j * WINDOW:(j + 2) * WINDOW, g * HEAD_DIM:(g + 1) * HEAD_DIM])
        for hg in range(GROUP):
            head = g * GROUP + hg
            attn_s[j * WINDOW:(j + 1) * WINDOW, head * HEAD_DIM:(head + 1) * HEAD_DIM] = (
                o4[hg * WINDOW:(hg + 1) * WINDOW] * invs[hg]).astype(BF16)

    pairs = [(j, g) for j in range(nblk) for g in range(N_KV)]
    s_next = scores(*pairs[0])
    for i, (j, g) in enumerate(pairs):
        s_cur = s_next
        if i + 1 < len(pairs):
            s_next = scores(*pairs[i + 1])
        attend(j, g, s_cur)

    xrc_s[...] = xr_s[tc - ndelay * SUBLANES:tc, :]
    k_s[0:WINDOW, :] = k_s[tc:tc + WINDOW, :]
    v_s[0:WINDOW, :] = v_s[tc:tc + WINDOW, :]

    merged = (g_s[:, 0:D_MODEL] * _dot(lru_s[...], w_lo_ref[...])
              + g_s[:, D_MODEL:2 * D_MODEL] * _dot(attn_s[...], w_ao_ref[...]))
    m = _dot(merged.astype(BF16), w_out_ref[...])
    y_ref[...] = x_tile[...] + _rmsnorm(m, vec_ref[V_NORM_MIX_POST:V_NORM_MIX_POST + 1, :])

    @pl.when(t == pl.num_programs(1) - 1)
    def _():
        for k in range(SEG_LEN):
            rows = slice(k * SUBLANES, (k + 1) * SUBLANES)
            for g in range(N_KV):
                kstate_ref[:, k, g, :] = kvf_s[0, rows, g * HEAD_DIM:(g + 1) * HEAD_DIM]
                vstate_ref[:, k, g, :] = kvf_s[1, rows, g * HEAD_DIM:(g + 1) * HEAD_DIM]


def _prompt_mixer(x, vecs, w_in, wri, w_lo, w_ao, w_out, tab, sink, layer, prev_states, permute_in):
    bsz, seq, _ = x.shape
    tc = PROMPT_TILE
    kern = functools.partial(_pmix_kernel, layer=layer, tc=tc, permute_in=permute_in)
    tile = pl.BlockSpec((None, tc, D_MODEL), lambda b, t: (b, t, 0))
    kv_state = pl.BlockSpec((None, None, SUBLANES, SEG_LEN, N_KV, HEAD_DIM), lambda b, t: (layer, b, 0, 0, 0, 0))
    if permute_in:
        x = x.reshape(bsz, seq // SEG_LEN, SEG_LEN, D_MODEL)
        x_spec = pl.BlockSpec((None, tc // SEG_LEN, SEG_LEN, D_MODEL), lambda b, t: (b, t, 0, 0))
    else:
        x_spec = tile
    in_specs = [
        x_spec,
        _const_spec((None, VEC_ROWS, D_MODEL), (layer, 0, 0)),
        _const_spec((D_MODEL, IN_COLS), (0, 0)),
        _const_spec((None, N_LRU_BLOCKS, LRU_BLOCK, 2 * LRU_BLOCK), (layer, 0, 0, 0)),
        _const_spec((D_RNN, D_MODEL), (0, 0)),
        _const_spec((Q_COLS, D_MODEL), (0, 0)),
        _const_spec((D_MODEL, D_MODEL), (0, 0)),
        _const_spec((2, N_HEADS, WINDOW, 2 * WINDOW), (0, 0, 0, 0)),
        pl.BlockSpec(memory_space=pltpu.SMEM),
    ]
    args = [x, vecs, w_in, wri, w_lo, w_ao, w_out, tab, sink]
    kern, in_specs, args, aliases = _with_state_aliases(kern, len(args), in_specs, args, prev_states, 1)
    return pl.pallas_call(
        kern,
        grid=(bsz, seq // tc),
        in_specs=in_specs,
        out_specs=(tile,
                   pl.BlockSpec((None, bsz, D_RNN), lambda b, t: (layer, 0, 0)),
                   pl.BlockSpec((None, None, CONV_LRU - 1, D_RNN), lambda b, t: (layer, b, 0, 0)),
                   kv_state, kv_state),
        out_shape=(jax.ShapeDtypeStruct((bsz, seq, D_MODEL), F32),
                   jax.ShapeDtypeStruct((DEPTH, bsz, D_RNN), F32),
                   jax.ShapeDtypeStruct((DEPTH, bsz, CONV_LRU - 1, D_RNN), F32),
                   jax.ShapeDtypeStruct((DEPTH, bsz, SUBLANES, SEG_LEN, N_KV, HEAD_DIM), F32),
                   jax.ShapeDtypeStruct((DEPTH, bsz, SUBLANES, SEG_LEN, N_KV, HEAD_DIM), F32)),
        input_output_aliases=aliases,
        scratch_shapes=[
            pltpu.VMEM((tc if permute_in else SUBLANES, D_MODEL), F32),
            pltpu.VMEM((tc, D_MODEL), BF16),
            pltpu.VMEM((tc, D_RNN), F32),
            pltpu.VMEM(((CONV_LRU - 1) * SUBLANES, D_RNN), F32),
            pltpu.VMEM((tc, Q_COLS), BF16),
            pltpu.VMEM((WINDOW + tc, KV_COLS), BF16),
            pltpu.VMEM((WINDOW + tc, KV_COLS), BF16),
            pltpu.VMEM((2, WINDOW, KV_COLS), F32),
            pltpu.VMEM((tc, 2 * D_MODEL), F32),
            pltpu.VMEM((tc, D_RNN), BF16),
            pltpu.VMEM((tc, Q_COLS), BF16),
            pltpu.VMEM((SUBLANES, D_RNN), F32),
            pltpu.VMEM((bsz, D_RNN), F32),
        ],
        compiler_params=pltpu.CompilerParams(
            dimension_semantics=("arbitrary", "arbitrary"), vmem_limit_bytes=VMEM_LIMIT_BYTES),
        name=f"prompt_mixer_l{layer}",
    )(*args)


def _pffn_kernel(x_ref, vec_ref, fcv_ref, w_up_ref, w_down_ref, y_ref, fstate_ref,
                 h_s, tail_s, acc_s, *, tc, permute_out):
    t = pl.program_id(1)
    nblk = tc // WINDOW
    ndelay = CONV_FF - 1

    @pl.when(t == 0)
    def _():
        tail_s[...] = jnp.zeros(tail_s.shape, F32)

    sub = lax.broadcasted_iota(jnp.int32, (SUBLANES, PROMPT_FF_CHUNK), 0)
    n_chunks = D_FF // PROMPT_FF_CHUNK

    def chunk_cols(c, part):
        return slice(part * D_FF + c * PROMPT_FF_CHUNK, part * D_FF + (c + 1) * PROMPT_FF_CHUNK)

    h_s[...] = _rmsnorm(x_ref[...], vec_ref[V_NORM_FFN_PRE:V_NORM_FFN_PRE + 1, :]).astype(BF16)
    hb = h_s[...]

    def up_project(c):
        return [_dot(hb, w_up_ref[:, chunk_cols(c, part)]) for part in range(2)]

    u_next = up_project(0)
    for c in range(n_chunks):
        u_cur = u_next
        if c + 1 < n_chunks:
            u_next = up_project(c + 1)
        conv = []
        for part in range(2):
            cols = chunk_cols(c, part)
            u = u_cur[part]
            bias = jnp.broadcast_to(fcv_ref[CONV_FF:CONV_FF + 1, cols], (SUBLANES, PROMPT_FF_CHUNK))
            taps = [jnp.broadcast_to(fcv_ref[j:j + 1, cols], (SUBLANES, PROMPT_FF_CHUNK)) for j in range(CONV_FF)]
            ys = []
            prev_tail = [tail_s[i * SUBLANES:(i + 1) * SUBLANES, cols] for i in range(ndelay)]
            for j in range(nblk):
                groups = _row_groups(u, j)
                delayed = _delayed_groups(groups, prev_tail, ndelay, sub)
                for k in range(SEG_LEN):
                    y = bias + taps[CONV_FF - 1] * groups[k]
                    for d in range(1, CONV_FF):
                        y = y + taps[CONV_FF - 1 - d] * delayed[d - 1][k]
                    ys.append(y)
                prev_tail = groups[SEG_LEN - ndelay:]
            conv.append(jnp.concatenate(ys, axis=0))
            tail_s[:, cols] = u[tc - ndelay * SUBLANES:tc]
            for i in range(ndelay):
                r = tc - (ndelay - i) * SUBLANES + SUBLANES - 1
                fstate_ref[i:i + 1, cols] = u[r:r + 1]
        act = (_gelu_tanh_doubled(conv[1]) * conv[0]).astype(BF16)
        part_f = _dot(act, w_down_ref[c * PROMPT_FF_CHUNK:(c + 1) * PROMPT_FF_CHUNK, :])
        if c == 0:
            acc_s[...] = part_f
        else:
            acc_s[...] += part_f
    y = x_ref[...] + _rmsnorm(acc_s[...], vec_ref[V_NORM_FFN_POST:V_NORM_FFN_POST + 1, :])
    if permute_out:
        for j in range(nblk):
            for k in range(SEG_LEN):
                y_ref[j * SUBLANES:(j + 1) * SUBLANES, k, :] = (
                    y[j * WINDOW + k * SUBLANES:j * WINDOW + (k + 1) * SUBLANES])
    else:
        y_ref[...] = y


def _prompt_ffn(x, vecs, fcv, w_up, w_down, layer, prev_states, permute_out):
    bsz, seq, _ = x.shape
    tc = PROMPT_TILE
    kern = functools.partial(_pffn_kernel, tc=tc, permute_out=permute_out)
    tile = pl.BlockSpec((None, tc, D_MODEL), lambda b, t: (b, t, 0))
    if permute_out:
        y_shape = (bsz, seq // SEG_LEN, SEG_LEN, D_MODEL)
        y_spec = pl.BlockSpec((None, tc // SEG_LEN, SEG_LEN, D_MODEL), lambda b, t: (b, t, 0, 0))
    else:
        y_shape, y_spec = (bsz, seq, D_MODEL), tile
    in_specs = [
        tile,
        _const_spec((None, VEC_ROWS, D_MODEL), (layer, 0, 0)),
        _const_spec((None, SUBLANES, 2 * D_FF), (layer, 0, 0)),
        _const_spec((D_MODEL, 2 * D_FF), (0, 0)),
        _const_spec((D_FF, D_MODEL), (0, 0)),
    ]
    args = [x, vecs, fcv, w_up, w_down]
    kern, in_specs, args, aliases = _with_state_aliases(kern, len(args), in_specs, args, prev_states, 1)
    return pl.pallas_call(
        kern,
        grid=(bsz, seq // tc),
        in_specs=in_specs,
        out_specs=(y_spec, pl.BlockSpec((None, None, CONV_FF - 1, 2 * D_FF), lambda b, t: (layer, b, 0, 0))),
        out_shape=(jax.ShapeDtypeStruct(y_shape, F32),
                   jax.ShapeDtypeStruct((DEPTH, bsz, CONV_FF - 1, 2 * D_FF), F32)),
        input_output_aliases=aliases,
        scratch_shapes=[
            pltpu.VMEM((tc, D_MODEL), BF16),
            pltpu.VMEM(((CONV_FF - 1) * SUBLANES, 2 * D_FF), F32),
            pltpu.VMEM((tc, D_MODEL), F32),
        ],
        compiler_params=pltpu.CompilerParams(
            dimension_semantics=("arbitrary", "arbitrary"), vmem_limit_bytes=VMEM_LIMIT_BYTES),
        name=f"prompt_ffn_l{layer}",
    )(*args)


def _cast_rows(src_ref, dst_ref, step=LRU_BLOCK):
    for r in range(0, src_ref.shape[0], step):
        dst_ref[r:r + step, :] = src_ref[r:r + step, :].astype(BF16)


def _smix_in_kernel(xs_ref, h0_ref, cprev_ref, vec_ref, w_in_f32_ref, wri_ref,
                    q_ref, k_ref, v_ref, lru_ref, gate_ref, w_in_ref, hlast_ref, cstate_ref,
                    h_s, xr_s, *, nb, nt):
    _cast_rows(w_in_f32_ref, w_in_ref)
    for t in range(nt):
        h_s[t * nb:(t + 1) * nb, :] = _rmsnorm(
            xs_ref[:, t, :], vec_ref[V_NORM_MIX_PRE:V_NORM_MIX_PRE + 1, :]).astype(BF16)
    hb = h_s[...]
    xr_s[...] = _dot(hb, w_in_ref[:, 0:C_Q])
    qf = _dot(hb, w_in_ref[:, C_Q:C_K]) * (HEAD_DIM ** -0.5)
    kf = _dot(hb, w_in_ref[:, C_K:C_V])
    vf = _dot(hb, w_in_ref[:, C_V:C_G])
    for t in range(nt):
        rows = slice(t * nb, (t + 1) * nb)
        q_ref[:, t * Q_COLS:(t + 1) * Q_COLS] = qf[rows].astype(BF16)
        for g in range(N_KV):
            k_ref[:, t * N_KV + g, :] = kf[rows, g * HEAD_DIM:(g + 1) * HEAD_DIM]
            v_ref[:, t * N_KV + g, :] = vf[rows, g * HEAD_DIM:(g + 1) * HEAD_DIM]
    gate_ref[...] = _sigmoid(_dot(hb, w_in_ref[:, C_G:IN_COLS]))

    npre = CONV_LRU - 1
    for t in range(nt - npre, nt):
        cstate_ref[:, t - (nt - npre), :] = xr_s[t * nb:(t + 1) * nb, :]

    for n in range(N_LRU_BLOCKS):
        cb = slice(n * LRU_BLOCK, (n + 1) * LRU_BLOCK)
        xx = [cprev_ref[:, j, cb] for j in range(npre)]
        xx += [xr_s[t * nb:(t + 1) * nb, cb] for t in range(nt)]
        xcs = []
        for t in range(nt):
            xc = vec_ref[V_CONV_B:V_CONV_B + 1, cb]
            for j in range(CONV_LRU):
                xc = xc + vec_ref[V_CONV_W + j:V_CONV_W + j + 1, cb] * xx[t + j]
            xcs.append(xc)
        a, ix, mult = _lru_coeffs(jnp.concatenate(xcs, axis=0), vec_ref, wri_ref, n)
        b = mult * ix
        h = h0_ref[:, cb]
        for t in range(nt):
            rows = slice(t * nb, (t + 1) * nb)
            h = a[rows] * h + b[rows]
            lru_ref[rows, cb] = h.astype(BF16)
        hlast_ref[:, cb] = h


def _sample_mixer_in(xs, h0, cprev, vecs, w_in, wri, layer, prev_states):
    nb, nt, _ = xs.shape
    assert PAST_LEN > 0
    kern = functools.partial(_smix_in_kernel, nb=nb, nt=nt)
    whole = lambda shape: pl.BlockSpec(shape, lambda i: (0,) * len(shape))
    conv_state = pl.BlockSpec((None, nb, CONV_LRU - 1, D_RNN), lambda i: (layer, 0, 0, 0))
    in_specs = [
        whole((nb, nt, D_MODEL)),
        pl.BlockSpec((None, nb, D_RNN), lambda i: (layer, 0, 0)),
        conv_state,
        _const_spec((None, VEC_ROWS, D_MODEL), (layer, 0, 0)),
        _const_spec((None, D_MODEL, IN_COLS), (layer, 0, 0)),
        _const_spec((None, N_LRU_BLOCKS, LRU_BLOCK, 2 * LRU_BLOCK), (layer, 0, 0, 0)),
    ]
    args = [xs, h0, cprev, vecs, w_in, wri]
    kern, in_specs, args, aliases = _with_state_aliases(kern, len(args), in_specs, args, prev_states, 6)
    return pl.pallas_call(
        kern,
        grid=(1,),
        in_specs=in_specs,
        out_specs=(whole((nb, nt * Q_COLS)), whole((nb, nt * N_KV, HEAD_DIM)), whole((nb, nt * N_KV, HEAD_DIM)),
                   whole((nt * nb, D_RNN)), whole((nt * nb, 2 * D_MODEL)),
                   pl.BlockSpec((D_MODEL, IN_COLS), lambda i: (0, 0), pipeline_mode=pl.Buffered(1)),
                   pl.BlockSpec((None, nb, D_RNN), lambda i: (layer, 0, 0)), conv_state),
        out_shape=(jax.ShapeDtypeStruct((nb, nt * Q_COLS), BF16),
                   jax.ShapeDtypeStruct((nb, nt * N_KV, HEAD_DIM), F32),
                   jax.ShapeDtypeStruct((nb, nt * N_KV, HEAD_DIM), F32),
                   jax.ShapeDtypeStruct((nt * nb, D_RNN), BF16),
                   jax.ShapeDtypeStruct((nt * nb, 2 * D_MODEL), F32),
                   jax.ShapeDtypeStruct((D_MODEL, IN_COLS), BF16),
                   jax.ShapeDtypeStruct((DEPTH, nb, D_RNN), F32),
                   jax.ShapeDtypeStruct((DEPTH, nb, CONV_LRU - 1, D_RNN), F32)),
        input_output_aliases=aliases,
        scratch_shapes=[pltpu.VMEM((nt * nb, D_MODEL), BF16), pltpu.VMEM((nt * nb, D_RNN), F32)],
        compiler_params=pltpu.CompilerParams(
            dimension_semantics=("arbitrary",), vmem_limit_bytes=VMEM_LIMIT_BYTES),
        name=f"sample_mixer_in_l{layer}",
    )(*args)


def _sattn_kernel(q_ref, kn_ref, vn_ref, ck_ref, cv_ref, tab_ref, attn_ref, sk_ref, sv_ref,
                  kc_s, vc_s, *, bt, nt):
    n_cache = WINDOW * N_KV
    n_new = nt * N_KV
    for u in range(SAMPLE_ATTN_UNROLL):
        kc_s[u, n_cache + n_new:SAMPLE_KEYS, :] = jnp.zeros((SAMPLE_KEYS - n_cache - n_new, HEAD_DIM), F32)
        vc_s[u, n_cache + n_new:SAMPLE_KEYS, :] = jnp.zeros((SAMPLE_KEYS - n_cache - n_new, HEAD_DIM), F32)

    def body(i, _):
        seqs = [i * SAMPLE_ATTN_UNROLL + u for u in range(SAMPLE_ATTN_UNROLL)]
        scores = []
        for u, b in enumerate(seqs):
            for c_ref, n_ref, s_ref, scr in ((ck_ref, kn_ref, sk_ref, kc_s.at[u]), (cv_ref, vn_ref, sv_ref, vc_s.at[u])):
                scr[0:n_cache, :] = c_ref[b]
                scr[n_cache:n_cache + n_new, :] = n_ref[b]
                s_ref[b, 0:n_cache - n_new, :] = c_ref[b, n_new:n_cache, :]
                s_ref[b, n_cache - n_new:n_cache, :] = n_ref[b]
            scores.append(_dot_nt(q_ref[b], kc_s[u].astype(BF16)))
        probs = []
        for s in scores:
            s = s + tab_ref[...]
            p = jnp.exp(s - jnp.max(s, axis=-1, keepdims=True))
            probs.append((p.astype(BF16), 1.0 / jnp.sum(p, axis=-1, keepdims=True)))
        for u, b in enumerate(seqs):
            p, inv = probs[u]
            attn_ref[b] = (_dot(p, vc_s[u].astype(BF16)) * inv).astype(BF16)
        return 0

    lax.fori_loop(0, bt // SAMPLE_ATTN_UNROLL, body, 0)


def _sample_attention(q, kn, vn, cache_k, cache_v, stab, layer, prev_states):
    nb, rows, _ = q.shape
    nt = rows // N_HEADS
    bt = SAMPLE_BATCH_TILE
    kern = functools.partial(_sattn_kernel, bt=bt, nt=nt)
    cache_spec = pl.BlockSpec((None, bt, WINDOW * N_KV, HEAD_DIM), lambda i: (layer, i, 0, 0))
    new_spec = pl.BlockSpec((bt, nt * N_KV, HEAD_DIM), lambda i: (i, 0, 0))
    q_spec = pl.BlockSpec((bt, rows, HEAD_DIM), lambda i: (i, 0, 0))
    in_specs = [q_spec, new_spec, new_spec, cache_spec, cache_spec,
                _const_spec((None, rows, SAMPLE_KEYS), (layer, 0, 0))]
    args = [q, kn, vn, cache_k, cache_v, stab]
    kern, in_specs, args, aliases = _with_state_aliases(kern, len(args), in_specs, args, prev_states, 1)
    return pl.pallas_call(
        kern,
        grid=(nb // bt,),
        in_specs=in_specs,
        out_specs=(q_spec, cache_spec, cache_spec),
        out_shape=(jax.ShapeDtypeStruct((nb, rows, HEAD_DIM), BF16),
                   jax.ShapeDtypeStruct((DEPTH, nb, WINDOW * N_KV, HEAD_DIM), F32),
                   jax.ShapeDtypeStruct((DEPTH, nb, WINDOW * N_KV, HEAD_DIM), F32)),
        input_output_aliases=aliases,
        scratch_shapes=[pltpu.VMEM((SAMPLE_ATTN_UNROLL, SAMPLE_KEYS, HEAD_DIM), F32),
                        pltpu.VMEM((SAMPLE_ATTN_UNROLL, SAMPLE_KEYS, HEAD_DIM), F32)],
        compiler_params=pltpu.CompilerParams(
            dimension_semantics=("arbitrary",), vmem_limit_bytes=VMEM_LIMIT_BYTES),
        name=f"sample_attention_l{layer}",
    )(*args)


def _smix_out_kernel(xs_ref, lru_ref, attn_ref, gate_ref, vec_ref, w_lo_f32_ref, w_ao_f32_ref, w_out_f32_ref,
                     xmid_ref, w_lo_ref, w_ao_ref, w_out_ref, a_s, *, nb, nt):
    _cast_rows(w_lo_f32_ref, w_lo_ref)
    _cast_rows(w_ao_f32_ref, w_ao_ref)
    _cast_rows(w_out_f32_ref, w_out_ref)
    for t in range(nt):
        a_s[t * nb:(t + 1) * nb, :] = attn_ref[:, t * Q_COLS:(t + 1) * Q_COLS]
    merged = (gate_ref[:, 0:D_MODEL] * _dot(lru_ref[...], w_lo_ref[...])
              + gate_ref[:, D_MODEL:2 * D_MODEL] * _dot(a_s[...], w_ao_ref[...]))
    m = _dot(merged.astype(BF16), w_out_ref[...])
    mn = _rmsnorm(m, vec_ref[V_NORM_MIX_POST:V_NORM_MIX_POST + 1, :])
    for t in range(nt):
        rows = slice(t * nb, (t + 1) * nb)
        xmid_ref[rows, :] = xs_ref[:, t, :] + mn[rows]


def _sample_mixer_out(xs, lru, attn, gates, vecs, w_lo, w_ao, w_out, layer):
    nb, nt, _ = xs.shape
    kern = functools.partial(_smix_out_kernel, nb=nb, nt=nt)
    whole = lambda shape: pl.BlockSpec(shape, lambda i: (0,) * len(shape))
    return pl.pallas_call(
        kern,
        grid=(1,),
        in_specs=[
            whole(xs.shape), whole(lru.shape), whole(attn.shape), whole(gates.shape),
            _const_spec((None, VEC_ROWS, D_MODEL), (layer, 0, 0)),
            _const_spec((None, D_RNN, D_MODEL), (layer, 0, 0)),
            _const_spec((None, Q_COLS, D_MODEL), (layer, 0, 0)),
            _const_spec((None, D_MODEL, D_MODEL), (layer, 0, 0)),
        ],
        out_specs=(whole((nt * nb, D_MODEL)),) + (whole((D_MODEL, D_MODEL)),) * 3,
        out_shape=(jax.ShapeDtypeStruct((nt * nb, D_MODEL), F32),)
        + (jax.ShapeDtypeStruct((D_MODEL, D_MODEL), BF16),) * 3,
        scratch_shapes=[pltpu.VMEM((nt * nb, Q_COLS), BF16)],
        compiler_params=pltpu.CompilerParams(
            dimension_semantics=("arbitrary",), vmem_limit_bytes=VMEM_LIMIT_BYTES),
        name=f"sample_mixer_out_l{layer}",
    )(xs, lru, attn, gates, vecs, w_lo, w_ao, w_out)


def _sffn_kernel(x_ref, vec_ref, fcv_ref, prev_ref, w_f32_ref, wd_f32_ref, ys_ref, w_ref, wd_ref, fstate_ref,
                 h_s, val_s, acc_s, *, nb, nt, nck):
    c = pl.program_id(0)

    @pl.when(c == 0)
    def _():
        h_s[...] = _rmsnorm(x_ref[...], vec_ref[V_NORM_FFN_PRE:V_NORM_FFN_PRE + 1, :]).astype(BF16)
        acc_s[...] = jnp.zeros(acc_s.shape, F32)

    _cast_rows(w_f32_ref, w_ref)
    u = _dot(h_s[...], w_ref[...])
    uu = [prev_ref[:, j, :] for j in range(CONV_FF - 1)] + [u[t * nb:(t + 1) * nb] for t in range(nt)]
    ys = []
    for t in range(nt):
        y = fcv_ref[CONV_FF:CONV_FF + 1, :]
        for j in range(CONV_FF):
            y = y + fcv_ref[j:j + 1, :] * uu[t + j]
        ys.append(y)
    conv = jnp.concatenate(ys, axis=0)
    for j in range(CONV_FF - 1):
        fstate_ref[:, j, :] = uu[nt + j]

    @pl.when(c < nck)
    def _():
        val_s[c] = conv

    @pl.when(c >= nck)
    def _():
        act = (_gelu_tanh_doubled(conv) * val_s[c - nck]).astype(BF16)
        _cast_rows(wd_f32_ref, wd_ref)
        acc_s[...] += _dot(act, wd_ref[...])

    @pl.when(c == 2 * nck - 1)
    def _():
        y = x_ref[...] + _rmsnorm(acc_s[...], vec_ref[V_NORM_FFN_POST:V_NORM_FFN_POST + 1, :])
        for t in range(nt):
            ys_ref[:, t, :] = y[t * nb:(t + 1) * nb]


def _sample_ffn(xmid, fprev, vecs, fcv, w_up, w_down, layer, nb, prev_states):
    nt = xmid.shape[0] // nb
    ck = FF_CHUNK
    nck = D_FF // ck
    kern = functools.partial(_sffn_kernel, nb=nb, nt=nt, nck=nck)
    state_spec = pl.BlockSpec((None, nb, CONV_FF - 1, ck), lambda c: (layer, 0, 0, c))
    in_specs = [
        _const_spec((nt * nb, D_MODEL), (0, 0)),
        _const_spec((None, VEC_ROWS, D_MODEL), (layer, 0, 0)),
        pl.BlockSpec((None, SUBLANES, ck), lambda c: (layer, 0, c)),
        state_spec,
        pl.BlockSpec((None, D_MODEL, ck), lambda c: (layer, 0, c)),
        pl.BlockSpec((None, ck, D_MODEL), lambda c: (layer, jnp.maximum(c - nck, 0), 0)),
    ]
    args = [xmid, vecs, fcv, fprev, w_up, w_down]
    kern, in_specs, args, aliases = _with_state_aliases(kern, len(args), in_specs, args, prev_states, 3)
    return pl.pallas_call(
        kern,
        grid=(2 * nck,),
        in_specs=in_specs,
        out_specs=(pl.BlockSpec((nb, nt, D_MODEL), lambda c: (0, 0, 0)),
                   pl.BlockSpec((D_MODEL, ck), lambda c: (0, c)),
                   pl.BlockSpec((ck, D_MODEL), lambda c: (jnp.maximum(c - nck, 0), 0)),
                   state_spec),
        out_shape=(jax.ShapeDtypeStruct((nb, nt, D_MODEL), F32),
                   jax.ShapeDtypeStruct((D_MODEL, 2 * D_FF), BF16),
                   jax.ShapeDtypeStruct((D_FF, D_MODEL), BF16),
                   jax.ShapeDtypeStruct((DEPTH, nb, CONV_FF - 1, 2 * D_FF), F32)),
        input_output_aliases=aliases,
        scratch_shapes=[pltpu.VMEM((nt * nb, D_MODEL), BF16),
                        pltpu.VMEM((nck, nt * nb, ck), F32),
                        pltpu.VMEM((nt * nb, D_MODEL), F32)],
        compiler_params=pltpu.CompilerParams(
            dimension_semantics=("arbitrary",), vmem_limit_bytes=VMEM_LIMIT_BYTES),
        name=f"sample_ffn_l{layer}",
    )(*args)


def kernel(x_prompt, x_sample, state_lru_h, state_lru_conv, cache_win_k, cache_win_v, state_ffn_conv,
           norm_mix_pre, norm_mix_post, norm_ffn_pre, norm_ffn_post, w_in, conv_lru_w, conv_lru_b,
           lru_wr, lru_br, lru_wi, lru_bi, lru_lambda, w_lru_o, w_attn_o, w_out, attn_sink, rel_bias,
           w_up, ffn_conv_w, ffn_conv_b, w_down):
    nb, nt, _ = x_sample.shape
    bp = x_prompt.shape[0]

    row = lambda v: v[:, None, :]
    vecs = jnp.concatenate(
        [row(norm_mix_pre), row(norm_mix_post), row(conv_lru_b), row(lru_br), row(lru_bi), row(lru_lambda),
         conv_lru_w, row(norm_ffn_pre), row(norm_ffn_post),
         jnp.zeros((DEPTH, VEC_ROWS - 12, D_MODEL), F32)], axis=1)
    fcv = jnp.concatenate(
        [ffn_conv_w, row(ffn_conv_b), jnp.zeros((DEPTH, SUBLANES - CONV_FF - 1, 2 * D_FF), F32)], axis=1)
    fcv = fcv * jnp.where(jnp.arange(2 * D_FF) < D_FF, 0.5, 1.0).astype(F32)
    wri_b = jnp.concatenate([lru_wr, lru_wi], axis=-1).astype(BF16)

    ptab, stab = _bias_tables(rel_bias, attn_sink)

    yp, xs = x_prompt, x_sample
    p_mix = p_ffn = s_mix = s_att = s_ffn = None
    cache_rows = lambda c: c.reshape(DEPTH, nb, WINDOW * N_KV, HEAD_DIM)
    for l in range(DEPTH):
        q, kn, vn, lru, gates, w_in_b, *s_mix = _sample_mixer_in(
            xs, state_lru_h, state_lru_conv, vecs, w_in, wri_b, l, s_mix)
        attn, *s_att = _sample_attention(
            q.reshape(nb, nt * N_HEADS, HEAD_DIM), kn, vn, cache_rows(cache_win_k), cache_rows(cache_win_v),
            stab, l, s_att)
        xmid, w_lo_b, w_ao_b, w_out_b = _sample_mixer_out(
            xs, lru, attn.reshape(nb, nt * Q_COLS), gates, vecs, w_lru_o, w_attn_o, w_out, l)
        xs, w_up_b, w_down_b, *s_ffn = _sample_ffn(xmid, state_ffn_conv, vecs, fcv, w_up, w_down, l, nb, s_ffn)

        yp, *p_mix = _prompt_mixer(yp, vecs, w_in_b, wri_b, w_lo_b, w_ao_b, w_out_b, ptab, attn_sink, l, p_mix,
                                   permute_in=(l == 0))
        yp, *p_ffn = _prompt_ffn(yp, vecs, fcv, w_up_b, w_down_b, l, p_ffn, permute_out=(l == DEPTH - 1))

    p_h, p_c, p_k, p_v = p_mix
    s_h, s_c = s_mix
    s_k, s_v = s_att
    kv_shape = (DEPTH, bp, WINDOW, N_KV, HEAD_DIM)
    return (yp.reshape(x_prompt.shape), xs, p_h, p_c, p_k.reshape(kv_shape), p_v.reshape(kv_shape), p_ffn[0],
            s_h, s_c, s_k.reshape(cache_win_k.shape), s_v.reshape(cache_win_v.shape), s_ffn[0])
```

```python
import functools
import math

import numpy as np
import jax
import jax.numpy as jnp
from jax import lax
from jax.experimental import pallas as pl
from jax.experimental.pallas import tpu as pltpu

D_MODEL = 1024
DEPTH = 2
PAST_LEN = 16384
D_RNN = D_MODEL
N_LRU_BLOCKS = 8
LRU_BLOCK = D_RNN // N_LRU_BLOCKS
CONV_LRU = 4
LRU_C = 8.0
N_HEADS = 8
N_KV = 2
GROUP = N_HEADS // N_KV
HEAD_DIM = D_MODEL // N_HEADS
WINDOW = 128
N_BUCKETS = 32
MAX_EXACT = N_BUCKETS // 2
MAX_DISTANCE = 128
D_FF = 4 * D_MODEL
CONV_FF = 3
EPS = 1e-6
Q_COLS = N_HEADS * HEAD_DIM
KV_COLS = N_KV * HEAD_DIM
IN_COLS = D_RNN + Q_COLS + 2 * KV_COLS + 2 * D_MODEL
C_Q = D_RNN
C_K = C_Q + Q_COLS
C_V = C_K + KV_COLS
C_G = C_V + KV_COLS

F32 = jnp.float32
BF16 = jnp.bfloat16

SUBLANES = 8
LANES = 128
VMEM_LIMIT_BYTES = 56 * 1024 * 1024

V_NORM_MIX_PRE, V_NORM_MIX_POST, V_CONV_B, V_BR, V_BI, V_LAMBDA, V_CONV_W = 0, 1, 2, 3, 4, 5, 6
V_NORM_FFN_PRE, V_NORM_FFN_POST = 10, 11
VEC_ROWS = 16

PROMPT_TILE = 512
PROJ_PIECE = 2 * KV_COLS
assert (C_K - C_Q) % PROJ_PIECE == 0 and (IN_COLS - C_G) % PROJ_PIECE == 0
PROMPT_FF_CHUNK = 2048
FF_CHUNK = 1024
SAMPLE_KEYS = N_KV * (WINDOW + 4) + SUBLANES
SINK_COL = N_KV * (WINDOW + 4)
SAMPLE_BATCH_TILE = 16
SAMPLE_ATTN_UNROLL = 8
SEG_LEN = WINDOW // SUBLANES


def _bucket_thresholds():
    d = np.arange(0, 2 * WINDOW)
    nf = np.maximum(d, 1).astype(np.float64)
    large = MAX_EXACT + (np.log(nf / MAX_EXACT) / math.log(MAX_DISTANCE / MAX_EXACT)
                         * (N_BUCKETS - MAX_EXACT)).astype(np.int64)
    bucket = np.where(d < MAX_EXACT, d, np.minimum(large, N_BUCKETS - 1))
    return tuple(int(d[bucket >= b].min()) for b in range(1, N_BUCKETS))


_BUCKET_THRESHOLDS = _bucket_thresholds()


def _dot(a, b):
    return jnp.dot(a, b, preferred_element_type=F32)


def _dot_nt(a, b):
    return lax.dot_general(a, b, (((1,), (1,)), ((), ())), preferred_element_type=F32)


def _rmsnorm(x, g):
    return x * lax.rsqrt(jnp.mean(x * x, axis=-1, keepdims=True) + EPS) * g


def _sigmoid(x):
    return 1.0 / (1.0 + jnp.exp(-x))


def _gelu_tanh_doubled(x):
    c = math.sqrt(2.0 / math.pi)
    t = jnp.tanh(x * (c + (c * 0.044715) * (x * x)))
    return x + x * t


def _const_spec(block_shape, index):
    return pl.BlockSpec(block_shape, lambda *_: index, pipeline_mode=pl.Buffered(1))


def _with_state_aliases(kern, n_in, in_specs, args, prev_states, first_state_out):
    if prev_states is None:
        return kern, list(in_specs), list(args), {}
    n = len(prev_states)

    def body(*refs):
        return kern(*refs[:n_in], *refs[n_in + n:])

    return (body, list(in_specs) + [pl.BlockSpec(memory_space=pl.ANY)] * n, list(args) + list(prev_states),
            {n_in + i: first_state_out + i for i in range(n)})


def _bucket_of(d):
    n = jnp.maximum(d, 0)
    bucket = jnp.zeros(d.shape, jnp.int32)
    for thr in _BUCKET_THRESHOLDS:
        bucket = bucket + jnp.where(n >= thr, 1, 0)
    return bucket


def _block_time(p):
    return lax.bitwise_and(p, SUBLANES - 1) * SEG_LEN + lax.shift_right_logical(p, 3)


def _table_kernel(rel_ref, sink_ref, pt_ref, st_ref):
    qi = lax.broadcasted_iota(jnp.int32, (WINDOW, 2 * WINDOW), 0)
    kj = lax.broadcasted_iota(jnp.int32, (WINDOW, 2 * WINDOW), 1)
    kpos = lax.bitwise_and(kj, WINDOW - 1)
    d = _block_time(qi) + WINDOW - (_block_time(kpos) + (kj - kpos))
    bucket = _bucket_of(d)
    in_band = jnp.where(d >= 0, jnp.where(d < WINDOW, 1, 0), 0)
    cur_only = jnp.where(kj >= WINDOW, in_band, 0)
    for h in range(N_HEADS):
        val = jnp.zeros(d.shape, F32)
        for b in range(N_BUCKETS):
            val = jnp.where(bucket == b, rel_ref[b, h], val)
        pt_ref[1, h] = jnp.where(in_band == 1, val, -jnp.inf)
        pt_ref[0, h] = jnp.where(cur_only == 1, val, -jnp.inf)

    r = lax.broadcasted_iota(jnp.int32, (4 * N_HEADS, SAMPLE_KEYS), 0)
    j = lax.broadcasted_iota(jnp.int32, (4 * N_HEADS, SAMPLE_KEYS), 1)
    t = lax.shift_right_logical(r, 3)
    hh = lax.bitwise_and(r, N_HEADS - 1)
    d = t + WINDOW - lax.shift_right_logical(j, 1)
    bucket = _bucket_of(d)
    in_band = jnp.where(d >= 0, jnp.where(d < WINDOW, 1, 0), 0)
    in_band = jnp.where(lax.bitwise_and(j, N_KV - 1) == lax.shift_right_logical(hh, 2), in_band, 0)
    val = jnp.zeros(d.shape, F32)
    for h in range(N_HEADS):
        hval = jnp.zeros(d.shape, F32)
        for b in range(N_BUCKETS):
            hval = jnp.where(bucket == b, rel_ref[b, h], hval)
        val = jnp.where(hh == h, hval, val)
    val = jnp.where(in_band == 1, val, -jnp.inf)
    for l in range(DEPTH):
        sk = jnp.zeros(d.shape, F32)
        for h in range(N_HEADS):
            sk = jnp.where(hh == h, sink_ref[l, h], sk)
        st_ref[l] = jnp.where(j == SINK_COL, sk, val)


def _bias_tables(rel_bias, attn_sink):
    smem = pl.BlockSpec(memory_space=pltpu.SMEM)
    return pl.pallas_call(
        _table_kernel,
        out_shape=(jax.ShapeDtypeStruct((2, N_HEADS, WINDOW, 2 * WINDOW), F32),
                   jax.ShapeDtypeStruct((DEPTH, 4 * N_HEADS, SAMPLE_KEYS), F32)),
        in_specs=[smem, smem],
        name="bias_tables",
    )(rel_bias, attn_sink)


def _lru_coeffs(xc, vec_ref, wri_ref, n):
    cb = slice(n * LRU_BLOCK, (n + 1) * LRU_BLOCK)
    rw = _dot(xc.astype(BF16), wri_ref[n])
    r = _sigmoid(rw[:, :LRU_BLOCK] + vec_ref[V_BR:V_BR + 1, cb])
    i = _sigmoid(rw[:, LRU_BLOCK:] + vec_ref[V_BI:V_BI + 1, cb])
    z = -vec_ref[V_LAMBDA:V_LAMBDA + 1, cb]
    softplus = jnp.maximum(z, 0.0) + jnp.log1p(jnp.exp(-jnp.abs(z)))
    log_a = (-LRU_C * softplus) * r
    a = jnp.exp(log_a)
    mult = jnp.sqrt(jnp.maximum(1.0 - a * a, 0.0))
    return a, i * xc, mult


def _delayed_groups(groups, prev_tail, ndelay, sub):
    ng = len(groups)
    wrapped = {}
    for i in range(ndelay):
        k = ng - ndelay + i
        wrapped[k] = jnp.where(sub == 0, pltpu.roll(prev_tail[i], 1, 0), pltpu.roll(groups[k], 1, 0))
    return [[groups[k - d] if k >= d else wrapped[ng + k - d] for k in range(ng)] for d in range(1, ndelay + 1)]


def _row_groups(x, block):
    return [x[block * WINDOW + k * SUBLANES:block * WINDOW + (k + 1) * SUBLANES] for k in range(SEG_LEN)]


def _pmix_kernel(x_ref, vec_ref, w_in_ref, wri_ref, w_lo_ref, w_ao_ref, w_out_ref, tab_ref, sink_ref,
                 y_ref, hlast_ref, cstate_ref, kstate_ref, vstate_ref,
                 xp_s, h_s, xr_s, xrc_s, q_s, k_s, v_s, kvf_s, g_s, lru_s, attn_s, hc_s, hl_s, *, layer, tc, permute_in):
    t = pl.program_id(1)
    nblk = tc // WINDOW
    ndelay = CONV_LRU - 1
    n_pieces = (IN_COLS - C_Q) // PROJ_PIECE
    batch_row = lax.broadcasted_iota(jnp.int32, (hl_s.shape[0], LRU_BLOCK), 0)

    @pl.when(jnp.logical_and(t == 0, pl.program_id(0) == 0))
    def _():
        hl_s[...] = jnp.zeros(hl_s.shape, F32)

    @pl.when(t == 0)
    def _():
        xrc_s[...] = jnp.zeros(xrc_s.shape, F32)
        k_s[0:WINDOW, :] = jnp.zeros((WINDOW, KV_COLS), BF16)
        v_s[0:WINDOW, :] = jnp.zeros((WINDOW, KV_COLS), BF16)
        hc_s[...] = jnp.zeros(hc_s.shape, F32)

    if permute_in:
        for j in range(nblk):
            for k in range(SEG_LEN):
                xp_s[j * WINDOW + k * SUBLANES:j * WINDOW + (k + 1) * SUBLANES, :] = (
                    x_ref[j * SUBLANES:(j + 1) * SUBLANES, k, :])
        x_tile = xp_s
    else:
        x_tile = x_ref

    h_s[...] = _rmsnorm(x_tile[...], vec_ref[V_NORM_MIX_PRE:V_NORM_MIX_PRE + 1, :]).astype(BF16)
    hb = h_s[...]
    xr_s[...] = _dot(hb, w_in_ref[:, 0:C_Q])

    def project_piece(i):
        c0 = C_Q + i * PROJ_PIECE
        z = _dot(hb, w_in_ref[:, c0:c0 + PROJ_PIECE])
        if c0 < C_K:
            q_s[:, c0 - C_Q:c0 - C_Q + PROJ_PIECE] = (z * (HEAD_DIM ** -0.5)).astype(BF16)
        elif c0 == C_K:
            k_s[WINDOW:WINDOW + tc, :] = z[:, 0:KV_COLS].astype(BF16)
            v_s[WINDOW:WINDOW + tc, :] = z[:, KV_COLS:2 * KV_COLS].astype(BF16)
            kvf_s[0] = z[tc - WINDOW:tc, 0:KV_COLS]
            kvf_s[1] = z[tc - WINDOW:tc, KV_COLS:2 * KV_COLS]
        else:
            g_s[:, c0 - C_G:c0 - C_G + PROJ_PIECE] = _sigmoid(z)

    for i in range(ndelay):
        r = tc - (ndelay - i) * SUBLANES + SUBLANES - 1
        cstate_ref[i:i + 1, :] = xr_s[r:r + 1, :]

    sub = lax.broadcasted_iota(jnp.int32, (SUBLANES, LRU_BLOCK), 0)
    seq_start = (sub + t) == 0
    for n in range(N_LRU_BLOCKS):
        cb = slice(n * LRU_BLOCK, (n + 1) * LRU_BLOCK)
        bias = jnp.broadcast_to(vec_ref[V_CONV_B:V_CONV_B + 1, cb], (SUBLANES, LRU_BLOCK))
        taps = [jnp.broadcast_to(vec_ref[V_CONV_W + j:V_CONV_W + j + 1, cb], (SUBLANES, LRU_BLOCK))
                for j in range(CONV_LRU)]
        xcs = []
        prev_tail = [xrc_s[i * SUBLANES:(i + 1) * SUBLANES, cb] for i in range(ndelay)]
        for j in range(nblk):
            groups = [xr_s[j * WINDOW + k * SUBLANES:j * WINDOW + (k + 1) * SUBLANES, cb] for k in range(SEG_LEN)]
            delayed = _delayed_groups(groups, prev_tail, ndelay, sub)
            for k in range(SEG_LEN):
                xc = bias + taps[CONV_LRU - 1] * groups[k]
                for d in range(1, CONV_LRU):
                    xc = xc + taps[CONV_LRU - 1 - d] * delayed[d - 1][k]
                xcs.append(xc)
            prev_tail = groups[SEG_LEN - ndelay:]
        a, ix, mult = _lru_coeffs(jnp.concatenate(xcs, axis=0), vec_ref, wri_ref, n)
        b = mult * ix

        carry = hc_s[0:1, cb]
        hs = []
        for j in range(nblk):
            ag, bg, ig = _row_groups(a, j), _row_groups(b, j), _row_groups(ix, j)
            if j == 0:
                bg[0] = jnp.where(seq_start, ig[0], bg[0])
            acc_a, acc_b = [ag[0]], [bg[0]]
            for k in range(1, SEG_LEN):
                acc_b.append(ag[k] * acc_b[-1] + bg[k])
                acc_a.append(ag[k] * acc_a[-1])
            seg_a, seg_b = acc_a[-1], acc_b[-1]
            for s in (1, 2, 4):
                ash = jnp.where(sub >= s, pltpu.roll(seg_a, s, 0), 1.0)
                bsh = jnp.where(sub >= s, pltpu.roll(seg_b, s, 0), 0.0)
                seg_b = seg_a * bsh + seg_b
                seg_a = seg_a * ash
            h_end = seg_a * carry + seg_b
            h_in = jnp.where(sub == 0, carry, pltpu.roll(h_end, 1, 0))
            hs += [acc_a[k] * h_in + acc_b[k] for k in range(SEG_LEN)]
            carry = h_end[SUBLANES - 1:SUBLANES, :]
        lru_s[:, cb] = jnp.concatenate(hs, axis=0).astype(BF16)
        hc_s[0:1, cb] = carry
        hl_s[:, cb] = jnp.where(batch_row == pl.program_id(0), carry, hl_s[:, cb])
        if n < n_pieces:
            project_piece(n)
    for i in range(N_LRU_BLOCKS, n_pieces):
        project_piece(i)
    hlast_ref[...] = hl_s[...]

    first = jnp.where(t == 0, 0, 1)

    def scores(j, g):
        q4 = jnp.concatenate(
            [q_s[j * WINDOW:(j + 1) * WINDOW, (g * GROUP + hg) * HEAD_DIM:(g * GROUP + hg + 1) * HEAD_DIM]
             for hg in range(GROUP)], axis=0)
        return _dot_nt(q4, k_s[j * WINDOW:(j + 2) * WINDOW, g * HEAD_DIM:(g + 1) * HEAD_DIM])

    def attend(j, g, s):
        variant = first if j == 0 else 1
        ps, invs = [], []
        for hg in range(GROUP):
            head = g * GROUP + hg
            sh = s[hg * WINDOW:(hg + 1) * WINDOW] + tab_ref[variant, head]
            sk = sink_ref[layer, head]
            m = jnp.maximum(jnp.max(sh, axis=-1, keepdims=True), sk)
            p = jnp.exp(sh - m)
            den = jnp.sum(p, axis=-1, keepdims=True) + jnp.exp(sk - m)
            ps.append(p.astype(BF16))
            invs.append(1.0 / den)
        o4 = _dot(jnp.concatenate(ps, axis=0),
                  v_s[j * WINDOW:(j + 2) * WINDOW, g * HEAD_DIM:(g + 1) * HEAD_DIM])
        for hg in range(GROUP):
            head = g * GROUP + hg
            attn_s[j * WINDOW:(j + 1) * WINDOW, head * HEAD_DIM:(head + 1) * HEAD_DIM] = (
                o4[hg * WINDOW:(hg + 1) * WINDOW] * invs[hg]).astype(BF16)

    pairs = [(j, g) for j in range(nblk) for g in range(N_KV)]
    s_next = scores(*pairs[0])
    for i, (j, g) in enumerate(pairs):
        s_cur = s_next
        if i + 1 < len(pairs):
            s_next = scores(*pairs[i + 1])
        attend(j, g, s_cur)

    xrc_s[...] = xr_s[tc - ndelay * SUBLANES:tc, :]
    k_s[0:WINDOW, :] = k_s[tc:tc + WINDOW, :]
    v_s[0:WINDOW, :] = v_s[tc:tc + WINDOW, :]

    merged = (g_s[:, 0:D_MODEL] * _dot(lru_s[...], w_lo_ref[...])
              + g_s[:, D_MODEL:2 * D_MODEL] * _dot(attn_s[...], w_ao_ref[...]))
    m = _dot(merged.astype(BF16), w_out_ref[...])
    y_ref[...] = x_tile[...] + _rmsnorm(m, vec_ref[V_NORM_MIX_POST:V_NORM_MIX_POST + 1, :])

    @pl.when(t == pl.num_programs(1) - 1)
    def _():
        for k in range(SEG_LEN):
            rows = slice(k * SUBLANES, (k + 1) * SUBLANES)
            for g in range(N_KV):
                kstate_ref[:, k, g, :] = kvf_s[0, rows, g * HEAD_DIM:(g + 1) * HEAD_DIM]
                vstate_ref[:, k, g, :] = kvf_s[1, rows, g * HEAD_DIM:(g + 1) * HEAD_DIM]


def _prompt_mixer(x, vecs, w_in, wri, w_lo, w_ao, w_out, tab, sink, layer, prev_states, permute_in):
    bsz, seq, _ = x.shape
    tc = PROMPT_TILE
    kern = functools.partial(_pmix_kernel, layer=layer, tc=tc, permute_in=permute_in)
    tile = pl.BlockSpec((None, tc, D_MODEL), lambda b, t: (b, t, 0))
    kv_state = pl.BlockSpec((None, None, SUBLANES, SEG_LEN, N_KV, HEAD_DIM), lambda b, t: (layer, b, 0, 0, 0, 0))
    if permute_in:
        x = x.reshape(bsz, seq // SEG_LEN, SEG_LEN, D_MODEL)
        x_spec = pl.BlockSpec((None, tc // SEG_LEN, SEG_LEN, D_MODEL), lambda b, t: (b, t, 0, 0))
    else:
        x_spec = tile
    in_specs = [
        x_spec,
        _const_spec((None, VEC_ROWS, D_MODEL), (layer, 0, 0)),
        _const_spec((D_MODEL, IN_COLS), (0, 0)),
        _const_spec((None, N_LRU_BLOCKS, LRU_BLOCK, 2 * LRU_BLOCK), (layer, 0, 0, 0)),
        _const_spec((D_RNN, D_MODEL), (0, 0)),
        _const_spec((Q_COLS, D_MODEL), (0, 0)),
        _const_spec((D_MODEL, D_MODEL), (0, 0)),
        _const_spec((2, N_HEADS, WINDOW, 2 * WINDOW), (0, 0, 0, 0)),
        pl.BlockSpec(memory_space=pltpu.SMEM),
    ]
    args = [x, vecs, w_in, wri, w_lo, w_ao, w_out, tab, sink]
    kern, in_specs, args, aliases = _with_state_aliases(kern, len(args), in_specs, args, prev_states, 1)
    return pl.pallas_call(
        kern,
        grid=(bsz, seq // tc),
        in_specs=in_specs,
        out_specs=(tile,
                   pl.BlockSpec((None, bsz, D_RNN), lambda b, t: (layer, 0, 0)),
                   pl.BlockSpec((None, None, CONV_LRU - 1, D_RNN), lambda b, t: (layer, b, 0, 0)),
                   kv_state, kv_state),
        out_shape=(jax.ShapeDtypeStruct((bsz, seq, D_MODEL), F32),
                   jax.ShapeDtypeStruct((DEPTH, bsz, D_RNN), F32),
                   jax.ShapeDtypeStruct((DEPTH, bsz, CONV_LRU - 1, D_RNN), F32),
                   jax.ShapeDtypeStruct((DEPTH, bsz, SUBLANES, SEG_LEN, N_KV, HEAD_DIM), F32),
                   jax.ShapeDtypeStruct((DEPTH, bsz, SUBLANES, SEG_LEN, N_KV, HEAD_DIM), F32)),
        input_output_aliases=aliases,
        scratch_shapes=[
            pltpu.VMEM((tc if permute_in else SUBLANES, D_MODEL), F32),
            pltpu.VMEM((tc, D_MODEL), BF16),
            pltpu.VMEM((tc, D_RNN), F32),
            pltpu.VMEM(((CONV_LRU - 1) * SUBLANES, D_RNN), F32),
            pltpu.VMEM((tc, Q_COLS), BF16),
            pltpu.VMEM((WINDOW + tc, KV_COLS), BF16),
            pltpu.VMEM((WINDOW + tc, KV_COLS), BF16),
            pltpu.VMEM((2, WINDOW, KV_COLS), F32),
            pltpu.VMEM((tc, 2 * D_MODEL), F32),
            pltpu.VMEM((tc, D_RNN), BF16),
            pltpu.VMEM((tc, Q_COLS), BF16),
            pltpu.VMEM((SUBLANES, D_RNN), F32),
            pltpu.VMEM((bsz, D_RNN), F32),
        ],
        compiler_params=pltpu.CompilerParams(
            dimension_semantics=("arbitrary", "arbitrary"), vmem_limit_bytes=VMEM_LIMIT_BYTES),
        name=f"prompt_mixer_l{layer}",
    )(*args)


def _pffn_kernel(x_ref, vec_ref, fcv_ref, w_up_ref, w_down_ref, y_ref, fstate_ref,
                 h_s, tail_s, acc_s, *, tc, permute_out):
    t = pl.program_id(1)
    nblk = tc // WINDOW
    ndelay = CONV_FF - 1

    @pl.when(t == 0)
    def _():
        tail_s[...] = jnp.zeros(tail_s.shape, F32)

    sub = lax.broadcasted_iota(jnp.int32, (SUBLANES, PROMPT_FF_CHUNK), 0)
    n_chunks = D_FF // PROMPT_FF_CHUNK

    def chunk_cols(c, part):
        return slice(part * D_FF + c * PROMPT_FF_CHUNK, part * D_FF + (c + 1) * PROMPT_FF_CHUNK)

    h_s[...] = _rmsnorm(x_ref[...], vec_ref[V_NORM_FFN_PRE:V_NORM_FFN_PRE + 1, :]).astype(BF16)
    hb = h_s[...]

    def up_project(c):
        return [_dot(hb, w_up_ref[:, chunk_cols(c, part)]) for part in range(2)]

    u_next = up_project(0)
    for c in range(n_chunks):
        u_cur = u_next
        if c + 1 < n_chunks:
            u_next = up_project(c + 1)
        conv = []
        for part in range(2):
            cols = chunk_cols(c, part)
            u = u_cur[part]
            bias = jnp.broadcast_to(fcv_ref[CONV_FF:CONV_FF + 1, cols], (SUBLANES, PROMPT_FF_CHUNK))
            taps = [jnp.broadcast_to(fcv_ref[j:j + 1, cols], (SUBLANES, PROMPT_FF_CHUNK)) for j in range(CONV_FF)]
            ys = []
            prev_tail = [tail_s[i * SUBLANES:(i + 1) * SUBLANES, cols] for i in range(ndelay)]
            for j in range(nblk):
                groups = _row_groups(u, j)
                delayed = _delayed_groups(groups, prev_tail, ndelay, sub)
                for k in range(SEG_LEN):
                    y = bias + taps[CONV_FF - 1] * groups[k]
                    for d in range(1, CONV_FF):
                        y = y + taps[CONV_FF - 1 - d] * delayed[d - 1][k]
                    ys.append(y)
                prev_tail = groups[SEG_LEN - ndelay:]
            conv.append(jnp.concatenate(ys, axis=0))
            tail_s[:, cols] = u[tc - ndelay * SUBLANES:tc]
            for i in range(ndelay):
                r = tc - (ndelay - i) * SUBLANES + SUBLANES - 1
                fstate_ref[i:i + 1, cols] = u[r:r + 1]
        act = (_gelu_tanh_doubled(conv[1]) * conv[0]).astype(BF16)
        part_f = _dot(act, w_down_ref[c * PROMPT_FF_CHUNK:(c + 1) * PROMPT_FF_CHUNK, :])
        if c == 0:
            acc_s[...] = part_f
        else:
            acc_s[...] += part_f
    y = x_ref[...] + _rmsnorm(acc_s[...], vec_ref[V_NORM_FFN_POST:V_NORM_FFN_POST + 1, :])
    if permute_out:
        for j in range(nblk):
            for k in range(SEG_LEN):
                y_ref[j * SUBLANES:(j + 1) * SUBLANES, k, :] = (
                    y[j * WINDOW + k * SUBLANES:j * WINDOW + (k + 1) * SUBLANES])
    else:
        y_ref[...] = y


def _prompt_ffn(x, vecs, fcv, w_up, w_down, layer, prev_states, permute_out):
    bsz, seq, _ = x.shape
    tc = PROMPT_TILE
    kern = functools.partial(_pffn_kernel, tc=tc, permute_out=permute_out)
    tile = pl.BlockSpec((None, tc, D_MODEL), lambda b, t: (b, t, 0))
    if permute_out:
        y_shape = (bsz, seq // SEG_LEN, SEG_LEN, D_MODEL)
        y_spec = pl.BlockSpec((None, tc // SEG_LEN, SEG_LEN, D_MODEL), lambda b, t: (b, t, 0, 0))
    else:
        y_shape, y_spec = (bsz, seq, D_MODEL), tile
    in_specs = [
        tile,
        _const_spec((None, VEC_ROWS, D_MODEL), (layer, 0, 0)),
        _const_spec((None, SUBLANES, 2 * D_FF), (layer, 0, 0)),
        _const_spec((D_MODEL, 2 * D_FF), (0, 0)),
        _const_spec((D_FF, D_MODEL), (0, 0)),
    ]
    args = [x, vecs, fcv, w_up, w_down]
    kern, in_specs, args, aliases = _with_state_aliases(kern, len(args), in_specs, args, prev_states, 1)
    return pl.pallas_call(
        kern,
        grid=(bsz, seq // tc),
        in_specs=in_specs,
        out_specs=(y_spec, pl.BlockSpec((None, None, CONV_FF - 1, 2 * D_FF), lambda b, t: (layer, b, 0, 0))),
        out_shape=(jax.ShapeDtypeStruct(y_shape, F32),
                   jax.ShapeDtypeStruct((DEPTH, bsz, CONV_FF - 1, 2 * D_FF), F32)),
        input_output_aliases=aliases,
        scratch_shapes=[
            pltpu.VMEM((tc, D_MODEL), BF16),
            pltpu.VMEM(((CONV_FF - 1) * SUBLANES, 2 * D_FF), F32),
            pltpu.VMEM((tc, D_MODEL), F32),
        ],
        compiler_params=pltpu.CompilerParams(
            dimension_semantics=("arbitrary", "arbitrary"), vmem_limit_bytes=VMEM_LIMIT_BYTES),
        name=f"prompt_ffn_l{layer}",
    )(*args)


def _cast_rows(src_ref, dst_ref, step=LRU_BLOCK):
    for r in range(0, src_ref.shape[0], step):
        dst_ref[r:r + step, :] = src_ref[r:r + step, :].astype(BF16)


def _smix_in_kernel(xs_ref, h0_ref, cprev_ref, vec_ref, w_in_f32_ref, wri_ref,
                    q_ref, k_ref, v_ref, lru_ref, gate_ref, w_in_ref, hlast_ref, cstate_ref,
                    h_s, xr_s, *, nb, nt):
    c = pl.program_id(0)
    n_steps = IN_COLS // PROJ_PIECE

    @pl.when(c == 0)
    def _():
        for t in range(nt):
            h_s[t * nb:(t + 1) * nb, :] = _rmsnorm(
                xs_ref[:, t, :], vec_ref[V_NORM_MIX_PRE:V_NORM_MIX_PRE + 1, :]).astype(BF16)

    _cast_rows(w_in_f32_ref, w_in_ref)
    z = _dot(h_s[...], w_in_ref[...])

    for i in range(n_steps):
        c0 = i * PROJ_PIECE

        @pl.when(c == i)
        def _(c0=c0):
            if c0 < C_Q:
                xr_s[:, c0:c0 + PROJ_PIECE] = z
            elif c0 < C_K:
                for t in range(nt):
                    q_ref[:, t * Q_COLS + c0 - C_Q:t * Q_COLS + c0 - C_Q + PROJ_PIECE] = (
                        z[t * nb:(t + 1) * nb] * (HEAD_DIM ** -0.5)).astype(BF16)
            elif c0 == C_K:
                for t in range(nt):
                    for g in range(N_KV):
                        k_ref[:, t * N_KV + g, :] = z[t * nb:(t + 1) * nb, g * HEAD_DIM:(g + 1) * HEAD_DIM]
                        v_ref[:, t * N_KV + g, :] = z[t * nb:(t + 1) * nb,
                                                      KV_COLS + g * HEAD_DIM:KV_COLS + (g + 1) * HEAD_DIM]
            else:
                gate_ref[:, c0 - C_G:c0 - C_G + PROJ_PIECE] = _sigmoid(z)

    @pl.when(c == n_steps - 1)
    def _():
        _smix_in_recurrence(h0_ref, cprev_ref, vec_ref, wri_ref, lru_ref, hlast_ref, cstate_ref, xr_s, nb, nt)


def _smix_in_recurrence(h0_ref, cprev_ref, vec_ref, wri_ref, lru_ref, hlast_ref, cstate_ref, xr_s, nb, nt):
    npre = CONV_LRU - 1
    for t in range(nt - npre, nt):
        cstate_ref[:, t - (nt - npre), :] = xr_s[t * nb:(t + 1) * nb, :]

    for n in range(N_LRU_BLOCKS):
        cb = slice(n * LRU_BLOCK, (n + 1) * LRU_BLOCK)
        xx = [cprev_ref[:, j, cb] for j in range(npre)]
        xx += [xr_s[t * nb:(t + 1) * nb, cb] for t in range(nt)]
        xcs = []
        for t in range(nt):
            xc = vec_ref[V_CONV_B:V_CONV_B + 1, cb]
            for j in range(CONV_LRU):
                xc = xc + vec_ref[V_CONV_W + j:V_CONV_W + j + 1, cb] * xx[t + j]
            xcs.append(xc)
        a, ix, mult = _lru_coeffs(jnp.concatenate(xcs, axis=0), vec_ref, wri_ref, n)
        b = mult * ix
        h = h0_ref[:, cb]
        for t in range(nt):
            rows = slice(t * nb, (t + 1) * nb)
            h = a[rows] * h + b[rows]
            lru_ref[rows, cb] = h.astype(BF16)
        hlast_ref[:, cb] = h


def _sample_mixer_in(xs, h0, cprev, vecs, w_in, wri, layer, prev_states):
    nb, nt, _ = xs.shape
    assert PAST_LEN > 0
    kern = functools.partial(_smix_in_kernel, nb=nb, nt=nt)
    whole = lambda shape: pl.BlockSpec(shape, lambda i: (0,) * len(shape))
    conv_state = pl.BlockSpec((None, nb, CONV_LRU - 1, D_RNN), lambda i: (layer, 0, 0, 0))
    in_specs = [
        whole((nb, nt, D_MODEL)),
        pl.BlockSpec((None, nb, D_RNN), lambda i: (layer, 0, 0)),
        conv_state,
        _const_spec((None, VEC_ROWS, D_MODEL), (layer, 0, 0)),
        pl.BlockSpec((None, D_MODEL, PROJ_PIECE), lambda i: (layer, 0, i)),
        _const_spec((None, N_LRU_BLOCKS, LRU_BLOCK, 2 * LRU_BLOCK), (layer, 0, 0, 0)),
    ]
    args = [xs, h0, cprev, vecs, w_in, wri]
    kern, in_specs, args, aliases = _with_state_aliases(kern, len(args), in_specs, args, prev_states, 6)
    return pl.pallas_call(
        kern,
        grid=(IN_COLS // PROJ_PIECE,),
        in_specs=in_specs,
        out_specs=(whole((nb, nt * Q_COLS)), whole((nb, nt * N_KV, HEAD_DIM)), whole((nb, nt * N_KV, HEAD_DIM)),
                   whole((nt * nb, D_RNN)), whole((nt * nb, 2 * D_MODEL)),
                   pl.BlockSpec((D_MODEL, PROJ_PIECE), lambda i: (0, i)),
                   pl.BlockSpec((None, nb, D_RNN), lambda i: (layer, 0, 0)), conv_state),
        out_shape=(jax.ShapeDtypeStruct((nb, nt * Q_COLS), BF16),
                   jax.ShapeDtypeStruct((nb, nt * N_KV, HEAD_DIM), F32),
                   jax.ShapeDtypeStruct((nb, nt * N_KV, HEAD_DIM), F32),
                   jax.ShapeDtypeStruct((nt * nb, D_RNN), BF16),
                   jax.ShapeDtypeStruct((nt * nb, 2 * D_MODEL), F32),
                   jax.ShapeDtypeStruct((D_MODEL, IN_COLS), BF16),
                   jax.ShapeDtypeStruct((DEPTH, nb, D_RNN), F32),
                   jax.ShapeDtypeStruct((DEPTH, nb, CONV_LRU - 1, D_RNN), F32)),
        input_output_aliases=aliases,
        scratch_shapes=[pltpu.VMEM((nt * nb, D_MODEL), BF16), pltpu.VMEM((nt * nb, D_RNN), F32)],
        compiler_params=pltpu.CompilerParams(
            dimension_semantics=("arbitrary",), vmem_limit_bytes=VMEM_LIMIT_BYTES),
        name=f"sample_mixer_in_l{layer}",
    )(*args)


def _sattn_kernel(q_ref, kn_ref, vn_ref, ck_ref, cv_ref, tab_ref, attn_ref, sk_ref, sv_ref,
                  kc_s, vc_s, *, bt, nt):
    n_cache = WINDOW * N_KV
    n_new = nt * N_KV
    for u in range(SAMPLE_ATTN_UNROLL):
        kc_s[u, n_cache + n_new:SAMPLE_KEYS, :] = jnp.zeros((SAMPLE_KEYS - n_cache - n_new, HEAD_DIM), F32)
        vc_s[u, n_cache + n_new:SAMPLE_KEYS, :] = jnp.zeros((SAMPLE_KEYS - n_cache - n_new, HEAD_DIM), F32)

    def body(i, _):
        seqs = [i * SAMPLE_ATTN_UNROLL + u for u in range(SAMPLE_ATTN_UNROLL)]
        scores = []
        for u, b in enumerate(seqs):
            for c_ref, n_ref, s_ref, scr in ((ck_ref, kn_ref, sk_ref, kc_s.at[u]), (cv_ref, vn_ref, sv_ref, vc_s.at[u])):
                scr[0:n_cache, :] = c_ref[b]
                scr[n_cache:n_cache + n_new, :] = n_ref[b]
                s_ref[b, 0:n_cache - n_new, :] = c_ref[b, n_new:n_cache, :]
                s_ref[b, n_cache - n_new:n_cache, :] = n_ref[b]
            scores.append(_dot_nt(q_ref[b], kc_s[u].astype(BF16)))
        probs = []
        for s in scores:
            s = s + tab_ref[...]
            p = jnp.exp(s - jnp.max(s, axis=-1, keepdims=True))
            probs.append((p.astype(BF16), 1.0 / jnp.sum(p, axis=-1, keepdims=True)))
        for u, b in enumerate(seqs):
            p, inv = probs[u]
            attn_ref[b] = (_dot(p, vc_s[u].astype(BF16)) * inv).astype(BF16)
        return 0

    lax.fori_loop(0, bt // SAMPLE_ATTN_UNROLL, body, 0)


def _sample_attention(q, kn, vn, cache_k, cache_v, stab, layer, prev_states):
    nb, rows, _ = q.shape
    nt = rows // N_HEADS
    bt = SAMPLE_BATCH_TILE
    kern = functools.partial(_sattn_kernel, bt=bt, nt=nt)
    cache_spec = pl.BlockSpec((None, bt, WINDOW * N_KV, HEAD_DIM), lambda i: (layer, i, 0, 0))
    new_spec = pl.BlockSpec((bt, nt * N_KV, HEAD_DIM), lambda i: (i, 0, 0))
    q_spec = pl.BlockSpec((bt, rows, HEAD_DIM), lambda i: (i, 0, 0))
    in_specs = [q_spec, new_spec, new_spec, cache_spec, cache_spec,
                _const_spec((None, rows, SAMPLE_KEYS), (layer, 0, 0))]
    args = [q, kn, vn, cache_k, cache_v, stab]
    kern, in_specs, args, aliases = _with_state_aliases(kern, len(args), in_specs, args, prev_states, 1)
    return pl.pallas_call(
        kern,
        grid=(nb // bt,),
        in_specs=in_specs,
        out_specs=(q_spec, cache_spec, cache_spec),
        out_shape=(jax.ShapeDtypeStruct((nb, rows, HEAD_DIM), BF16),
                   jax.ShapeDtypeStruct((DEPTH, nb, WINDOW * N_KV, HEAD_DIM), F32),
                   jax.ShapeDtypeStruct((DEPTH, nb, WINDOW * N_KV, HEAD_DIM), F32)),
        input_output_aliases=aliases,
        scratch_shapes=[pltpu.VMEM((SAMPLE_ATTN_UNROLL, SAMPLE_KEYS, HEAD_DIM), F32),
                        pltpu.VMEM((SAMPLE_ATTN_UNROLL, SAMPLE_KEYS, HEAD_DIM), F32)],
        compiler_params=pltpu.CompilerParams(
            dimension_semantics=("arbitrary",), vmem_limit_bytes=VMEM_LIMIT_BYTES),
        name=f"sample_attention_l{layer}",
    )(*args)


def _smix_out_kernel(xs_ref, lru_ref, attn_ref, gate_ref, vec_ref, w_lo_f32_ref, w_ao_f32_ref, w_out_f32_ref,
                     xmid_ref, w_lo_ref, w_ao_ref, w_out_ref, a_s, *, nb, nt):
    _cast_rows(w_lo_f32_ref, w_lo_ref)
    _cast_rows(w_ao_f32_ref, w_ao_ref)
    _cast_rows(w_out_f32_ref, w_out_ref)
    for t in range(nt):
        a_s[t * nb:(t + 1) * nb, :] = attn_ref[:, t * Q_COLS:(t + 1) * Q_COLS]
    merged = (gate_ref[:, 0:D_MODEL] * _dot(lru_ref[...], w_lo_ref[...])
              + gate_ref[:, D_MODEL:2 * D_MODEL] * _dot(a_s[...], w_ao_ref[...]))
    m = _dot(merged.astype(BF16), w_out_ref[...])
    mn = _rmsnorm(m, vec_ref[V_NORM_MIX_POST:V_NORM_MIX_POST + 1, :])
    for t in range(nt):
        rows = slice(t * nb, (t + 1) * nb)
        xmid_ref[rows, :] = xs_ref[:, t, :] + mn[rows]


def _sample_mixer_out(xs, lru, attn, gates, vecs, w_lo, w_ao, w_out, layer):
    nb, nt, _ = xs.shape
    kern = functools.partial(_smix_out_kernel, nb=nb, nt=nt)
    whole = lambda shape: pl.BlockSpec(shape, lambda i: (0,) * len(shape))
    return pl.pallas_call(
        kern,
        grid=(1,),
        in_specs=[
            whole(xs.shape), whole(lru.shape), whole(attn.shape), whole(gates.shape),
            _const_spec((None, VEC_ROWS, D_MODEL), (layer, 0, 0)),
            _const_spec((None, D_RNN, D_MODEL), (layer, 0, 0)),
            _const_spec((None, Q_COLS, D_MODEL), (layer, 0, 0)),
            _const_spec((None, D_MODEL, D_MODEL), (layer, 0, 0)),
        ],
        out_specs=(whole((nt * nb, D_MODEL)),) + (whole((D_MODEL, D_MODEL)),) * 3,
        out_shape=(jax.ShapeDtypeStruct((nt * nb, D_MODEL), F32),)
        + (jax.ShapeDtypeStruct((D_MODEL, D_MODEL), BF16),) * 3,
        scratch_shapes=[pltpu.VMEM((nt * nb, Q_COLS), BF16)],
        compiler_params=pltpu.CompilerParams(
            dimension_semantics=("arbitrary",), vmem_limit_bytes=VMEM_LIMIT_BYTES),
        name=f"sample_mixer_out_l{layer}",
    )(xs, lru, attn, gates, vecs, w_lo, w_ao, w_out)


def _sffn_kernel(x_ref, vec_ref, fcv_ref, prev_ref, w_f32_ref, wd_f32_ref, ys_ref, w_ref, wd_ref, fstate_ref,
                 h_s, val_s, acc_s, *, nb, nt, nck):
    c = pl.program_id(0)

    @pl.when(c == 0)
    def _():
        h_s[...] = _rmsnorm(x_ref[...], vec_ref[V_NORM_FFN_PRE:V_NORM_FFN_PRE + 1, :]).astype(BF16)
        acc_s[...] = jnp.zeros(acc_s.shape, F32)

    _cast_rows(w_f32_ref, w_ref)
    u = _dot(h_s[...], w_ref[...])
    uu = [prev_ref[:, j, :] for j in range(CONV_FF - 1)] + [u[t * nb:(t + 1) * nb] for t in range(nt)]
    ys = []
    for t in range(nt):
        y = fcv_ref[CONV_FF:CONV_FF + 1, :]
        for j in range(CONV_FF):
            y = y + fcv_ref[j:j + 1, :] * uu[t + j]
        ys.append(y)
    conv = jnp.concatenate(ys, axis=0)
    for j in range(CONV_FF - 1):
        fstate_ref[:, j, :] = uu[nt + j]

    @pl.when(c < nck)
    def _():
        val_s[c] = conv

    @pl.when(c >= nck)
    def _():
        act = (_gelu_tanh_doubled(conv) * val_s[c - nck]).astype(BF16)
        _cast_rows(wd_f32_ref, wd_ref)
        acc_s[...] += _dot(act, wd_ref[...])

    @pl.when(c == 2 * nck - 1)
    def _():
        y = x_ref[...] + _rmsnorm(acc_s[...], vec_ref[V_NORM_FFN_POST:V_NORM_FFN_POST + 1, :])
        for t in range(nt):
            ys_ref[:, t, :] = y[t * nb:(t + 1) * nb]


def _sample_ffn(xmid, fprev, vecs, fcv, w_up, w_down, layer, nb, prev_states):
    nt = xmid.shape[0] // nb
    ck = FF_CHUNK
    nck = D_FF // ck
    kern = functools.partial(_sffn_kernel, nb=nb, nt=nt, nck=nck)
    state_spec = pl.BlockSpec((None, nb, CONV_FF - 1, ck), lambda c: (layer, 0, 0, c))
    in_specs = [
        _const_spec((nt * nb, D_MODEL), (0, 0)),
        _const_spec((None, VEC_ROWS, D_MODEL), (layer, 0, 0)),
        pl.BlockSpec((None, SUBLANES, ck), lambda c: (layer, 0, c)),
        state_spec,
        pl.BlockSpec((None, D_MODEL, ck), lambda c: (layer, 0, c)),
        pl.BlockSpec((None, ck, D_MODEL), lambda c: (layer, jnp.maximum(c - nck, 0), 0)),
    ]
    args = [xmid, vecs, fcv, fprev, w_up, w_down]
    kern, in_specs, args, aliases = _with_state_aliases(kern, len(args), in_specs, args, prev_states, 3)
    return pl.pallas_call(
        kern,
        grid=(2 * nck,),
        in_specs=in_specs,
        out_specs=(pl.BlockSpec((nb, nt, D_MODEL), lambda c: (0, 0, 0)),
                   pl.BlockSpec((D_MODEL, ck), lambda c: (0, c)),
                   pl.BlockSpec((ck, D_MODEL), lambda c: (jnp.maximum(c - nck, 0), 0)),
                   state_spec),
        out_shape=(jax.ShapeDtypeStruct((nb, nt, D_MODEL), F32),
                   jax.ShapeDtypeStruct((D_MODEL, 2 * D_FF), BF16),
                   jax.ShapeDtypeStruct((D_FF, D_MODEL), BF16),
                   jax.ShapeDtypeStruct((DEPTH, nb, CONV_FF - 1, 2 * D_FF), F32)),
        input_output_aliases=aliases,
        scratch_shapes=[pltpu.VMEM((nt * nb, D_MODEL), BF16),
                        pltpu.VMEM((nck, nt * nb, ck), F32),
                        pltpu.VMEM((nt * nb, D_MODEL), F32)],
        compiler_params=pltpu.CompilerParams(
            dimension_semantics=("arbitrary",), vmem_limit_bytes=VMEM_LIMIT_BYTES),
        name=f"sample_ffn_l{layer}",
    )(*args)


def kernel(x_prompt, x_sample, state_lru_h, state_lru_conv, cache_win_k, cache_win_v, state_ffn_conv,
           norm_mix_pre, norm_mix_post, norm_ffn_pre, norm_ffn_post, w_in, conv_lru_w, conv_lru_b,
           lru_wr, lru_br, lru_wi, lru_bi, lru_lambda, w_lru_o, w_attn_o, w_out, attn_sink, rel_bias,
           w_up, ffn_conv_w, ffn_conv_b, w_down):
    nb, nt, _ = x_sample.shape
    bp = x_prompt.shape[0]

    row = lambda v: v[:, None, :]
    vecs = jnp.concatenate(
        [row(norm_mix_pre), row(norm_mix_post), row(conv_lru_b), row(lru_br), row(lru_bi), row(lru_lambda),
         conv_lru_w, row(norm_ffn_pre), row(norm_ffn_post),
         jnp.zeros((DEPTH, VEC_ROWS - 12, D_MODEL), F32)], axis=1)
    fcv = jnp.concatenate(
        [ffn_conv_w, row(ffn_conv_b), jnp.zeros((DEPTH, SUBLANES - CONV_FF - 1, 2 * D_FF), F32)], axis=1)
    fcv = fcv * jnp.where(jnp.arange(2 * D_FF) < D_FF, 0.5, 1.0).astype(F32)
    wri_b = jnp.concatenate([lru_wr, lru_wi], axis=-1).astype(BF16)

    ptab, stab = _bias_tables(rel_bias, attn_sink)

    yp, xs = x_prompt, x_sample
    p_mix = p_ffn = s_mix = s_att = s_ffn = None
    cache_rows = lambda c: c.reshape(DEPTH, nb, WINDOW * N_KV, HEAD_DIM)
    for l in range(DEPTH):
        q, kn, vn, lru, gates, w_in_b, *s_mix = _sample_mixer_in(
            xs, state_lru_h, state_lru_conv, vecs, w_in, wri_b, l, s_mix)
        attn, *s_att = _sample_attention(
            q.reshape(nb, nt * N_HEADS, HEAD_DIM), kn, vn, cache_rows(cache_win_k), cache_rows(cache_win_v),
            stab, l, s_att)
        xmid, w_lo_b, w_ao_b, w_out_b = _sample_mixer_out(
            xs, lru, attn.reshape(nb, nt * Q_COLS), gates, vecs, w_lru_o, w_attn_o, w_out, l)
        xs, w_up_b, w_down_b, *s_ffn = _sample_ffn(xmid, state_ffn_conv, vecs, fcv, w_up, w_down, l, nb, s_ffn)

        yp, *p_mix = _prompt_mixer(yp, vecs, w_in_b, wri_b, w_lo_b, w_ao_b, w_out_b, ptab, attn_sink, l, p_mix,
                                   permute_in=(l == 0))
        yp, *p_ffn = _prompt_ffn(yp, vecs, fcv, w_up_b, w_down_b, l, p_ffn, permute_out=(l == DEPTH - 1))

    p_h, p_c, p_k, p_v = p_mix
    s_h, s_c = s_mix
    s_k, s_v = s_att
    kv_shape = (DEPTH, bp, WINDOW, N_KV, HEAD_DIM)
    return (yp.reshape(x_prompt.shape), xs, p_h, p_c, p_k.reshape(kv_shape), p_v.reshape(kv_shape), p_ffn[0],
            s_h, s_c, s_k.reshape(cache_win_k.shape), s_v.reshape(cache_win_v.shape), s_ffn[0])
```

```python
import functools
import math

import numpy as np
import jax
import jax.numpy as jnp
from jax import lax
from jax.experimental import pallas as pl
from jax.experimental.pallas import tpu as pltpu

D_MODEL = 1024
DEPTH = 2
PAST_LEN = 16384
D_RNN = D_MODEL
N_LRU_BLOCKS = 8
LRU_BLOCK = D_RNN // N_LRU_BLOCKS
CONV_LRU = 4
LRU_C = 8.0
N_HEADS = 8
N_KV = 2
GROUP = N_HEADS // N_KV
HEAD_DIM = D_MODEL // N_HEADS
WINDOW = 128
N_BUCKETS = 32
MAX_EXACT = N_BUCKETS // 2
MAX_DISTANCE = 128
D_FF = 4 * D_MODEL
CONV_FF = 3
EPS = 1e-6
Q_COLS = N_HEADS * HEAD_DIM
KV_COLS = N_KV * HEAD_DIM
IN_COLS = D_RNN + Q_COLS + 2 * KV_COLS + 2 * D_MODEL
C_Q = D_RNN
C_K = C_Q + Q_COLS
C_V = C_K + KV_COLS
C_G = C_V + KV_COLS

F32 = jnp.float32
BF16 = jnp.bfloat16

SUBLANES = 8
LANES = 128
VMEM_LIMIT_BYTES = 56 * 1024 * 1024

V_NORM_MIX_PRE, V_NORM_MIX_POST, V_CONV_B, V_BR, V_BI, V_LAMBDA, V_CONV_W = 0, 1, 2, 3, 4, 5, 6
V_NORM_FFN_PRE, V_NORM_FFN_POST = 10, 11
VEC_ROWS = 16

PROMPT_TILE = 512
PROJ_PIECE = 2 * KV_COLS
assert (C_K - C_Q) % PROJ_PIECE == 0 and (IN_COLS - C_G) % PROJ_PIECE == 0
PROMPT_FF_CHUNK = 2048
FF_CHUNK = 1024
SAMPLE_KEYS = N_KV * (WINDOW + 4) + SUBLANES
SINK_COL = N_KV * (WINDOW + 4)
SAMPLE_BATCH_TILE = 16
SAMPLE_ATTN_UNROLL = 8
SEG_LEN = WINDOW // SUBLANES


def _bucket_thresholds():
    d = np.arange(0, 2 * WINDOW)
    nf = np.maximum(d, 1).astype(np.float64)
    large = MAX_EXACT + (np.log(nf / MAX_EXACT) / math.log(MAX_DISTANCE / MAX_EXACT)
                         * (N_BUCKETS - MAX_EXACT)).astype(np.int64)
    bucket = np.where(d < MAX_EXACT, d, np.minimum(large, N_BUCKETS - 1))
    return tuple(int(d[bucket >= b].min()) for b in range(1, N_BUCKETS))


_BUCKET_THRESHOLDS = _bucket_thresholds()


def _dot(a, b):
    return jnp.dot(a, b, preferred_element_type=F32)


def _dot_nt(a, b):
    return lax.dot_general(a, b, (((1,), (1,)), ((), ())), preferred_element_type=F32)


def _rmsnorm(x, g):
    return x * lax.rsqrt(jnp.mean(x * x, axis=-1, keepdims=True) + EPS) * g


def _sigmoid(x):
    return 1.0 / (1.0 + jnp.exp(-x))


def _gelu_tanh_doubled(x):
    c = math.sqrt(2.0 / math.pi)
    t = jnp.tanh(x * (c + (c * 0.044715) * (x * x)))
    return x + x * t


def _const_spec(block_shape, index):
    return pl.BlockSpec(block_shape, lambda *_: index, pipeline_mode=pl.Buffered(1))


def _with_state_aliases(kern, n_in, in_specs, args, prev_states, first_state_out):
    if prev_states is None:
        return kern, list(in_specs), list(args), {}
    n = len(prev_states)

    def body(*refs):
        return kern(*refs[:n_in], *refs[n_in + n:])

    return (body, list(in_specs) + [pl.BlockSpec(memory_space=pl.ANY)] * n, list(args) + list(prev_states),
            {n_in + i: first_state_out + i for i in range(n)})


def _bucket_of(d):
    n = jnp.maximum(d, 0)
    bucket = jnp.zeros(d.shape, jnp.int32)
    for thr in _BUCKET_THRESHOLDS:
        bucket = bucket + jnp.where(n >= thr, 1, 0)
    return bucket


def _block_time(p):
    return lax.bitwise_and(p, SUBLANES - 1) * SEG_LEN + lax.shift_right_logical(p, 3)


def _table_kernel(rel_ref, sink_ref, pt_ref, st_ref):
    qi = lax.broadcasted_iota(jnp.int32, (WINDOW, 2 * WINDOW), 0)
    kj = lax.broadcasted_iota(jnp.int32, (WINDOW, 2 * WINDOW), 1)
    kpos = lax.bitwise_and(kj, WINDOW - 1)
    d = _block_time(qi) + WINDOW - (_block_time(kpos) + (kj - kpos))
    bucket = _bucket_of(d)
    in_band = jnp.where(d >= 0, jnp.where(d < WINDOW, 1, 0), 0)
    cur_only = jnp.where(kj >= WINDOW, in_band, 0)
    for h in range(N_HEADS):
        val = jnp.zeros(d.shape, F32)
        for b in range(N_BUCKETS):
            val = jnp.where(bucket == b, rel_ref[b, h], val)
        pt_ref[1, h] = jnp.where(in_band == 1, val, -jnp.inf)
        pt_ref[0, h] = jnp.where(cur_only == 1, val, -jnp.inf)

    r = lax.broadcasted_iota(jnp.int32, (4 * N_HEADS, SAMPLE_KEYS), 0)
    j = lax.broadcasted_iota(jnp.int32, (4 * N_HEADS, SAMPLE_KEYS), 1)
    t = lax.shift_right_logical(r, 3)
    hh = lax.bitwise_and(r, N_HEADS - 1)
    d = t + WINDOW - lax.shift_right_logical(j, 1)
    bucket = _bucket_of(d)
    in_band = jnp.where(d >= 0, jnp.where(d < WINDOW, 1, 0), 0)
    in_band = jnp.where(lax.bitwise_and(j, N_KV - 1) == lax.shift_right_logical(hh, 2), in_band, 0)
    val = jnp.zeros(d.shape, F32)
    for h in range(N_HEADS):
        hval = jnp.zeros(d.shape, F32)
        for b in range(N_BUCKETS):
            hval = jnp.where(bucket == b, rel_ref[b, h], hval)
        val = jnp.where(hh == h, hval, val)
    val = jnp.where(in_band == 1, val, -jnp.inf)
    for l in range(DEPTH):
        sk = jnp.zeros(d.shape, F32)
        for h in range(N_HEADS):
            sk = jnp.where(hh == h, sink_ref[l, h], sk)
        st_ref[l] = jnp.where(j == SINK_COL, sk, val)


def _bias_tables(rel_bias, attn_sink):
    smem = pl.BlockSpec(memory_space=pltpu.SMEM)
    return pl.pallas_call(
        _table_kernel,
        out_shape=(jax.ShapeDtypeStruct((2, N_HEADS, WINDOW, 2 * WINDOW), F32),
                   jax.ShapeDtypeStruct((DEPTH, 4 * N_HEADS, SAMPLE_KEYS), F32)),
        in_specs=[smem, smem],
        name="bias_tables",
    )(rel_bias, attn_sink)


def _lru_coeffs(xc, vec_ref, wri_ref, n):
    cb = slice(n * LRU_BLOCK, (n + 1) * LRU_BLOCK)
    rw = _dot(xc.astype(BF16), wri_ref[n])
    r = _sigmoid(rw[:, :LRU_BLOCK] + vec_ref[V_BR:V_BR + 1, cb])
    i = _sigmoid(rw[:, LRU_BLOCK:] + vec_ref[V_BI:V_BI + 1, cb])
    z = -vec_ref[V_LAMBDA:V_LAMBDA + 1, cb]
    softplus = jnp.maximum(z, 0.0) + jnp.log1p(jnp.exp(-jnp.abs(z)))
    log_a = (-LRU_C * softplus) * r
    a = jnp.exp(log_a)
    mult = jnp.sqrt(jnp.maximum(1.0 - a * a, 0.0))
    return a, i * xc, mult


def _delayed_groups(groups, prev_tail, ndelay, sub):
    ng = len(groups)
    wrapped = {}
    for i in range(ndelay):
        k = ng - ndelay + i
        wrapped[k] = jnp.where(sub == 0, pltpu.roll(prev_tail[i], 1, 0), pltpu.roll(groups[k], 1, 0))
    return [[groups[k - d] if k >= d else wrapped[ng + k - d] for k in range(ng)] for d in range(1, ndelay + 1)]


def _row_groups(x, block):
    return [x[block * WINDOW + k * SUBLANES:block * WINDOW + (k + 1) * SUBLANES] for k in range(SEG_LEN)]


def _pmix_kernel(x_ref, vec_ref, w_in_ref, wri_ref, w_lo_ref, w_ao_ref, w_out_ref, tab_ref, sink_ref,
                 y_ref, hlast_ref, cstate_ref, kstate_ref, vstate_ref,
                 xp_s, h_s, xr_s, xrc_s, q_s, k_s, v_s, kvf_s, g_s, lru_s, attn_s, hc_s, hl_s, *, layer, tc, permute_in):
    t = pl.program_id(1)
    nblk = tc // WINDOW
    ndelay = CONV_LRU - 1
    n_pieces = (IN_COLS - C_Q) // PROJ_PIECE
    batch_row = lax.broadcasted_iota(jnp.int32, (hl_s.shape[0], LRU_BLOCK), 0)

    @pl.when(jnp.logical_and(t == 0, pl.program_id(0) == 0))
    def _():
        hl_s[...] = jnp.zeros(hl_s.shape, F32)

    @pl.when(t == 0)
    def _():
        xrc_s[...] = jnp.zeros(xrc_s.shape, F32)
        k_s[0:WINDOW, :] = jnp.zeros((WINDOW, KV_COLS), BF16)
        v_s[0:WINDOW, :] = jnp.zeros((WINDOW, KV_COLS), BF16)
        hc_s[...] = jnp.zeros(hc_s.shape, F32)

    if permute_in:
        for j in range(nblk):
            for k in range(SEG_LEN):
                xp_s[j * WINDOW + k * SUBLANES:j * WINDOW + (k + 1) * SUBLANES, :] = (
                    x_ref[j * SUBLANES:(j + 1) * SUBLANES, k, :])
        x_tile = xp_s
    else:
        x_tile = x_ref

    h_s[...] = _rmsnorm(x_tile[...], vec_ref[V_NORM_MIX_PRE:V_NORM_MIX_PRE + 1, :]).astype(BF16)
    hb = h_s[...]
    xr_s[...] = _dot(hb, w_in_ref[:, 0:C_Q])

    def project_piece(i):
        c0 = C_Q + i * PROJ_PIECE
        z = _dot(hb, w_in_ref[:, c0:c0 + PROJ_PIECE])
        if c0 < C_K:
            q_s[:, c0 - C_Q:c0 - C_Q + PROJ_PIECE] = (z * (HEAD_DIM ** -0.5)).astype(BF16)
        elif c0 == C_K:
            k_s[WINDOW:WINDOW + tc, :] = z[:, 0:KV_COLS].astype(BF16)
            v_s[WINDOW:WINDOW + tc, :] = z[:, KV_COLS:2 * KV_COLS].astype(BF16)
            kvf_s[0] = z[tc - WINDOW:tc, 0:KV_COLS]
            kvf_s[1] = z[tc - WINDOW:tc, KV_COLS:2 * KV_COLS]
        else:
            g_s[:, c0 - C_G:c0 - C_G + PROJ_PIECE] = _sigmoid(z)

    for i in range(ndelay):
        r = tc - (ndelay - i) * SUBLANES + SUBLANES - 1
        cstate_ref[i:i + 1, :] = xr_s[r:r + 1, :]

    sub = lax.broadcasted_iota(jnp.int32, (SUBLANES, LRU_BLOCK), 0)
    seq_start = (sub + t) == 0
    for n in range(N_LRU_BLOCKS):
        cb = slice(n * LRU_BLOCK, (n + 1) * LRU_BLOCK)
        bias = jnp.broadcast_to(vec_ref[V_CONV_B:V_CONV_B + 1, cb], (SUBLANES, LRU_BLOCK))
        taps = [jnp.broadcast_to(vec_ref[V_CONV_W + j:V_CONV_W + j + 1, cb], (SUBLANES, LRU_BLOCK))
                for j in range(CONV_LRU)]
        xcs = []
        prev_tail = [xrc_s[i * SUBLANES:(i + 1) * SUBLANES, cb] for i in range(ndelay)]
        for j in range(nblk):
            groups = [xr_s[j * WINDOW + k * SUBLANES:j * WINDOW + (k + 1) * SUBLANES, cb] for k in range(SEG_LEN)]
            delayed = _delayed_groups(groups, prev_tail, ndelay, sub)
            for k in range(SEG_LEN):
                xc = bias + taps[CONV_LRU - 1] * groups[k]
                for d in range(1, CONV_LRU):
                    xc = xc + taps[CONV_LRU - 1 - d] * delayed[d - 1][k]
                xcs.append(xc)
            prev_tail = groups[SEG_LEN - ndelay:]
        a, ix, mult = _lru_coeffs(jnp.concatenate(xcs, axis=0), vec_ref, wri_ref, n)
        b = mult * ix

        carry = hc_s[0:1, cb]
        hs = []
        for j in range(nblk):
            ag, bg, ig = _row_groups(a, j), _row_groups(b, j), _row_groups(ix, j)
            if j == 0:
                bg[0] = jnp.where(seq_start, ig[0], bg[0])
            acc_a, acc_b = [ag[0]], [bg[0]]
            for k in range(1, SEG_LEN):
                acc_b.append(ag[k] * acc_b[-1] + bg[k])
                acc_a.append(ag[k] * acc_a[-1])
            seg_a, seg_b = acc_a[-1], acc_b[-1]
            for s in (1, 2, 4):
                ash = jnp.where(sub >= s, pltpu.roll(seg_a, s, 0), 1.0)
                bsh = jnp.where(sub >= s, pltpu.roll(seg_b, s, 0), 0.0)
                seg_b = seg_a * bsh + seg_b
                seg_a = seg_a * ash
            h_end = seg_a * carry + seg_b
            h_in = jnp.where(sub == 0, carry, pltpu.roll(h_end, 1, 0))
            hs += [acc_a[k] * h_in + acc_b[k] for k in range(SEG_LEN)]
            carry = h_end[SUBLANES - 1:SUBLANES, :]
        lru_s[:, cb] = jnp.concatenate(hs, axis=0).astype(BF16)
        hc_s[0:1, cb] = carry
        hl_s[:, cb] = jnp.where(batch_row == pl.program_id(0), carry, hl_s[:, cb])
        if n < n_pieces:
            project_piece(n)
    for i in range(N_LRU_BLOCKS, n_pieces):
        project_piece(i)
    hlast_ref[...] = hl_s[...]

    first = jnp.where(t == 0, 0, 1)

    def scores(j, g):
        q4 = jnp.concatenate(
            [q_s[j * WINDOW:(j + 1) * WINDOW, (g * GROUP + hg) * HEAD_DIM:(g * GROUP + hg + 1) * HEAD_DIM]
             for hg in range(GROUP)], axis=0)
        return _dot_nt(q4, k_s[j * WINDOW:(j + 2) * WINDOW, g * HEAD_DIM:(g + 1) * HEAD_DIM])

    def attend(j, g, s):
        variant = first if j == 0 else 1
        ps, invs = [], []
        for hg in range(GROUP):
            head = g * GROUP + hg
            sh = s[hg * WINDOW:(hg + 1) * WINDOW] + tab_ref[variant, head]
            sk = sink_ref[layer, head]
            m = jnp.maximum(jnp.max(sh, axis=-1, keepdims=True), sk)
            p = jnp.exp(sh - m)
            den = jnp.sum(p, axis=-1, keepdims=True) + jnp.exp(sk - m)
            ps.append(p.astype(BF16))
            invs.append(1.0 / den)
        o4 = _dot(jnp.concatenate(ps, axis=0),
                  v_s[j * WINDOW:(j + 2) * WINDOW, g * HEAD_DIM:(g + 1) * HEAD_DIM])
        for hg in range(GROUP):
            head = g * GROUP + hg
            attn_s[j * WINDOW:(j + 1) * WINDOW, head * HEAD_DIM:(head + 1) * HEAD_DIM] = (
                o4[hg * WINDOW:(hg + 1) * WINDOW] * invs[hg]).astype(BF16)

    pairs = [(j, g) for j in range(nblk) for g in range(N_KV)]
    s_next = scores(*pairs[0])
    for i, (j, g) in enumerate(pairs):
        s_cur = s_next
        if i + 1 < len(pairs):
            s_next = scores(*pairs[i + 1])
        attend(j, g, s_cur)

    xrc_s[...] = xr_s[tc - ndelay * SUBLANES:tc, :]
    k_s[0:WINDOW, :] = k_s[tc:tc + WINDOW, :]
    v_s[0:WINDOW, :] = v_s[tc:tc + WINDOW, :]

    merged = (g_s[:, 0:D_MODEL] * _dot(lru_s[...], w_lo_ref[...])
              + g_s[:, D_MODEL:2 * D_MODEL] * _dot(attn_s[...], w_ao_ref[...]))
    m = _dot(merged.astype(BF16), w_out_ref[...])
    y_ref[...] = x_tile[...] + _rmsnorm(m, vec_ref[V_NORM_MIX_POST:V_NORM_MIX_POST + 1, :])

    @pl.when(t == pl.num_programs(1) - 1)
    def _():
        for k in range(SEG_LEN):
            rows = slice(k * SUBLANES, (k + 1) * SUBLANES)
            for g in range(N_KV):
                kstate_ref[:, k, g, :] = kvf_s[0, rows, g * HEAD_DIM:(g + 1) * HEAD_DIM]
                vstate_ref[:, k, g, :] = kvf_s[1, rows, g * HEAD_DIM:(g + 1) * HEAD_DIM]


def _prompt_mixer(x, vecs, w_in, wri, w_lo, w_ao, w_out, tab, sink, layer, prev_states, permute_in):
    bsz, seq, _ = x.shape
    tc = PROMPT_TILE
    kern = functools.partial(_pmix_kernel, layer=layer, tc=tc, permute_in=permute_in)
    tile = pl.BlockSpec((None, tc, D_MODEL), lambda b, t: (b, t, 0))
    kv_state = pl.BlockSpec((None, None, SUBLANES, SEG_LEN, N_KV, HEAD_DIM), lambda b, t: (layer, b, 0, 0, 0, 0))
    if permute_in:
        x = x.reshape(bsz, seq // SEG_LEN, SEG_LEN, D_MODEL)
        x_spec = pl.BlockSpec((None, tc // SEG_LEN, SEG_LEN, D_MODEL), lambda b, t: (b, t, 0, 0))
    else:
        x_spec = tile
    in_specs = [
        x_spec,
        _const_spec((None, VEC_ROWS, D_MODEL), (layer, 0, 0)),
        _const_spec((D_MODEL, IN_COLS), (0, 0)),
        _const_spec((None, N_LRU_BLOCKS, LRU_BLOCK, 2 * LRU_BLOCK), (layer, 0, 0, 0)),
        _const_spec((D_RNN, D_MODEL), (0, 0)),
        _const_spec((Q_COLS, D_MODEL), (0, 0)),
        _const_spec((D_MODEL, D_MODEL), (0, 0)),
        _const_spec((2, N_HEADS, WINDOW, 2 * WINDOW), (0, 0, 0, 0)),
        pl.BlockSpec(memory_space=pltpu.SMEM),
    ]
    args = [x, vecs, w_in, wri, w_lo, w_ao, w_out, tab, sink]
    kern, in_specs, args, aliases = _with_state_aliases(kern, len(args), in_specs, args, prev_states, 1)
    return pl.pallas_call(
        kern,
        grid=(bsz, seq // tc),
        in_specs=in_specs,
        out_specs=(tile,
                   pl.BlockSpec((None, bsz, D_RNN), lambda b, t: (layer, 0, 0)),
                   pl.BlockSpec((None, None, CONV_LRU - 1, D_RNN), lambda b, t: (layer, b, 0, 0)),
                   kv_state, kv_state),
        out_shape=(jax.ShapeDtypeStruct((bsz, seq, D_MODEL), F32),
                   jax.ShapeDtypeStruct((DEPTH, bsz, D_RNN), F32),
                   jax.ShapeDtypeStruct((DEPTH, bsz, CONV_LRU - 1, D_RNN), F32),
                   jax.ShapeDtypeStruct((DEPTH, bsz, SUBLANES, SEG_LEN, N_KV, HEAD_DIM), F32),
                   jax.ShapeDtypeStruct((DEPTH, bsz, SUBLANES, SEG_LEN, N_KV, HEAD_DIM), F32)),
        input_output_aliases=aliases,
        scratch_shapes=[
            pltpu.VMEM((tc if permute_in else SUBLANES, D_MODEL), F32),
            pltpu.VMEM((tc, D_MODEL), BF16),
            pltpu.VMEM((tc, D_RNN), F32),
            pltpu.VMEM(((CONV_LRU - 1) * SUBLANES, D_RNN), F32),
            pltpu.VMEM((tc, Q_COLS), BF16),
            pltpu.VMEM((WINDOW + tc, KV_COLS), BF16),
            pltpu.VMEM((WINDOW + tc, KV_COLS), BF16),
            pltpu.VMEM((2, WINDOW, KV_COLS), F32),
            pltpu.VMEM((tc, 2 * D_MODEL), F32),
            pltpu.VMEM((tc, D_RNN), BF16),
            pltpu.VMEM((tc, Q_COLS), BF16),
            pltpu.VMEM((SUBLANES, D_RNN), F32),
            pltpu.VMEM((bsz, D_RNN), F32),
        ],
        compiler_params=pltpu.CompilerParams(
            dimension_semantics=("arbitrary", "arbitrary"), vmem_limit_bytes=VMEM_LIMIT_BYTES),
        name=f"prompt_mixer_l{layer}",
    )(*args)


def _pffn_kernel(x_ref, vec_ref, fcv_ref, w_up_ref, w_down_ref, y_ref, fstate_ref,
                 h_s, tail_s, acc_s, *, tc, permute_out):
    t = pl.program_id(1)
    nblk = tc // WINDOW
    ndelay = CONV_FF - 1

    @pl.when(t == 0)
    def _():
        tail_s[...] = jnp.zeros(tail_s.shape, F32)

    sub = lax.broadcasted_iota(jnp.int32, (SUBLANES, PROMPT_FF_CHUNK), 0)
    n_chunks = D_FF // PROMPT_FF_CHUNK

    def chunk_cols(c, part):
        return slice(part * D_FF + c * PROMPT_FF_CHUNK, part * D_FF + (c + 1) * PROMPT_FF_CHUNK)

    h_s[...] = _rmsnorm(x_ref[...], vec_ref[V_NORM_FFN_PRE:V_NORM_FFN_PRE + 1, :]).astype(BF16)
    hb = h_s[...]

    def up_project(c):
        return [_dot(hb, w_up_ref[:, chunk_cols(c, part)]) for part in range(2)]

    u_next = up_project(0)
    for c in range(n_chunks):
        u_cur = u_next
        if c + 1 < n_chunks:
            u_next = up_project(c + 1)
        conv = []
        for part in range(2):
            cols = chunk_cols(c, part)
            u = u_cur[part]
            bias = jnp.broadcast_to(fcv_ref[CONV_FF:CONV_FF + 1, cols], (SUBLANES, PROMPT_FF_CHUNK))
            taps = [jnp.broadcast_to(fcv_ref[j:j + 1, cols], (SUBLANES, PROMPT_FF_CHUNK)) for j in range(CONV_FF)]
            ys = []
            prev_tail = [tail_s[i * SUBLANES:(i + 1) * SUBLANES, cols] for i in range(ndelay)]
            for j in range(nblk):
                groups = _row_groups(u, j)
                delayed = _delayed_groups(groups, prev_tail, ndelay, sub)
                for k in range(SEG_LEN):
                    y = bias + taps[CONV_FF - 1] * groups[k]
                    for d in range(1, CONV_FF):
                        y = y + taps[CONV_FF - 1 - d] * delayed[d - 1][k]
                    ys.append(y)
                prev_tail = groups[SEG_LEN - ndelay:]
            conv.append(jnp.concatenate(ys, axis=0))
            tail_s[:, cols] = u[tc - ndelay * SUBLANES:tc]
            for i in range(ndelay):
                r = tc - (ndelay - i) * SUBLANES + SUBLANES - 1
                fstate_ref[i:i + 1, cols] = u[r:r + 1]
        act = (_gelu_tanh_doubled(conv[1]) * conv[0]).astype(BF16)
        part_f = _dot(act, w_down_ref[c * PROMPT_FF_CHUNK:(c + 1) * PROMPT_FF_CHUNK, :])
        if c == 0:
            acc_s[...] = part_f
        else:
            acc_s[...] += part_f
    y = x_ref[...] + _rmsnorm(acc_s[...], vec_ref[V_NORM_FFN_POST:V_NORM_FFN_POST + 1, :])
    if permute_out:
        for j in range(nblk):
            for k in range(SEG_LEN):
                y_ref[j * SUBLANES:(j + 1) * SUBLANES, k, :] = (
                    y[j * WINDOW + k * SUBLANES:j * WINDOW + (k + 1) * SUBLANES])
    else:
        y_ref[...] = y


def _prompt_ffn(x, vecs, fcv, w_up, w_down, layer, prev_states, permute_out):
    bsz, seq, _ = x.shape
    tc = PROMPT_TILE
    kern = functools.partial(_pffn_kernel, tc=tc, permute_out=permute_out)
    tile = pl.BlockSpec((None, tc, D_MODEL), lambda b, t: (b, t, 0))
    if permute_out:
        y_shape = (bsz, seq // SEG_LEN, SEG_LEN, D_MODEL)
        y_spec = pl.BlockSpec((None, tc // SEG_LEN, SEG_LEN, D_MODEL), lambda b, t: (b, t, 0, 0))
    else:
        y_shape, y_spec = (bsz, seq, D_MODEL), tile
    in_specs = [
        tile,
        _const_spec((None, VEC_ROWS, D_MODEL), (layer, 0, 0)),
        _const_spec((None, SUBLANES, 2 * D_FF), (layer, 0, 0)),
        _const_spec((D_MODEL, 2 * D_FF), (0, 0)),
        _const_spec((D_FF, D_MODEL), (0, 0)),
    ]
    args = [x, vecs, fcv, w_up, w_down]
    kern, in_specs, args, aliases = _with_state_aliases(kern, len(args), in_specs, args, prev_states, 1)
    return pl.pallas_call(
        kern,
        grid=(bsz, seq // tc),
        in_specs=in_specs,
        out_specs=(y_spec, pl.BlockSpec((None, None, CONV_FF - 1, 2 * D_FF), lambda b, t: (layer, b, 0, 0))),
        out_shape=(jax.ShapeDtypeStruct(y_shape, F32),
                   jax.ShapeDtypeStruct((DEPTH, bsz, CONV_FF - 1, 2 * D_FF), F32)),
        input_output_aliases=aliases,
        scratch_shapes=[
            pltpu.VMEM((tc, D_MODEL), BF16),
            pltpu.VMEM(((CONV_FF - 1) * SUBLANES, 2 * D_FF), F32),
            pltpu.VMEM((tc, D_MODEL), F32),
        ],
        compiler_params=pltpu.CompilerParams(
            dimension_semantics=("arbitrary", "arbitrary"), vmem_limit_bytes=VMEM_LIMIT_BYTES),
        name=f"prompt_ffn_l{layer}",
    )(*args)


def _cast_rows(src_ref, dst_ref, step=LRU_BLOCK):
    for r in range(0, src_ref.shape[0], step):
        dst_ref[r:r + step, :] = src_ref[r:r + step, :].astype(BF16)


def _smix_in_kernel(xs_ref, h0_ref, cprev_ref, vec_ref, w_in_f32_ref, wri_ref,
                    q_ref, k_ref, v_ref, lru_ref, gate_ref, w_in_ref, hlast_ref, cstate_ref,
                    h_s, xr_s, *, nb, nt):
    _cast_rows(w_in_f32_ref, w_in_ref)
    for t in range(nt):
        h_s[t * nb:(t + 1) * nb, :] = _rmsnorm(
            xs_ref[:, t, :], vec_ref[V_NORM_MIX_PRE:V_NORM_MIX_PRE + 1, :]).astype(BF16)
    hb = h_s[...]
    xr_s[...] = _dot(hb, w_in_ref[:, 0:C_Q])
    qf = _dot(hb, w_in_ref[:, C_Q:C_K]) * (HEAD_DIM ** -0.5)
    kf = _dot(hb, w_in_ref[:, C_K:C_V])
    vf = _dot(hb, w_in_ref[:, C_V:C_G])
    for t in range(nt):
        rows = slice(t * nb, (t + 1) * nb)
        q_ref[:, t * Q_COLS:(t + 1) * Q_COLS] = qf[rows].astype(BF16)
        for g in range(N_KV):
            k_ref[:, t * N_KV + g, :] = kf[rows, g * HEAD_DIM:(g + 1) * HEAD_DIM]
            v_ref[:, t * N_KV + g, :] = vf[rows, g * HEAD_DIM:(g + 1) * HEAD_DIM]
    gate_ref[...] = _sigmoid(_dot(hb, w_in_ref[:, C_G:IN_COLS]))

    npre = CONV_LRU - 1
    for t in range(nt - npre, nt):
        cstate_ref[:, t - (nt - npre), :] = xr_s[t * nb:(t + 1) * nb, :]

    for n in range(N_LRU_BLOCKS):
        cb = slice(n * LRU_BLOCK, (n + 1) * LRU_BLOCK)
        xx = [cprev_ref[:, j, cb] for j in range(npre)]
        xx += [xr_s[t * nb:(t + 1) * nb, cb] for t in range(nt)]
        xcs = []
        for t in range(nt):
            xc = vec_ref[V_CONV_B:V_CONV_B + 1, cb]
            for j in range(CONV_LRU):
                xc = xc + vec_ref[V_CONV_W + j:V_CONV_W + j + 1, cb] * xx[t + j]
            xcs.append(xc)
        a, ix, mult = _lru_coeffs(jnp.concatenate(xcs, axis=0), vec_ref, wri_ref, n)
        b = mult * ix
        h = h0_ref[:, cb]
        for t in range(nt):
            rows = slice(t * nb, (t + 1) * nb)
            h = a[rows] * h + b[rows]
            lru_ref[rows, cb] = h.astype(BF16)
        hlast_ref[:, cb] = h


def _sample_mixer_in(xs, h0, cprev, vecs, w_in, wri, layer, prev_states):
    nb, nt, _ = xs.shape
    assert PAST_LEN > 0
    kern = functools.partial(_smix_in_kernel, nb=nb, nt=nt)
    whole = lambda shape: pl.BlockSpec(shape, lambda i: (0,) * len(shape))
    conv_state = pl.BlockSpec((None, nb, CONV_LRU - 1, D_RNN), lambda i: (layer, 0, 0, 0))
    in_specs = [
        whole((nb, nt, D_MODEL)),
        pl.BlockSpec((None, nb, D_RNN), lambda i: (layer, 0, 0)),
        conv_state,
        _const_spec((None, VEC_ROWS, D_MODEL), (layer, 0, 0)),
        _const_spec((None, D_MODEL, IN_COLS), (layer, 0, 0)),
        _const_spec((None, N_LRU_BLOCKS, LRU_BLOCK, 2 * LRU_BLOCK), (layer, 0, 0, 0)),
    ]
    args = [xs, h0, cprev, vecs, w_in, wri]
    kern, in_specs, args, aliases = _with_state_aliases(kern, len(args), in_specs, args, prev_states, 6)
    return pl.pallas_call(
        kern,
        grid=(1,),
        in_specs=in_specs,
        out_specs=(whole((nb, nt * Q_COLS)), whole((nb, nt * N_KV, HEAD_DIM)), whole((nb, nt * N_KV, HEAD_DIM)),
                   whole((nt * nb, D_RNN)), whole((nt * nb, 2 * D_MODEL)),
                   pl.BlockSpec((D_MODEL, IN_COLS), lambda i: (0, 0), pipeline_mode=pl.Buffered(1)),
                   pl.BlockSpec((None, nb, D_RNN), lambda i: (layer, 0, 0)), conv_state),
        out_shape=(jax.ShapeDtypeStruct((nb, nt * Q_COLS), BF16),
                   jax.ShapeDtypeStruct((nb, nt * N_KV, HEAD_DIM), F32),
                   jax.ShapeDtypeStruct((nb, nt * N_KV, HEAD_DIM), F32),
                   jax.ShapeDtypeStruct((nt * nb, D_RNN), BF16),
                   jax.ShapeDtypeStruct((nt * nb, 2 * D_MODEL), F32),
                   jax.ShapeDtypeStruct((D_MODEL, IN_COLS), BF16),
                   jax.ShapeDtypeStruct((DEPTH, nb, D_RNN), F32),
                   jax.ShapeDtypeStruct((DEPTH, nb, CONV_LRU - 1, D_RNN), F32)),
        input_output_aliases=aliases,
        scratch_shapes=[pltpu.VMEM((nt * nb, D_MODEL), BF16), pltpu.VMEM((nt * nb, D_RNN), F32)],
        compiler_params=pltpu.CompilerParams(
            dimension_semantics=("arbitrary",), vmem_limit_bytes=VMEM_LIMIT_BYTES),
        name=f"sample_mixer_in_l{layer}",
    )(*args)


def _sattn_kernel(q_ref, kn_ref, vn_ref, ck_ref, cv_ref, tab_ref, attn_ref, sk_ref, sv_ref,
                  kc_s, vc_s, *, bt, nt):
    n_cache = WINDOW * N_KV
    n_new = nt * N_KV
    for u in range(SAMPLE_ATTN_UNROLL):
        kc_s[u, n_cache + n_new:SAMPLE_KEYS, :] = jnp.zeros((SAMPLE_KEYS - n_cache - n_new, HEAD_DIM), F32)
        vc_s[u, n_cache + n_new:SAMPLE_KEYS, :] = jnp.zeros((SAMPLE_KEYS - n_cache - n_new, HEAD_DIM), F32)

    def body(i, _):
        seqs = [i * SAMPLE_ATTN_UNROLL + u for u in range(SAMPLE_ATTN_UNROLL)]
        scores = []
        for u, b in enumerate(seqs):
            for c_ref, n_ref, s_ref, scr in ((ck_ref, kn_ref, sk_ref, kc_s.at[u]), (cv_ref, vn_ref, sv_ref, vc_s.at[u])):
                scr[0:n_cache, :] = c_ref[b]
                scr[n_cache:n_cache + n_new, :] = n_ref[b]
                s_ref[b, 0:n_cache - n_new, :] = c_ref[b, n_new:n_cache, :]
                s_ref[b, n_cache - n_new:n_cache, :] = n_ref[b]
            scores.append(_dot_nt(q_ref[b], kc_s[u].astype(BF16)))
        probs = []
        for s in scores:
            s = s + tab_ref[...]
            p = jnp.exp(s - jnp.max(s, axis=-1, keepdims=True))
            probs.append((p.astype(BF16), 1.0 / jnp.sum(p, axis=-1, keepdims=True)))
        for u, b in enumerate(seqs):
            p, inv = probs[u]
            attn_ref[b] = (_dot(p, vc_s[u].astype(BF16)) * inv).astype(BF16)
        return 0

    lax.fori_loop(0, bt // SAMPLE_ATTN_UNROLL, body, 0)


def _sample_attention(q, kn, vn, cache_k, cache_v, stab, layer, prev_states):
    nb, rows, _ = q.shape
    nt = rows // N_HEADS
    bt = SAMPLE_BATCH_TILE
    kern = functools.partial(_sattn_kernel, bt=bt, nt=nt)
    cache_spec = pl.BlockSpec((None, bt, WINDOW * N_KV, HEAD_DIM), lambda i: (layer, i, 0, 0))
    new_spec = pl.BlockSpec((bt, nt * N_KV, HEAD_DIM), lambda i: (i, 0, 0))
    q_spec = pl.BlockSpec((bt, rows, HEAD_DIM), lambda i: (i, 0, 0))
    in_specs = [q_spec, new_spec, new_spec, cache_spec, cache_spec,
                _const_spec((None, rows, SAMPLE_KEYS), (layer, 0, 0))]
    args = [q, kn, vn, cache_k, cache_v, stab]
    kern, in_specs, args, aliases = _with_state_aliases(kern, len(args), in_specs, args, prev_states, 1)
    return pl.pallas_call(
        kern,
        grid=(nb // bt,),
        in_specs=in_specs,
        out_specs=(q_spec, cache_spec, cache_spec),
        out_shape=(jax.ShapeDtypeStruct((nb, rows, HEAD_DIM), BF16),
                   jax.ShapeDtypeStruct((DEPTH, nb, WINDOW * N_KV, HEAD_DIM), F32),
                   jax.ShapeDtypeStruct((DEPTH, nb, WINDOW * N_KV, HEAD_DIM), F32)),
        input_output_aliases=aliases,
        scratch_shapes=[pltpu.VMEM((SAMPLE_ATTN_UNROLL, SAMPLE_KEYS, HEAD_DIM), F32),
                        pltpu.VMEM((SAMPLE_ATTN_UNROLL, SAMPLE_KEYS, HEAD_DIM), F32)],
        compiler_params=pltpu.CompilerParams(
            dimension_semantics=("arbitrary",), vmem_limit_bytes=VMEM_LIMIT_BYTES),
        name=f"sample_attention_l{layer}",
    )(*args)


def _smix_out_kernel(xs_ref, lru_ref, attn_ref, gate_ref, vec_ref, w_lo_f32_ref, w_ao_f32_ref, w_out_f32_ref,
                     xmid_ref, w_lo_ref, w_ao_ref, w_out_ref, a_s, *, nb, nt):
    _cast_rows(w_lo_f32_ref, w_lo_ref)
    _cast_rows(w_ao_f32_ref, w_ao_ref)
    _cast_rows(w_out_f32_ref, w_out_ref)
    for t in range(nt):
        a_s[t * nb:(t + 1) * nb, :] = attn_ref[:, t * Q_COLS:(t + 1) * Q_COLS]
    merged = (gate_ref[:, 0:D_MODEL] * _dot(lru_ref[...], w_lo_ref[...])
              + gate_ref[:, D_MODEL:2 * D_MODEL] * _dot(a_s[...], w_ao_ref[...]))
    m = _dot(merged.astype(BF16), w_out_ref[...])
    mn = _rmsnorm(m, vec_ref[V_NORM_MIX_POST:V_NORM_MIX_POST + 1, :])
    for t in range(nt):
        rows = slice(t * nb, (t + 1) * nb)
        xmid_ref[rows, :] = xs_ref[:, t, :] + mn[rows]


def _sample_mixer_out(xs, lru, attn, gates, vecs, w_lo, w_ao, w_out, layer):
    nb, nt, _ = xs.shape
    kern = functools.partial(_smix_out_kernel, nb=nb, nt=nt)
    whole = lambda shape: pl.BlockSpec(shape, lambda i: (0,) * len(shape))
    return pl.pallas_call(
        kern,
        grid=(1,),
        in_specs=[
            whole(xs.shape), whole(lru.shape), whole(attn.shape), whole(gates.shape),
            _const_spec((None, VEC_ROWS, D_MODEL), (layer, 0, 0)),
            _const_spec((None, D_RNN, D_MODEL), (layer, 0, 0)),
            _const_spec((None, Q_COLS, D_MODEL), (layer, 0, 0)),
            _const_spec((None, D_MODEL, D_MODEL), (layer, 0, 0)),
        ],
        out_specs=(whole((nt * nb, D_MODEL)),) + (whole((D_MODEL, D_MODEL)),) * 3,
        out_shape=(jax.ShapeDtypeStruct((nt * nb, D_MODEL), F32),)
        + (jax.ShapeDtypeStruct((D_MODEL, D_MODEL), BF16),) * 3,
        scratch_shapes=[pltpu.VMEM((nt * nb, Q_COLS), BF16)],
        compiler_params=pltpu.CompilerParams(
            dimension_semantics=("arbitrary",), vmem_limit_bytes=VMEM_LIMIT_BYTES),
        name=f"sample_mixer_out_l{layer}",
    )(xs, lru, attn, gates, vecs, w_lo, w_ao, w_out)


def _sffn_kernel(x_ref, vec_ref, fcv_ref, prev_ref, w_f32_ref, wd_f32_ref, ys_ref, w_ref, wd_ref, fstate_ref,
                 h_s, val_s, acc_s, *, nb, nt, nck):
    c = pl.program_id(0)

    @pl.when(c == 0)
    def _():
        h_s[...] = _rmsnorm(x_ref[...], vec_ref[V_NORM_FFN_PRE:V_NORM_FFN_PRE + 1, :]).astype(BF16)
        acc_s[...] = jnp.zeros(acc_s.shape, F32)

    _cast_rows(w_f32_ref, w_ref)
    u = _dot(h_s[...], w_ref[...])
    uu = [prev_ref[:, j, :] for j in range(CONV_FF - 1)] + [u[t * nb:(t + 1) * nb] for t in range(nt)]
    ys = []
    for t in range(nt):
        y = fcv_ref[CONV_FF:CONV_FF + 1, :]
        for j in range(CONV_FF):
            y = y + fcv_ref[j:j + 1, :] * uu[t + j]
        ys.append(y)
    conv = jnp.concatenate(ys, axis=0)
    for j in range(CONV_FF - 1):
        fstate_ref[:, j, :] = uu[nt + j]

    @pl.when(c < nck)
    def _():
        val_s[c] = conv

    @pl.when(c >= nck)
    def _():
        act = (_gelu_tanh_doubled(conv) * val_s[c - nck]).astype(BF16)
        _cast_rows(wd_f32_ref, wd_ref)
        acc_s[...] += _dot(act, wd_ref[...])

    @pl.when(c == 2 * nck - 1)
    def _():
        y = x_ref[...] + _rmsnorm(acc_s[...], vec_ref[V_NORM_FFN_POST:V_NORM_FFN_POST + 1, :])
        for t in range(nt):
            ys_ref[:, t, :] = y[t * nb:(t + 1) * nb]


def _sample_ffn(xmid, fprev, vecs, fcv, w_up, w_down, layer, nb, prev_states):
    nt = xmid.shape[0] // nb
    ck = FF_CHUNK
    nck = D_FF // ck
    kern = functools.partial(_sffn_kernel, nb=nb, nt=nt, nck=nck)
    state_spec = pl.BlockSpec((None, nb, CONV_FF - 1, ck), lambda c: (layer, 0, 0, c))
    in_specs = [
        _const_spec((nt * nb, D_MODEL), (0, 0)),
        _const_spec((None, VEC_ROWS, D_MODEL), (layer, 0, 0)),
        pl.BlockSpec((None, SUBLANES, ck), lambda c: (layer, 0, c)),
        state_spec,
        pl.BlockSpec((None, D_MODEL, ck), lambda c: (layer, 0, c)),
        pl.BlockSpec((None, ck, D_MODEL), lambda c: (layer, jnp.maximum(c - nck, 0), 0)),
    ]
    args = [xmid, vecs, fcv, fprev, w_up, w_down]
    kern, in_specs, args, aliases = _with_state_aliases(kern, len(args), in_specs, args, prev_states, 3)
    return pl.pallas_call(
        kern,
        grid=(2 * nck,),
        in_specs=in_specs,
        out_specs=(pl.BlockSpec((nb, nt, D_MODEL), lambda c: (0, 0, 0)),
                   pl.BlockSpec((D_MODEL, ck), lambda c: (0, c)),
                   pl.BlockSpec((ck, D_MODEL), lambda c: (jnp.maximum(c - nck, 0), 0)),
                   state_spec),
        out_shape=(jax.ShapeDtypeStruct((nb, nt, D_MODEL), F32),
                   jax.ShapeDtypeStruct((D_MODEL, 2 * D_FF), BF16),
                   jax.ShapeDtypeStruct((D_FF, D_MODEL), BF16),
                   jax.ShapeDtypeStruct((DEPTH, nb, CONV_FF - 1, 2 * D_FF), F32)),
        input_output_aliases=aliases,
        scratch_shapes=[pltpu.VMEM((nt * nb, D_MODEL), BF16),
                        pltpu.VMEM((nck, nt * nb, ck), F32),
                        pltpu.VMEM((nt * nb, D_MODEL), F32)],
        compiler_params=pltpu.CompilerParams(
            dimension_semantics=("arbitrary",), vmem_limit_bytes=VMEM_LIMIT_BYTES),
        name=f"sample_ffn_l{layer}",
    )(*args)


def kernel(x_prompt, x_sample, state_lru_h, state_lru_conv, cache_win_k, cache_win_v, state_ffn_conv,
           norm_mix_pre, norm_mix_post, norm_ffn_pre, norm_ffn_post, w_in, conv_lru_w, conv_lru_b,
           lru_wr, lru_br, lru_wi, lru_bi, lru_lambda, w_lru_o, w_attn_o, w_out, attn_sink, rel_bias,
           w_up, ffn_conv_w, ffn_conv_b, w_down):
    nb, nt, _ = x_sample.shape
    bp = x_prompt.shape[0]

    row = lambda v: v[:, None, :]
    vecs = jnp.concatenate(
        [row(norm_mix_pre), row(norm_mix_post), row(conv_lru_b), row(lru_br), row(lru_bi), row(lru_lambda),
         conv_lru_w, row(norm_ffn_pre), row(norm_ffn_post),
         jnp.zeros((DEPTH, VEC_ROWS - 12, D_MODEL), F32)], axis=1)
    fcv = jnp.concatenate(
        [ffn_conv_w, row(ffn_conv_b), jnp.zeros((DEPTH, SUBLANES - CONV_FF - 1, 2 * D_FF), F32)], axis=1)
    fcv = fcv * jnp.where(jnp.arange(2 * D_FF) < D_FF, 0.5, 1.0).astype(F32)
    wri_b = jnp.concatenate([lru_wr, lru_wi], axis=-1).astype(BF16)

    ptab, stab = _bias_tables(rel_bias, attn_sink)

    yp, xs = x_prompt, x_sample
    p_mix = p_ffn = s_mix = s_att = s_ffn = None
    cache_rows = lambda c: c.reshape(DEPTH, nb, WINDOW * N_KV, HEAD_DIM)
    for l in range(DEPTH):
        q, kn, vn, lru, gates, w_in_b, *s_mix = _sample_mixer_in(
            xs, state_lru_h, state_lru_conv, vecs, w_in, wri_b, l, s_mix)
        attn, *s_att = _sample_attention(
            q.reshape(nb, nt * N_HEADS, HEAD_DIM), kn, vn, cache_rows(cache_win_k), cache_rows(cache_win_v),
            stab, l, s_att)
        xmid, w_lo_b, w_ao_b, w_out_b = _sample_mixer_out(
            xs, lru, attn.reshape(nb, nt * Q_COLS), gates, vecs, w_lru_o, w_attn_o, w_out, l)
        xs, w_up_b, w_down_b, *s_ffn = _sample_ffn(xmid, state_ffn_conv, vecs, fcv, w_up, w_down, l, nb, s_ffn)

        yp, *p_mix = _prompt_mixer(yp, vecs, w_in_b, wri_b, w_lo_b, w_ao_b, w_out_b, ptab, attn_sink, l, p_mix,
                                   permute_in=(l == 0))
        yp, *p_ffn = _prompt_ffn(yp, vecs, fcv, w_up_b, w_down_b, l, p_ffn, permute_out=(l == DEPTH - 1))

    p_h, p_c, p_k, p_v = p_mix
    s_h, s_c = s_mix
    s_k, s_v = s_att
    kv_shape = (DEPTH, bp, WINDOW, N_KV, HEAD_DIM)
    return (yp.reshape(x_prompt.shape), xs, p_h, p_c, p_k.reshape(kv_shape), p_v.reshape(kv_shape), p_ffn[0],
            s_h, s_c, s_k.reshape(cache_win_k.shape), s_v.reshape(cache_win_v.shape), s_ffn[0])
```

```python
import functools
import math

import numpy as np
import jax
import jax.numpy as jnp
from jax import lax
from jax.experimental import pallas as pl
from jax.experimental.pallas import tpu as pltpu

D_MODEL = 1024
DEPTH = 2
PAST_LEN = 16384
D_RNN = D_MODEL
N_LRU_BLOCKS = 8
LRU_BLOCK = D_RNN // N_LRU_BLOCKS
CONV_LRU = 4
LRU_C = 8.0
N_HEADS = 8
N_KV = 2
GROUP = N_HEADS // N_KV
HEAD_DIM = D_MODEL // N_HEADS
WINDOW = 128
N_BUCKETS = 32
MAX_EXACT = N_BUCKETS // 2
MAX_DISTANCE = 128
D_FF = 4 * D_MODEL
CONV_FF = 3
EPS = 1e-6
Q_COLS = N_HEADS * HEAD_DIM
KV_COLS = N_KV * HEAD_DIM
IN_COLS = D_RNN + Q_COLS + 2 * KV_COLS + 2 * D_MODEL
C_Q = D_RNN
C_K = C_Q + Q_COLS
C_V = C_K + KV_COLS
C_G = C_V + KV_COLS

F32 = jnp.float32
BF16 = jnp.bfloat16

SUBLANES = 8
LANES = 128
VMEM_LIMIT_BYTES = 56 * 1024 * 1024

V_NORM_MIX_PRE, V_NORM_MIX_POST, V_CONV_B, V_BR, V_BI, V_LAMBDA, V_CONV_W = 0, 1, 2, 3, 4, 5, 6
V_NORM_FFN_PRE, V_NORM_FFN_POST = 10, 11
VEC_ROWS = 12
FCV_ROWS = CONV_FF + 1

PROMPT_TILE = 512
PROJ_PIECE = 2 * KV_COLS
assert (C_K - C_Q) % PROJ_PIECE == 0 and (IN_COLS - C_G) % PROJ_PIECE == 0
PROMPT_FF_CHUNK = 2048
FF_CHUNK = 1024
SAMPLE_KEYS = N_KV * (WINDOW + 4) + SUBLANES
SINK_COL = N_KV * (WINDOW + 4)
SAMPLE_BATCH_TILE = 16
SAMPLE_ATTN_UNROLL = 16
SEG_LEN = WINDOW // SUBLANES


def _bucket_thresholds():
    d = np.arange(0, 2 * WINDOW)
    nf = np.maximum(d, 1).astype(np.float64)
    large = MAX_EXACT + (np.log(nf / MAX_EXACT) / math.log(MAX_DISTANCE / MAX_EXACT)
                         * (N_BUCKETS - MAX_EXACT)).astype(np.int64)
    bucket = np.where(d < MAX_EXACT, d, np.minimum(large, N_BUCKETS - 1))
    return tuple(int(d[bucket >= b].min()) for b in range(1, N_BUCKETS))


_BUCKET_THRESHOLDS = _bucket_thresholds()


def _dot(a, b):
    return jnp.dot(a, b, preferred_element_type=F32)


def _dot_nt(a, b):
    return lax.dot_general(a, b, (((1,), (1,)), ((), ())), preferred_element_type=F32)


def _rmsnorm(x, g):
    return x * lax.rsqrt(jnp.mean(x * x, axis=-1, keepdims=True) + EPS) * g


def _sigmoid(x):
    return 1.0 / (1.0 + jnp.exp(-x))


def _gelu_tanh_doubled(x):
    c = math.sqrt(2.0 / math.pi)
    t = jnp.tanh(x * (c + (c * 0.044715) * (x * x)))
    return x + x * t


def _const_spec(block_shape, index):
    return pl.BlockSpec(block_shape, lambda *_: index, pipeline_mode=pl.Buffered(1))


def _with_state_aliases(kern, n_in, in_specs, args, prev_states, first_state_out):
    if prev_states is None:
        return kern, list(in_specs), list(args), {}
    n = len(prev_states)

    def body(*refs):
        return kern(*refs[:n_in], *refs[n_in + n:])

    return (body, list(in_specs) + [pl.BlockSpec(memory_space=pl.ANY)] * n, list(args) + list(prev_states),
            {n_in + i: first_state_out + i for i in range(n)})


def _bucket_of(d):
    n = jnp.maximum(d, 0)
    bucket = jnp.zeros(d.shape, jnp.int32)
    for thr in _BUCKET_THRESHOLDS:
        bucket = bucket + jnp.where(n >= thr, 1, 0)
    return bucket


def _block_time(p):
    return lax.bitwise_and(p, SUBLANES - 1) * SEG_LEN + lax.shift_right_logical(p, 3)


def _table_kernel(rel_ref, sink_ref, pt_ref, st_ref):
    qi = lax.broadcasted_iota(jnp.int32, (WINDOW, 2 * WINDOW), 0)
    kj = lax.broadcasted_iota(jnp.int32, (WINDOW, 2 * WINDOW), 1)
    kpos = lax.bitwise_and(kj, WINDOW - 1)
    d = _block_time(qi) + WINDOW - (_block_time(kpos) + (kj - kpos))
    bucket = _bucket_of(d)
    in_band = jnp.where(d >= 0, jnp.where(d < WINDOW, 1, 0), 0)
    cur_only = jnp.where(kj >= WINDOW, in_band, 0)
    for h in range(N_HEADS):
        val = jnp.zeros(d.shape, F32)
        for b in range(N_BUCKETS):
            val = jnp.where(bucket == b, rel_ref[h, b], val)
        pt_ref[1, h] = jnp.where(in_band == 1, val, -jnp.inf)
        pt_ref[0, h] = jnp.where(cur_only == 1, val, -jnp.inf)

    r = lax.broadcasted_iota(jnp.int32, (4 * N_HEADS, SAMPLE_KEYS), 0)
    j = lax.broadcasted_iota(jnp.int32, (4 * N_HEADS, SAMPLE_KEYS), 1)
    t = lax.shift_right_logical(r, 3)
    hh = lax.bitwise_and(r, N_HEADS - 1)
    d = t + WINDOW - lax.shift_right_logical(j, 1)
    bucket = _bucket_of(d)
    in_band = jnp.where(d >= 0, jnp.where(d < WINDOW, 1, 0), 0)
    in_band = jnp.where(lax.bitwise_and(j, N_KV - 1) == lax.shift_right_logical(hh, 2), in_band, 0)
    val = jnp.zeros(d.shape, F32)
    for h in range(N_HEADS):
        hval = jnp.zeros(d.shape, F32)
        for b in range(N_BUCKETS):
            hval = jnp.where(bucket == b, rel_ref[h, b], hval)
        val = jnp.where(hh == h, hval, val)
    val = jnp.where(in_band == 1, val, -jnp.inf)
    for l in range(DEPTH):
        sk = jnp.zeros(d.shape, F32)
        for h in range(N_HEADS):
            sk = jnp.where(hh == h, sink_ref[l, h], sk)
        st_ref[l] = jnp.where(j == SINK_COL, sk, val)


def _bias_tables(rel_bias, attn_sink):
    smem = pl.BlockSpec(memory_space=pltpu.SMEM)
    return pl.pallas_call(
        _table_kernel,
        out_shape=(jax.ShapeDtypeStruct((2, N_HEADS, WINDOW, 2 * WINDOW), F32),
                   jax.ShapeDtypeStruct((DEPTH, 4 * N_HEADS, SAMPLE_KEYS), F32)),
        in_specs=[smem, smem],
        name="bias_tables",
    )(rel_bias, attn_sink)


def _lru_coeffs(xc, vec_ref, wri_ref, n):
    cb = slice(n * LRU_BLOCK, (n + 1) * LRU_BLOCK)
    rw = _dot(xc.astype(BF16), wri_ref[n])
    r = _sigmoid(rw[:, :LRU_BLOCK] + vec_ref[V_BR:V_BR + 1, cb])
    i = _sigmoid(rw[:, LRU_BLOCK:] + vec_ref[V_BI:V_BI + 1, cb])
    z = -vec_ref[V_LAMBDA:V_LAMBDA + 1, cb]
    softplus = jnp.maximum(z, 0.0) + jnp.log1p(jnp.exp(-jnp.abs(z)))
    log_a = (-LRU_C * softplus) * r
    a = jnp.exp(log_a)
    mult = jnp.sqrt(jnp.maximum(1.0 - a * a, 0.0))
    return a, i * xc, mult


def _delayed_groups(groups, prev_tail, ndelay, sub):
    ng = len(groups)
    wrapped = {}
    for i in range(ndelay):
        k = ng - ndelay + i
        wrapped[k] = jnp.where(sub == 0, pltpu.roll(prev_tail[i], 1, 0), pltpu.roll(groups[k], 1, 0))
    return [[groups[k - d] if k >= d else wrapped[ng + k - d] for k in range(ng)] for d in range(1, ndelay + 1)]


def _row_groups(x, block):
    return [x[block * WINDOW + k * SUBLANES:block * WINDOW + (k + 1) * SUBLANES] for k in range(SEG_LEN)]


def _pmix_kernel(x_ref, vec_ref, w_in_ref, wri_ref, w_lo_ref, w_ao_ref, w_out_ref, tab_ref, sink_ref,
                 y_ref, hlast_ref, cstate_ref, kstate_ref, vstate_ref,
                 xp_s, h_s, xr_s, xrc_s, q_s, k_s, v_s, kvf_s, g_s, lru_s, attn_s, hc_s, hl_s, *, layer, tc, permute_in):
    t = pl.program_id(1)
    nblk = tc // WINDOW
    ndelay = CONV_LRU - 1
    n_pieces = (IN_COLS - C_Q) // PROJ_PIECE
    batch_row = lax.broadcasted_iota(jnp.int32, (hl_s.shape[0], LRU_BLOCK), 0)

    @pl.when(jnp.logical_and(t == 0, pl.program_id(0) == 0))
    def _():
        hl_s[...] = jnp.zeros(hl_s.shape, F32)

    @pl.when(t == 0)
    def _():
        xrc_s[...] = jnp.zeros(xrc_s.shape, F32)
        k_s[0:WINDOW, :] = jnp.zeros((WINDOW, KV_COLS), BF16)
        v_s[0:WINDOW, :] = jnp.zeros((WINDOW, KV_COLS), BF16)
        hc_s[...] = jnp.zeros(hc_s.shape, F32)

    if permute_in:
        for j in range(nblk):
            for k in range(SEG_LEN):
                xp_s[j * WINDOW + k * SUBLANES:j * WINDOW + (k + 1) * SUBLANES, :] = (
                    x_ref[j * SUBLANES:(j + 1) * SUBLANES, k, :])
        x_tile = xp_s
    else:
        x_tile = x_ref

    h_s[...] = _rmsnorm(x_tile[...], vec_ref[V_NORM_MIX_PRE:V_NORM_MIX_PRE + 1, :]).astype(BF16)
    hb = h_s[...]
    xr_s[...] = _dot(hb, w_in_ref[:, 0:C_Q])

    def project_piece(i):
        c0 = C_Q + i * PROJ_PIECE
        z = _dot(hb, w_in_ref[:, c0:c0 + PROJ_PIECE])
        if c0 < C_K:
            q_s[:, c0 - C_Q:c0 - C_Q + PROJ_PIECE] = (z * (HEAD_DIM ** -0.5)).astype(BF16)
        elif c0 == C_K:
            k_s[WINDOW:WINDOW + tc, :] = z[:, 0:KV_COLS].astype(BF16)
            v_s[WINDOW:WINDOW + tc, :] = z[:, KV_COLS:2 * KV_COLS].astype(BF16)
            kvf_s[0] = z[tc - WINDOW:tc, 0:KV_COLS]
            kvf_s[1] = z[tc - WINDOW:tc, KV_COLS:2 * KV_COLS]
        else:
            g_s[:, c0 - C_G:c0 - C_G + PROJ_PIECE] = _sigmoid(z)

    for i in range(ndelay):
        r = tc - (ndelay - i) * SUBLANES + SUBLANES - 1
        cstate_ref[i:i + 1, :] = xr_s[r:r + 1, :]

    sub = lax.broadcasted_iota(jnp.int32, (SUBLANES, LRU_BLOCK), 0)
    seq_start = (sub + t) == 0
    for n in range(N_LRU_BLOCKS):
        cb = slice(n * LRU_BLOCK, (n + 1) * LRU_BLOCK)
        bias = jnp.broadcast_to(vec_ref[V_CONV_B:V_CONV_B + 1, cb], (SUBLANES, LRU_BLOCK))
        taps = [jnp.broadcast_to(vec_ref[V_CONV_W + j:V_CONV_W + j + 1, cb], (SUBLANES, LRU_BLOCK))
                for j in range(CONV_LRU)]
        xcs = []
        prev_tail = [xrc_s[i * SUBLANES:(i + 1) * SUBLANES, cb] for i in range(ndelay)]
        for j in range(nblk):
            groups = [xr_s[j * WINDOW + k * SUBLANES:j * WINDOW + (k + 1) * SUBLANES, cb] for k in range(SEG_LEN)]
            delayed = _delayed_groups(groups, prev_tail, ndelay, sub)
            for k in range(SEG_LEN):
                xc = bias + taps[CONV_LRU - 1] * groups[k]
                for d in range(1, CONV_LRU):
                    xc = xc + taps[CONV_LRU - 1 - d] * delayed[d - 1][k]
                xcs.append(xc)
            prev_tail = groups[SEG_LEN - ndelay:]
        a, ix, mult = _lru_coeffs(jnp.concatenate(xcs, axis=0), vec_ref, wri_ref, n)
        b = mult * ix

        carry = hc_s[0:1, cb]
        hs = []
        for j in range(nblk):
            ag, bg, ig = _row_groups(a, j), _row_groups(b, j), _row_groups(ix, j)
            if j == 0:
                bg[0] = jnp.where(seq_start, ig[0], bg[0])
            acc_a, acc_b = [ag[0]], [bg[0]]
            for k in range(1, SEG_LEN):
                acc_b.append(ag[k] * acc_b[-1] + bg[k])
                acc_a.append(ag[k] * acc_a[-1])
            seg_a, seg_b = acc_a[-1], acc_b[-1]
            for s in (1, 2, 4):
                ash = jnp.where(sub >= s, pltpu.roll(seg_a, s, 0), 1.0)
                bsh = jnp.where(sub >= s, pltpu.roll(seg_b, s, 0), 0.0)
                seg_b = seg_a * bsh + seg_b
                seg_a = seg_a * ash
            h_end = seg_a * carry + seg_b
            h_in = jnp.where(sub == 0, carry, pltpu.roll(h_end, 1, 0))
            hs += [acc_a[k] * h_in + acc_b[k] for k in range(SEG_LEN)]
            carry = h_end[SUBLANES - 1:SUBLANES, :]
        lru_s[:, cb] = jnp.concatenate(hs, axis=0).astype(BF16)
        hc_s[0:1, cb] = carry
        hl_s[:, cb] = jnp.where(batch_row == pl.program_id(0), carry, hl_s[:, cb])
        if n < n_pieces:
            project_piece(n)
    for i in range(N_LRU_BLOCKS, n_pieces):
        project_piece(i)
    hlast_ref[...] = hl_s[...]

    first = jnp.where(t == 0, 0, 1)

    def scores(j, g):
        q4 = jnp.concatenate(
            [q_s[j * WINDOW:(j + 1) * WINDOW, (g * GROUP + hg) * HEAD_DIM:(g * GROUP + hg + 1) * HEAD_DIM]
             for hg in range(GROUP)], axis=0)
        return _dot_nt(q4, k_s[j * WINDOW:(j + 2) * WINDOW, g * HEAD_DIM:(g + 1) * HEAD_DIM])

    def attend(j, g, s):
        variant = first if j == 0 else 1
        ps, invs = [], []
        for hg in range(GROUP):
            head = g * GROUP + hg
            sh = s[hg * WINDOW:(hg + 1) * WINDOW] + tab_ref[variant, head]
            sk = sink_ref[layer, head]
            m = jnp.maximum(jnp.max(sh, axis=-1, keepdims=True), sk)
            p = jnp.exp(sh - m)
            den = jnp.sum(p, axis=-1, keepdims=True) + jnp.exp(sk - m)
            ps.append(p.astype(BF16))
            invs.append(1.0 / den)
        o4 = _dot(jnp.concatenate(ps, axis=0),
                  v_s[j * WINDOW:(j + 2) * WINDOW, g * HEAD_DIM:(g + 1) * HEAD_DIM])
        for hg in range(GROUP):
            head = g * GROUP + hg
            attn_s[j * WINDOW:(j + 1) * WINDOW, head * HEAD_DIM:(head + 1) * HEAD_DIM] = (
                o4[hg * WINDOW:(hg + 1) * WINDOW] * invs[hg]).astype(BF16)

    pairs = [(j, g) for j in range(nblk) for g in range(N_KV)]
    s_next = scores(*pairs[0])
    for i, (j, g) in enumerate(pairs):
        s_cur = s_next
        if i + 1 < len(pairs):
            s_next = scores(*pairs[i + 1])
        attend(j, g, s_cur)

    xrc_s[...] = xr_s[tc - ndelay * SUBLANES:tc, :]
    k_s[0:WINDOW, :] = k_s[tc:tc + WINDOW, :]
    v_s[0:WINDOW, :] = v_s[tc:tc + WINDOW, :]

    merged = (g_s[:, 0:D_MODEL] * _dot(lru_s[...], w_lo_ref[...])
              + g_s[:, D_MODEL:2 * D_MODEL] * _dot(attn_s[...], w_ao_ref[...]))
    m = _dot(merged.astype(BF16), w_out_ref[...])
    y_ref[...] = x_tile[...] + _rmsnorm(m, vec_ref[V_NORM_MIX_POST:V_NORM_MIX_POST + 1, :])

    @pl.when(t == pl.num_programs(1) - 1)
    def _():
        for k in range(SEG_LEN):
            rows = slice(k * SUBLANES, (k + 1) * SUBLANES)
            for g in range(N_KV):
                kstate_ref[:, k, g, :] = kvf_s[0, rows, g * HEAD_DIM:(g + 1) * HEAD_DIM]
                vstate_ref[:, k, g, :] = kvf_s[1, rows, g * HEAD_DIM:(g + 1) * HEAD_DIM]


def _prompt_mixer(x, vecs, w_in, wri, w_lo, w_ao, w_out, tab, sink, layer, prev_states, permute_in):
    bsz, seq, _ = x.shape
    tc = PROMPT_TILE
    kern = functools.partial(_pmix_kernel, layer=layer, tc=tc, permute_in=permute_in)
    tile = pl.BlockSpec((None, tc, D_MODEL), lambda b, t: (b, t, 0))
    kv_state = pl.BlockSpec((None, None, SUBLANES, SEG_LEN, N_KV, HEAD_DIM), lambda b, t: (layer, b, 0, 0, 0, 0))
    if permute_in:
        x = x.reshape(bsz, seq // SEG_LEN, SEG_LEN, D_MODEL)
        x_spec = pl.BlockSpec((None, tc // SEG_LEN, SEG_LEN, D_MODEL), lambda b, t: (b, t, 0, 0))
    else:
        x_spec = tile
    in_specs = [
        x_spec,
        _const_spec((None, VEC_ROWS, D_MODEL), (layer, 0, 0)),
        _const_spec((D_MODEL, IN_COLS), (0, 0)),
        _const_spec((None, N_LRU_BLOCKS, LRU_BLOCK, 2 * LRU_BLOCK), (layer, 0, 0, 0)),
        _const_spec((D_RNN, D_MODEL), (0, 0)),
        _const_spec((Q_COLS, D_MODEL), (0, 0)),
        _const_spec((D_MODEL, D_MODEL), (0, 0)),
        _const_spec((2, N_HEADS, WINDOW, 2 * WINDOW), (0, 0, 0, 0)),
        pl.BlockSpec(memory_space=pltpu.SMEM),
    ]
    args = [x, vecs, w_in, wri, w_lo, w_ao, w_out, tab, sink]
    kern, in_specs, args, aliases = _with_state_aliases(kern, len(args), in_specs, args, prev_states, 1)
    return pl.pallas_call(
        kern,
        grid=(bsz, seq // tc),
        in_specs=in_specs,
        out_specs=(tile,
                   pl.BlockSpec((None, bsz, D_RNN), lambda b, t: (layer, 0, 0)),
                   pl.BlockSpec((None, None, CONV_LRU - 1, D_RNN), lambda b, t: (layer, b, 0, 0)),
                   kv_state, kv_state),
        out_shape=(jax.ShapeDtypeStruct((bsz, seq, D_MODEL), F32),
                   jax.ShapeDtypeStruct((DEPTH, bsz, D_RNN), F32),
                   jax.ShapeDtypeStruct((DEPTH, bsz, CONV_LRU - 1, D_RNN), F32),
                   jax.ShapeDtypeStruct((DEPTH, bsz, SUBLANES, SEG_LEN, N_KV, HEAD_DIM), F32),
                   jax.ShapeDtypeStruct((DEPTH, bsz, SUBLANES, SEG_LEN, N_KV, HEAD_DIM), F32)),
        input_output_aliases=aliases,
        scratch_shapes=[
            pltpu.VMEM((tc if permute_in else SUBLANES, D_MODEL), F32),
            pltpu.VMEM((tc, D_MODEL), BF16),
            pltpu.VMEM((tc, D_RNN), F32),
            pltpu.VMEM(((CONV_LRU - 1) * SUBLANES, D_RNN), F32),
            pltpu.VMEM((tc, Q_COLS), BF16),
            pltpu.VMEM((WINDOW + tc, KV_COLS), BF16),
            pltpu.VMEM((WINDOW + tc, KV_COLS), BF16),
            pltpu.VMEM((2, WINDOW, KV_COLS), F32),
            pltpu.VMEM((tc, 2 * D_MODEL), F32),
            pltpu.VMEM((tc, D_RNN), BF16),
            pltpu.VMEM((tc, Q_COLS), BF16),
            pltpu.VMEM((SUBLANES, D_RNN), F32),
            pltpu.VMEM((bsz, D_RNN), F32),
        ],
        compiler_params=pltpu.CompilerParams(
            dimension_semantics=("arbitrary", "arbitrary"), vmem_limit_bytes=VMEM_LIMIT_BYTES),
        name=f"prompt_mixer_l{layer}",
    )(*args)


def _pffn_kernel(x_ref, vec_ref, fcv_ref, w_up_ref, w_down_ref, y_ref, fstate_ref,
                 h_s, tail_s, acc_s, *, tc, permute_out):
    t = pl.program_id(1)
    nblk = tc // WINDOW
    ndelay = CONV_FF - 1

    @pl.when(t == 0)
    def _():
        tail_s[...] = jnp.zeros(tail_s.shape, F32)

    sub = lax.broadcasted_iota(jnp.int32, (SUBLANES, PROMPT_FF_CHUNK), 0)
    n_chunks = D_FF // PROMPT_FF_CHUNK

    def chunk_cols(c, part):
        return slice(part * D_FF + c * PROMPT_FF_CHUNK, part * D_FF + (c + 1) * PROMPT_FF_CHUNK)

    h_s[...] = _rmsnorm(x_ref[...], vec_ref[V_NORM_FFN_PRE:V_NORM_FFN_PRE + 1, :]).astype(BF16)
    hb = h_s[...]

    def up_project(c):
        return [_dot(hb, w_up_ref[:, chunk_cols(c, part)]) for part in range(2)]

    u_next = up_project(0)
    for c in range(n_chunks):
        u_cur = u_next
        if c + 1 < n_chunks:
            u_next = up_project(c + 1)
        conv = []
        for part in range(2):
            cols = chunk_cols(c, part)
            u = u_cur[part]
            bias = jnp.broadcast_to(fcv_ref[CONV_FF:CONV_FF + 1, cols], (SUBLANES, PROMPT_FF_CHUNK))
            taps = [jnp.broadcast_to(fcv_ref[j:j + 1, cols], (SUBLANES, PROMPT_FF_CHUNK)) for j in range(CONV_FF)]
            ys = []
            prev_tail = [tail_s[i * SUBLANES:(i + 1) * SUBLANES, cols] for i in range(ndelay)]
            for j in range(nblk):
                groups = _row_groups(u, j)
                delayed = _delayed_groups(groups, prev_tail, ndelay, sub)
                for k in range(SEG_LEN):
                    y = bias + taps[CONV_FF - 1] * groups[k]
                    for d in range(1, CONV_FF):
                        y = y + taps[CONV_FF - 1 - d] * delayed[d - 1][k]
                    ys.append(y)
                prev_tail = groups[SEG_LEN - ndelay:]
            conv.append(jnp.concatenate(ys, axis=0))
            tail_s[:, cols] = u[tc - ndelay * SUBLANES:tc]
            for i in range(ndelay):
                r = tc - (ndelay - i) * SUBLANES + SUBLANES - 1
                fstate_ref[i:i + 1, cols] = u[r:r + 1]
        act = (_gelu_tanh_doubled(conv[1]) * conv[0]).astype(BF16)
        part_f = _dot(act, w_down_ref[c * PROMPT_FF_CHUNK:(c + 1) * PROMPT_FF_CHUNK, :])
        if c == 0:
            acc_s[...] = part_f
        else:
            acc_s[...] += part_f
    y = x_ref[...] + _rmsnorm(acc_s[...], vec_ref[V_NORM_FFN_POST:V_NORM_FFN_POST + 1, :])
    if permute_out:
        for j in range(nblk):
            for k in range(SEG_LEN):
                y_ref[j * SUBLANES:(j + 1) * SUBLANES, k, :] = (
                    y[j * WINDOW + k * SUBLANES:j * WINDOW + (k + 1) * SUBLANES])
    else:
        y_ref[...] = y


def _prompt_ffn(x, vecs, fcv, w_up, w_down, layer, prev_states, permute_out):
    bsz, seq, _ = x.shape
    tc = PROMPT_TILE
    kern = functools.partial(_pffn_kernel, tc=tc, permute_out=permute_out)
    tile = pl.BlockSpec((None, tc, D_MODEL), lambda b, t: (b, t, 0))
    if permute_out:
        y_shape = (bsz, seq // SEG_LEN, SEG_LEN, D_MODEL)
        y_spec = pl.BlockSpec((None, tc // SEG_LEN, SEG_LEN, D_MODEL), lambda b, t: (b, t, 0, 0))
    else:
        y_shape, y_spec = (bsz, seq, D_MODEL), tile
    in_specs = [
        tile,
        _const_spec((None, VEC_ROWS, D_MODEL), (layer, 0, 0)),
        _const_spec((None, FCV_ROWS, 2 * D_FF), (layer, 0, 0)),
        _const_spec((D_MODEL, 2 * D_FF), (0, 0)),
        _const_spec((D_FF, D_MODEL), (0, 0)),
    ]
    args = [x, vecs, fcv, w_up, w_down]
    kern, in_specs, args, aliases = _with_state_aliases(kern, len(args), in_specs, args, prev_states, 1)
    return pl.pallas_call(
        kern,
        grid=(bsz, seq // tc),
        in_specs=in_specs,
        out_specs=(y_spec, pl.BlockSpec((None, None, CONV_FF - 1, 2 * D_FF), lambda b, t: (layer, b, 0, 0))),
        out_shape=(jax.ShapeDtypeStruct(y_shape, F32),
                   jax.ShapeDtypeStruct((DEPTH, bsz, CONV_FF - 1, 2 * D_FF), F32)),
        input_output_aliases=aliases,
        scratch_shapes=[
            pltpu.VMEM((tc, D_MODEL), BF16),
            pltpu.VMEM(((CONV_FF - 1) * SUBLANES, 2 * D_FF), F32),
            pltpu.VMEM((tc, D_MODEL), F32),
        ],
        compiler_params=pltpu.CompilerParams(
            dimension_semantics=("arbitrary", "arbitrary"), vmem_limit_bytes=VMEM_LIMIT_BYTES),
        name=f"prompt_ffn_l{layer}",
    )(*args)


def _cast_rows(src_ref, dst_ref, step=LRU_BLOCK):
    for r in range(0, src_ref.shape[0], step):
        dst_ref[r:r + step, :] = src_ref[r:r + step, :].astype(BF16)


def _smix_in_kernel(xs_ref, h0_ref, cprev_ref, vec_ref, w_in_f32_ref, wri_ref,
                    q_ref, k_ref, v_ref, lru_ref, gate_ref, w_in_ref, hlast_ref, cstate_ref,
                    h_s, xr_s, *, nb, nt):
    _cast_rows(w_in_f32_ref, w_in_ref)
    for t in range(nt):
        h_s[t * nb:(t + 1) * nb, :] = _rmsnorm(
            xs_ref[:, t, :], vec_ref[V_NORM_MIX_PRE:V_NORM_MIX_PRE + 1, :]).astype(BF16)
    hb = h_s[...]
    xr_s[...] = _dot(hb, w_in_ref[:, 0:C_Q])
    qf = _dot(hb, w_in_ref[:, C_Q:C_K]) * (HEAD_DIM ** -0.5)
    kf = _dot(hb, w_in_ref[:, C_K:C_V])
    vf = _dot(hb, w_in_ref[:, C_V:C_G])
    for t in range(nt):
        rows = slice(t * nb, (t + 1) * nb)
        for h in range(N_HEADS):
            q_ref[:, t * N_HEADS + h, :] = qf[rows, h * HEAD_DIM:(h + 1) * HEAD_DIM]
        for g in range(N_KV):
            k_ref[:, t * N_KV + g, :] = kf[rows, g * HEAD_DIM:(g + 1) * HEAD_DIM]
            v_ref[:, t * N_KV + g, :] = vf[rows, g * HEAD_DIM:(g + 1) * HEAD_DIM]
    gate_ref[...] = _sigmoid(_dot(hb, w_in_ref[:, C_G:IN_COLS]))

    npre = CONV_LRU - 1
    for t in range(nt - npre, nt):
        cstate_ref[t - (nt - npre)] = xr_s[t * nb:(t + 1) * nb, :]

    for n in range(N_LRU_BLOCKS):
        cb = slice(n * LRU_BLOCK, (n + 1) * LRU_BLOCK)
        xx = [cprev_ref[j, :, cb] for j in range(npre)]
        xx += [xr_s[t * nb:(t + 1) * nb, cb] for t in range(nt)]
        xcs = []
        for t in range(nt):
            xc = vec_ref[V_CONV_B:V_CONV_B + 1, cb]
            for j in range(CONV_LRU):
                xc = xc + vec_ref[V_CONV_W + j:V_CONV_W + j + 1, cb] * xx[t + j]
            xcs.append(xc)
        a, ix, mult = _lru_coeffs(jnp.concatenate(xcs, axis=0), vec_ref, wri_ref, n)
        b = mult * ix
        h = h0_ref[:, cb]
        for t in range(nt):
            rows = slice(t * nb, (t + 1) * nb)
            h = a[rows] * h + b[rows]
            lru_ref[rows, cb] = h.astype(BF16)
        hlast_ref[:, cb] = h


def _sample_mixer_in(xs, h0, cprev, vecs, w_in, wri, layer, prev_states):
    nb, nt, _ = xs.shape
    assert PAST_LEN > 0
    kern = functools.partial(_smix_in_kernel, nb=nb, nt=nt)
    whole = lambda shape: pl.BlockSpec(shape, lambda i: (0,) * len(shape))
    conv_state = pl.BlockSpec((None, CONV_LRU - 1, nb, D_RNN), lambda i: (layer, 0, 0, 0))
    in_specs = [
        whole((nb, nt, D_MODEL)),
        pl.BlockSpec((None, nb, D_RNN), lambda i: (layer, 0, 0)),
        conv_state,
        _const_spec((None, VEC_ROWS, D_MODEL), (layer, 0, 0)),
        _const_spec((None, D_MODEL, IN_COLS), (layer, 0, 0)),
        _const_spec((None, N_LRU_BLOCKS, LRU_BLOCK, 2 * LRU_BLOCK), (layer, 0, 0, 0)),
    ]
    args = [xs, h0, cprev, vecs, w_in, wri]
    kern, in_specs, args, aliases = _with_state_aliases(kern, len(args), in_specs, args, prev_states, 6)
    return pl.pallas_call(
        kern,
        grid=(1,),
        in_specs=in_specs,
        out_specs=(whole((nb, nt * N_HEADS, HEAD_DIM)),
                   whole((nb, nt * N_KV, HEAD_DIM)), whole((nb, nt * N_KV, HEAD_DIM)),
                   whole((nt * nb, D_RNN)), whole((nt * nb, 2 * D_MODEL)),
                   pl.BlockSpec((D_MODEL, IN_COLS), lambda i: (0, 0), pipeline_mode=pl.Buffered(1)),
                   pl.BlockSpec((None, nb, D_RNN), lambda i: (layer, 0, 0)), conv_state),
        out_shape=(jax.ShapeDtypeStruct((nb, nt * N_HEADS, HEAD_DIM), F32),
                   jax.ShapeDtypeStruct((nb, nt * N_KV, HEAD_DIM), F32),
                   jax.ShapeDtypeStruct((nb, nt * N_KV, HEAD_DIM), F32),
                   jax.ShapeDtypeStruct((nt * nb, D_RNN), BF16),
                   jax.ShapeDtypeStruct((nt * nb, 2 * D_MODEL), F32),
                   jax.ShapeDtypeStruct((D_MODEL, IN_COLS), BF16),
                   jax.ShapeDtypeStruct((DEPTH, nb, D_RNN), F32),
                   jax.ShapeDtypeStruct((DEPTH, CONV_LRU - 1, nb, D_RNN), F32)),
        input_output_aliases=aliases,
        scratch_shapes=[pltpu.VMEM((nt * nb, D_MODEL), BF16), pltpu.VMEM((nt * nb, D_RNN), F32)],
        compiler_params=pltpu.CompilerParams(
            dimension_semantics=("arbitrary",), vmem_limit_bytes=VMEM_LIMIT_BYTES),
        name=f"sample_mixer_in_l{layer}",
    )(*args)


def _sattn_kernel(q_ref, kn_ref, vn_ref, ck_ref, cv_ref, tab_ref, attn_ref, sk_ref, sv_ref,
                  kc_s, vc_s, *, bt, nt):
    n_cache = WINDOW * N_KV
    n_new = nt * N_KV
    for u in range(SAMPLE_ATTN_UNROLL):
        kc_s[u, n_cache + n_new:SAMPLE_KEYS, :] = jnp.zeros((SAMPLE_KEYS - n_cache - n_new, HEAD_DIM), F32)
        vc_s[u, n_cache + n_new:SAMPLE_KEYS, :] = jnp.zeros((SAMPLE_KEYS - n_cache - n_new, HEAD_DIM), F32)

    def body(i, _):
        seqs = [i * SAMPLE_ATTN_UNROLL + u for u in range(SAMPLE_ATTN_UNROLL)]
        scores = []
        for u, b in enumerate(seqs):
            for c_ref, n_ref, s_ref, scr in ((ck_ref, kn_ref, sk_ref, kc_s.at[u]), (cv_ref, vn_ref, sv_ref, vc_s.at[u])):
                scr[0:n_cache, :] = c_ref[b]
                scr[n_cache:n_cache + n_new, :] = n_ref[b]
                s_ref[b, 0:n_cache - n_new, :] = c_ref[b, n_new:n_cache, :]
                s_ref[b, n_cache - n_new:n_cache, :] = n_ref[b]
            scores.append(_dot_nt(q_ref[b].astype(BF16), kc_s[u].astype(BF16)))
        probs = []
        for s in scores:
            s = s + tab_ref[...]
            p = jnp.exp(s - jnp.max(s, axis=-1, keepdims=True))
            probs.append((p.astype(BF16), 1.0 / jnp.sum(p, axis=-1, keepdims=True)))
        for u, b in enumerate(seqs):
            p, inv = probs[u]
            attn_ref[b] = _dot(p, vc_s[u].astype(BF16)) * inv
        return 0

    lax.fori_loop(0, bt // SAMPLE_ATTN_UNROLL, body, 0)


def _sample_attention(q, kn, vn, cache_k, cache_v, stab, layer, prev_states):
    nb, rows, _ = q.shape
    nt = rows // N_HEADS
    bt = SAMPLE_BATCH_TILE
    kern = functools.partial(_sattn_kernel, bt=bt, nt=nt)
    cache_spec = pl.BlockSpec((None, bt, WINDOW * N_KV, HEAD_DIM), lambda i: (layer, i, 0, 0))
    new_spec = pl.BlockSpec((bt, nt * N_KV, HEAD_DIM), lambda i: (i, 0, 0))
    q_spec = pl.BlockSpec((bt, rows, HEAD_DIM), lambda i: (i, 0, 0))
    in_specs = [q_spec, new_spec, new_spec, cache_spec, cache_spec,
                _const_spec((None, rows, SAMPLE_KEYS), (layer, 0, 0))]
    args = [q, kn, vn, cache_k, cache_v, stab]
    kern, in_specs, args, aliases = _with_state_aliases(kern, len(args), in_specs, args, prev_states, 1)
    return pl.pallas_call(
        kern,
        grid=(nb // bt,),
        in_specs=in_specs,
        out_specs=(q_spec, cache_spec, cache_spec),
        out_shape=(jax.ShapeDtypeStruct((nb, rows, HEAD_DIM), F32),
                   jax.ShapeDtypeStruct((DEPTH, nb, WINDOW * N_KV, HEAD_DIM), F32),
                   jax.ShapeDtypeStruct((DEPTH, nb, WINDOW * N_KV, HEAD_DIM), F32)),
        input_output_aliases=aliases,
        scratch_shapes=[pltpu.VMEM((SAMPLE_ATTN_UNROLL, SAMPLE_KEYS, HEAD_DIM), F32),
                        pltpu.VMEM((SAMPLE_ATTN_UNROLL, SAMPLE_KEYS, HEAD_DIM), F32)],
        compiler_params=pltpu.CompilerParams(
            dimension_semantics=("arbitrary",), vmem_limit_bytes=VMEM_LIMIT_BYTES),
        name=f"sample_attention_l{layer}",
    )(*args)


def _smix_out_kernel(xs_ref, lru_ref, attn_ref, gate_ref, vec_ref, w_lo_f32_ref, w_ao_f32_ref, w_out_f32_ref,
                     xmid_ref, w_lo_ref, w_ao_ref, w_out_ref, a_s, *, nb, nt):
    _cast_rows(w_lo_f32_ref, w_lo_ref)
    _cast_rows(w_ao_f32_ref, w_ao_ref)
    _cast_rows(w_out_f32_ref, w_out_ref)
    for t in range(nt):
        for h in range(N_HEADS):
            a_s[t * nb:(t + 1) * nb, h * HEAD_DIM:(h + 1) * HEAD_DIM] = attn_ref[:, t * N_HEADS + h, :].astype(BF16)
    merged = (gate_ref[:, 0:D_MODEL] * _dot(lru_ref[...], w_lo_ref[...])
              + gate_ref[:, D_MODEL:2 * D_MODEL] * _dot(a_s[...], w_ao_ref[...]))
    m = _dot(merged.astype(BF16), w_out_ref[...])
    mn = _rmsnorm(m, vec_ref[V_NORM_MIX_POST:V_NORM_MIX_POST + 1, :])
    for t in range(nt):
        rows = slice(t * nb, (t + 1) * nb)
        xmid_ref[rows, :] = xs_ref[:, t, :] + mn[rows]


def _sample_mixer_out(xs, lru, attn, gates, vecs, w_lo, w_ao, w_out, layer):
    nb, nt, _ = xs.shape
    kern = functools.partial(_smix_out_kernel, nb=nb, nt=nt)
    whole = lambda shape: pl.BlockSpec(shape, lambda i: (0,) * len(shape))
    return pl.pallas_call(
        kern,
        grid=(1,),
        in_specs=[
            whole(xs.shape), whole(lru.shape), whole(attn.shape), whole(gates.shape),
            _const_spec((None, VEC_ROWS, D_MODEL), (layer, 0, 0)),
            _const_spec((None, D_RNN, D_MODEL), (layer, 0, 0)),
            _const_spec((None, Q_COLS, D_MODEL), (layer, 0, 0)),
            _const_spec((None, D_MODEL, D_MODEL), (layer, 0, 0)),
        ],
        out_specs=(whole((nt * nb, D_MODEL)),) + (whole((D_MODEL, D_MODEL)),) * 3,
        out_shape=(jax.ShapeDtypeStruct((nt * nb, D_MODEL), F32),)
        + (jax.ShapeDtypeStruct((D_MODEL, D_MODEL), BF16),) * 3,
        scratch_shapes=[pltpu.VMEM((nt * nb, Q_COLS), BF16)],
        compiler_params=pltpu.CompilerParams(
            dimension_semantics=("arbitrary",), vmem_limit_bytes=VMEM_LIMIT_BYTES),
        name=f"sample_mixer_out_l{layer}",
    )(xs, lru, attn, gates, vecs, w_lo, w_ao, w_out)


def _sffn_kernel(x_ref, vec_ref, fcv_ref, prev_ref, w_f32_ref, wd_f32_ref, ys_ref, w_ref, wd_ref, fstate_ref,
                 h_s, val_s, acc_s, *, nb, nt, nck):
    c = pl.program_id(0)

    @pl.when(c == 0)
    def _():
        h_s[...] = _rmsnorm(x_ref[...], vec_ref[V_NORM_FFN_PRE:V_NORM_FFN_PRE + 1, :]).astype(BF16)
        acc_s[...] = jnp.zeros(acc_s.shape, F32)

    _cast_rows(w_f32_ref, w_ref)
    u = _dot(h_s[...], w_ref[...])
    uu = [prev_ref[:, j, :] for j in range(CONV_FF - 1)] + [u[t * nb:(t + 1) * nb] for t in range(nt)]
    ys = []
    for t in range(nt):
        y = fcv_ref[CONV_FF:CONV_FF + 1, :]
        for j in range(CONV_FF):
            y = y + fcv_ref[j:j + 1, :] * uu[t + j]
        ys.append(y)
    conv = jnp.concatenate(ys, axis=0)
    for j in range(CONV_FF - 1):
        fstate_ref[:, j, :] = uu[nt + j]

    @pl.when(c < nck)
    def _():
        val_s[c] = conv

    @pl.when(c >= nck)
    def _():
        act = (_gelu_tanh_doubled(conv) * val_s[c - nck]).astype(BF16)
        _cast_rows(wd_f32_ref, wd_ref)
        acc_s[...] += _dot(act, wd_ref[...])

    @pl.when(c == 2 * nck - 1)
    def _():
        y = x_ref[...] + _rmsnorm(acc_s[...], vec_ref[V_NORM_FFN_POST:V_NORM_FFN_POST + 1, :])
        for t in range(nt):
            ys_ref[:, t, :] = y[t * nb:(t + 1) * nb]


def _sample_ffn(xmid, fprev, vecs, fcv, w_up, w_down, layer, nb, prev_states):
    nt = xmid.shape[0] // nb
    ck = FF_CHUNK
    nck = D_FF // ck
    kern = functools.partial(_sffn_kernel, nb=nb, nt=nt, nck=nck)
    state_spec = pl.BlockSpec((None, nb, CONV_FF - 1, ck), lambda c: (layer, 0, 0, c))
    in_specs = [
        _const_spec((nt * nb, D_MODEL), (0, 0)),
        _const_spec((None, VEC_ROWS, D_MODEL), (layer, 0, 0)),
        pl.BlockSpec((None, FCV_ROWS, ck), lambda c: (layer, 0, c)),
        state_spec,
        pl.BlockSpec((None, D_MODEL, ck), lambda c: (layer, 0, c)),
        pl.BlockSpec((None, ck, D_MODEL), lambda c: (layer, jnp.maximum(c - nck, 0), 0)),
    ]
    args = [xmid, vecs, fcv, fprev, w_up, w_down]
    kern, in_specs, args, aliases = _with_state_aliases(kern, len(args), in_specs, args, prev_states, 3)
    return pl.pallas_call(
        kern,
        grid=(2 * nck,),
        in_specs=in_specs,
        out_specs=(pl.BlockSpec((nb, nt, D_MODEL), lambda c: (0, 0, 0)),
                   pl.BlockSpec((D_MODEL, ck), lambda c: (0, c)),
                   pl.BlockSpec((ck, D_MODEL), lambda c: (jnp.maximum(c - nck, 0), 0)),
                   state_spec),
        out_shape=(jax.ShapeDtypeStruct((nb, nt, D_MODEL), F32),
                   jax.ShapeDtypeStruct((D_MODEL, 2 * D_FF), BF16),
                   jax.ShapeDtypeStruct((D_FF, D_MODEL), BF16),
                   jax.ShapeDtypeStruct((DEPTH, nb, CONV_FF - 1, 2 * D_FF), F32)),
        input_output_aliases=aliases,
        scratch_shapes=[pltpu.VMEM((nt * nb, D_MODEL), BF16),
                        pltpu.VMEM((nck, nt * nb, ck), F32),
                        pltpu.VMEM((nt * nb, D_MODEL), F32)],
        compiler_params=pltpu.CompilerParams(
            dimension_semantics=("arbitrary",), vmem_limit_bytes=VMEM_LIMIT_BYTES),
        name=f"sample_ffn_l{layer}",
    )(*args)


def kernel(x_prompt, x_sample, state_lru_h, state_lru_conv, cache_win_k, cache_win_v, state_ffn_conv,
           norm_mix_pre, norm_mix_post, norm_ffn_pre, norm_ffn_post, w_in, conv_lru_w, conv_lru_b,
           lru_wr, lru_br, lru_wi, lru_bi, lru_lambda, w_lru_o, w_attn_o, w_out, attn_sink, rel_bias,
           w_up, ffn_conv_w, ffn_conv_b, w_down):
    nb, nt, _ = x_sample.shape
    bp = x_prompt.shape[0]

    vecs = jnp.concatenate(
        [norm_mix_pre, norm_mix_post, conv_lru_b, lru_br, lru_bi, lru_lambda,
         conv_lru_w.reshape(DEPTH, CONV_LRU * D_RNN), norm_ffn_pre, norm_ffn_post], axis=-1
    ).reshape(DEPTH, VEC_ROWS, D_MODEL)
    fcv = jnp.concatenate([ffn_conv_w.reshape(DEPTH, CONV_FF * 2 * D_FF), ffn_conv_b], axis=-1
                          ).reshape(DEPTH, FCV_ROWS, 2 * D_FF)
    fcv = fcv * jnp.where(jnp.arange(2 * D_FF) < D_FF, 0.5, 1.0).astype(F32)
    wri_b = jnp.concatenate([lru_wr, lru_wi], axis=-1).astype(BF16)

    ptab, stab = _bias_tables(rel_bias.T, attn_sink)

    yp, xs = x_prompt, x_sample
    p_mix = p_ffn = s_mix = s_att = s_ffn = None
    cache_rows = lambda c: c.reshape(DEPTH, nb, WINDOW * N_KV, HEAD_DIM)
    for l in range(DEPTH):
        q, kn, vn, lru, gates, w_in_b, *s_mix = _sample_mixer_in(
            xs, state_lru_h, jnp.swapaxes(state_lru_conv, 1, 2), vecs, w_in, wri_b, l, s_mix)
        attn, *s_att = _sample_attention(
            q, kn, vn, cache_rows(cache_win_k), cache_rows(cache_win_v),
            stab, l, s_att)
        xmid, w_lo_b, w_ao_b, w_out_b = _sample_mixer_out(
            xs, lru, attn, gates, vecs, w_lru_o, w_attn_o, w_out, l)
        xs, w_up_b, w_down_b, *s_ffn = _sample_ffn(xmid, state_ffn_conv, vecs, fcv, w_up, w_down, l, nb, s_ffn)

        yp, *p_mix = _prompt_mixer(yp, vecs, w_in_b, wri_b, w_lo_b, w_ao_b, w_out_b, ptab, attn_sink, l, p_mix,
                                   permute_in=(l == 0))
        yp, *p_ffn = _prompt_ffn(yp, vecs, fcv, w_up_b, w_down_b, l, p_ffn, permute_out=(l == DEPTH - 1))

    p_h, p_c, p_k, p_v = p_mix
    s_h, s_c = s_mix
    s_k, s_v = s_att
    kv_shape = (DEPTH, bp, WINDOW, N_KV, HEAD_DIM)
    return (yp.reshape(x_prompt.shape), xs, p_h, p_c, p_k.reshape(kv_shape), p_v.reshape(kv_shape), p_ffn[0],
            s_h, jnp.swapaxes(s_c, 1, 2), s_k.reshape(cache_win_k.shape), s_v.reshape(cache_win_v.shape), s_ffn[0])
```

```python
import functools
import math

import numpy as np
import jax
import jax.numpy as jnp
from jax import lax
from jax.experimental import pallas as pl
from jax.experimental.pallas import tpu as pltpu

D_MODEL = 1024
DEPTH = 2
PAST_LEN = 16384
D_RNN = D_MODEL
N_LRU_BLOCKS = 8
LRU_BLOCK = D_RNN // N_LRU_BLOCKS
CONV_LRU = 4
LRU_C = 8.0
N_HEADS = 8
N_KV = 2
GROUP = N_HEADS // N_KV
HEAD_DIM = D_MODEL // N_HEADS
WINDOW = 128
N_BUCKETS = 32
MAX_EXACT = N_BUCKETS // 2
MAX_DISTANCE = 128
D_FF = 4 * D_MODEL
CONV_FF = 3
EPS = 1e-6
Q_COLS = N_HEADS * HEAD_DIM
KV_COLS = N_KV * HEAD_DIM
IN_COLS = D_RNN + Q_COLS + 2 * KV_COLS + 2 * D_MODEL
C_Q = D_RNN
C_K = C_Q + Q_COLS
C_V = C_K + KV_COLS
C_G = C_V + KV_COLS

F32 = jnp.float32
BF16 = jnp.bfloat16

SUBLANES = 8
LANES = 128
VMEM_LIMIT_BYTES = 56 * 1024 * 1024

V_NORM_MIX_PRE, V_NORM_MIX_POST, V_CONV_B, V_BR, V_BI, V_LAMBDA, V_CONV_W = 0, 1, 2, 3, 4, 5, 6
V_NORM_FFN_PRE, V_NORM_FFN_POST = 10, 11
VEC_ROWS = 12
FCV_ROWS = CONV_FF + 1

PROMPT_TILE = 512
PROJ_PIECE = 2 * KV_COLS
assert (C_K - C_Q) % PROJ_PIECE == 0 and (IN_COLS - C_G) % PROJ_PIECE == 0
PROMPT_FF_CHUNK = 2048
FF_CHUNK = 1024
SAMPLE_KEYS = N_KV * (WINDOW + 4) + SUBLANES
SINK_COL = N_KV * (WINDOW + 4)
SAMPLE_BATCH_TILE = 16
SAMPLE_ATTN_UNROLL = 16
SEG_LEN = WINDOW // SUBLANES


def _bucket_thresholds():
    d = np.arange(0, 2 * WINDOW)
    nf = np.maximum(d, 1).astype(np.float64)
    large = MAX_EXACT + (np.log(nf / MAX_EXACT) / math.log(MAX_DISTANCE / MAX_EXACT)
                         * (N_BUCKETS - MAX_EXACT)).astype(np.int64)
    bucket = np.where(d < MAX_EXACT, d, np.minimum(large, N_BUCKETS - 1))
    return tuple(int(d[bucket >= b].min()) for b in range(1, N_BUCKETS))


_BUCKET_THRESHOLDS = _bucket_thresholds()


def _dot(a, b):
    return jnp.dot(a, b, preferred_element_type=F32)


def _dot_nt(a, b):
    return lax.dot_general(a, b, (((1,), (1,)), ((), ())), preferred_element_type=F32)


def _rmsnorm(x, g):
    return x * lax.rsqrt(jnp.mean(x * x, axis=-1, keepdims=True) + EPS) * g


def _sigmoid(x):
    return 1.0 / (1.0 + jnp.exp(-x))


def _gelu_tanh_doubled(x):
    c = math.sqrt(2.0 / math.pi)
    t = jnp.tanh(x * (c + (c * 0.044715) * (x * x)))
    return x + x * t


def _const_spec(block_shape, index):
    return pl.BlockSpec(block_shape, lambda *_: index, pipeline_mode=pl.Buffered(1))


def _with_state_aliases(kern, n_in, in_specs, args, prev_states, first_state_out):
    if prev_states is None:
        return kern, list(in_specs), list(args), {}
    n = len(prev_states)

    def body(*refs):
        return kern(*refs[:n_in], *refs[n_in + n:])

    return (body, list(in_specs) + [pl.BlockSpec(memory_space=pl.ANY)] * n, list(args) + list(prev_states),
            {n_in + i: first_state_out + i for i in range(n)})


def _bucket_of(d):
    n = jnp.maximum(d, 0)
    bucket = jnp.zeros(d.shape, jnp.int32)
    for thr in _BUCKET_THRESHOLDS:
        bucket = bucket + jnp.where(n >= thr, 1, 0)
    return bucket


def _block_time(p):
    return lax.bitwise_and(p, SUBLANES - 1) * SEG_LEN + lax.shift_right_logical(p, 3)


def _table_kernel(rel_ref, sink_ref, pt_ref, st_ref):
    qi = lax.broadcasted_iota(jnp.int32, (WINDOW, 2 * WINDOW), 0)
    kj = lax.broadcasted_iota(jnp.int32, (WINDOW, 2 * WINDOW), 1)
    kpos = lax.bitwise_and(kj, WINDOW - 1)
    d = _block_time(qi) + WINDOW - (_block_time(kpos) + (kj - kpos))
    bucket = _bucket_of(d)
    in_band = jnp.where(d >= 0, jnp.where(d < WINDOW, 1, 0), 0)
    cur_only = jnp.where(kj >= WINDOW, in_band, 0)
    for h in range(N_HEADS):
        val = jnp.zeros(d.shape, F32)
        for b in range(N_BUCKETS):
            val = jnp.where(bucket == b, rel_ref[h, b], val)
        pt_ref[1, h] = jnp.where(in_band == 1, val, -jnp.inf)
        pt_ref[0, h] = jnp.where(cur_only == 1, val, -jnp.inf)

    r = lax.broadcasted_iota(jnp.int32, (4 * N_HEADS, SAMPLE_KEYS), 0)
    j = lax.broadcasted_iota(jnp.int32, (4 * N_HEADS, SAMPLE_KEYS), 1)
    t = lax.shift_right_logical(r, 3)
    hh = lax.bitwise_and(r, N_HEADS - 1)
    d = t + WINDOW - lax.shift_right_logical(j, 1)
    bucket = _bucket_of(d)
    in_band = jnp.where(d >= 0, jnp.where(d < WINDOW, 1, 0), 0)
    in_band = jnp.where(lax.bitwise_and(j, N_KV - 1) == lax.shift_right_logical(hh, 2), in_band, 0)
    val = jnp.zeros(d.shape, F32)
    for h in range(N_HEADS):
        hval = jnp.zeros(d.shape, F32)
        for b in range(N_BUCKETS):
            hval = jnp.where(bucket == b, rel_ref[h, b], hval)
        val = jnp.where(hh == h, hval, val)
    val = jnp.where(in_band == 1, val, -jnp.inf)
    for l in range(DEPTH):
        sk = jnp.zeros(d.shape, F32)
        for h in range(N_HEADS):
            sk = jnp.where(hh == h, sink_ref[l, h], sk)
        st_ref[l] = jnp.where(j == SINK_COL, sk, val)


def _bias_tables(rel_bias, attn_sink):
    smem = pl.BlockSpec(memory_space=pltpu.SMEM)
    return pl.pallas_call(
        _table_kernel,
        out_shape=(jax.ShapeDtypeStruct((2, N_HEADS, WINDOW, 2 * WINDOW), F32),
                   jax.ShapeDtypeStruct((DEPTH, 4 * N_HEADS, SAMPLE_KEYS), F32)),
        in_specs=[smem, smem],
        name="bias_tables",
    )(rel_bias, attn_sink)


def _lru_coeffs(xc, vec_ref, wri_ref, n):
    cb = slice(n * LRU_BLOCK, (n + 1) * LRU_BLOCK)
    rw = _dot(xc.astype(BF16), wri_ref[n])
    r = _sigmoid(rw[:, :LRU_BLOCK] + vec_ref[V_BR:V_BR + 1, cb])
    i = _sigmoid(rw[:, LRU_BLOCK:] + vec_ref[V_BI:V_BI + 1, cb])
    z = -vec_ref[V_LAMBDA:V_LAMBDA + 1, cb]
    softplus = jnp.maximum(z, 0.0) + jnp.log1p(jnp.exp(-jnp.abs(z)))
    log_a = (-LRU_C * softplus) * r
    a = jnp.exp(log_a)
    mult = jnp.sqrt(jnp.maximum(1.0 - a * a, 0.0))
    return a, i * xc, mult


def _delayed_groups(groups, prev_tail, ndelay, sub):
    ng = len(groups)
    wrapped = {}
    for i in range(ndelay):
        k = ng - ndelay + i
        wrapped[k] = jnp.where(sub == 0, pltpu.roll(prev_tail[i], 1, 0), pltpu.roll(groups[k], 1, 0))
    return [[groups[k - d] if k >= d else wrapped[ng + k - d] for k in range(ng)] for d in range(1, ndelay + 1)]


def _row_groups(x, block):
    return [x[block * WINDOW + k * SUBLANES:block * WINDOW + (k + 1) * SUBLANES] for k in range(SEG_LEN)]


def _pmix_kernel(x_ref, vec_ref, w_in_ref, wri_ref, w_lo_ref, w_ao_ref, w_out_ref, tab_ref, sink_ref,
                 y_ref, hlast_ref, cstate_ref, kstate_ref, vstate_ref,
                 xp_s, h_s, xr_s, xrc_s, q_s, k_s, v_s, kvf_s, g_s, lru_s, attn_s, hc_s, hl_s, *, layer, tc, permute_in):
    t = pl.program_id(1)
    nblk = tc // WINDOW
    ndelay = CONV_LRU - 1
    n_pieces = (IN_COLS - C_Q) // PROJ_PIECE
    batch_row = lax.broadcasted_iota(jnp.int32, (hl_s.shape[0], LRU_BLOCK), 0)

    @pl.when(jnp.logical_and(t == 0, pl.program_id(0) == 0))
    def _():
        hl_s[...] = jnp.zeros(hl_s.shape, F32)

    @pl.when(t == 0)
    def _():
        xrc_s[...] = jnp.zeros(xrc_s.shape, F32)
        k_s[0:WINDOW, :] = jnp.zeros((WINDOW, KV_COLS), BF16)
        v_s[0:WINDOW, :] = jnp.zeros((WINDOW, KV_COLS), BF16)
        hc_s[...] = jnp.zeros(hc_s.shape, F32)

    if permute_in:
        for j in range(nblk):
            for k in range(SEG_LEN):
                xp_s[j * WINDOW + k * SUBLANES:j * WINDOW + (k + 1) * SUBLANES, :] = (
                    x_ref[j * SUBLANES:(j + 1) * SUBLANES, k, :])
        x_tile = xp_s
    else:
        x_tile = x_ref

    h_s[...] = _rmsnorm(x_tile[...], vec_ref[V_NORM_MIX_PRE:V_NORM_MIX_PRE + 1, :]).astype(BF16)
    hb = h_s[...]
    xr_s[...] = _dot(hb, w_in_ref[:, 0:C_Q])

    def project_piece(i):
        c0 = C_Q + i * PROJ_PIECE
        z = _dot(hb, w_in_ref[:, c0:c0 + PROJ_PIECE])
        if c0 < C_K:
            q_s[:, c0 - C_Q:c0 - C_Q + PROJ_PIECE] = (z * (HEAD_DIM ** -0.5)).astype(BF16)
        elif c0 == C_K:
            k_s[WINDOW:WINDOW + tc, :] = z[:, 0:KV_COLS].astype(BF16)
            v_s[WINDOW:WINDOW + tc, :] = z[:, KV_COLS:2 * KV_COLS].astype(BF16)
            kvf_s[0] = z[tc - WINDOW:tc, 0:KV_COLS]
            kvf_s[1] = z[tc - WINDOW:tc, KV_COLS:2 * KV_COLS]
        else:
            g_s[:, c0 - C_G:c0 - C_G + PROJ_PIECE] = _sigmoid(z)

    for i in range(ndelay):
        r = tc - (ndelay - i) * SUBLANES + SUBLANES - 1
        cstate_ref[i:i + 1, :] = xr_s[r:r + 1, :]

    sub = lax.broadcasted_iota(jnp.int32, (SUBLANES, LRU_BLOCK), 0)
    seq_start = (sub + t) == 0
    for n in range(N_LRU_BLOCKS):
        cb = slice(n * LRU_BLOCK, (n + 1) * LRU_BLOCK)
        bias = jnp.broadcast_to(vec_ref[V_CONV_B:V_CONV_B + 1, cb], (SUBLANES, LRU_BLOCK))
        taps = [jnp.broadcast_to(vec_ref[V_CONV_W + j:V_CONV_W + j + 1, cb], (SUBLANES, LRU_BLOCK))
                for j in range(CONV_LRU)]
        xcs = []
        prev_tail = [xrc_s[i * SUBLANES:(i + 1) * SUBLANES, cb] for i in range(ndelay)]
        for j in range(nblk):
            groups = [xr_s[j * WINDOW + k * SUBLANES:j * WINDOW + (k + 1) * SUBLANES, cb] for k in range(SEG_LEN)]
            delayed = _delayed_groups(groups, prev_tail, ndelay, sub)
            for k in range(SEG_LEN):
                xc = bias + taps[CONV_LRU - 1] * groups[k]
                for d in range(1, CONV_LRU):
                    xc = xc + taps[CONV_LRU - 1 - d] * delayed[d - 1][k]
                xcs.append(xc)
            prev_tail = groups[SEG_LEN - ndelay:]
        a, ix, mult = _lru_coeffs(jnp.concatenate(xcs, axis=0), vec_ref, wri_ref, n)
        b = mult * ix

        carry = hc_s[0:1, cb]
        hs = []
        for j in range(nblk):
            ag, bg, ig = _row_groups(a, j), _row_groups(b, j), _row_groups(ix, j)
            if j == 0:
                bg[0] = jnp.where(seq_start, ig[0], bg[0])
            acc_a, acc_b = [ag[0]], [bg[0]]
            for k in range(1, SEG_LEN):
                acc_b.append(ag[k] * acc_b[-1] + bg[k])
                acc_a.append(ag[k] * acc_a[-1])
            seg_a, seg_b = acc_a[-1], acc_b[-1]
            for s in (1, 2, 4):
                ash = jnp.where(sub >= s, pltpu.roll(seg_a, s, 0), 1.0)
                bsh = jnp.where(sub >= s, pltpu.roll(seg_b, s, 0), 0.0)
                seg_b = seg_a * bsh + seg_b
                seg_a = seg_a * ash
            h_end = seg_a * carry + seg_b
            h_in = jnp.where(sub == 0, carry, pltpu.roll(h_end, 1, 0))
            hs += [acc_a[k] * h_in + acc_b[k] for k in range(SEG_LEN)]
            carry = h_end[SUBLANES - 1:SUBLANES, :]
        lru_s[:, cb] = jnp.concatenate(hs, axis=0).astype(BF16)
        hc_s[0:1, cb] = carry
        hl_s[:, cb] = jnp.where(batch_row == pl.program_id(0), carry, hl_s[:, cb])
        if n < n_pieces:
            project_piece(n)
    for i in range(N_LRU_BLOCKS, n_pieces):
        project_piece(i)
    hlast_ref[...] = hl_s[...]

    first = jnp.where(t == 0, 0, 1)

    def scores(j, g):
        q4 = jnp.concatenate(
            [q_s[j * WINDOW:(j + 1) * WINDOW, (g * GROUP + hg) * HEAD_DIM:(g * GROUP + hg + 1) * HEAD_DIM]
             for hg in range(GROUP)], axis=0)
        return _dot_nt(q4, k_s[j * WINDOW:(j + 2) * WINDOW, g * HEAD_DIM:(g + 1) * HEAD_DIM])

    def attend(j, g, s):
        variant = first if j == 0 else 1
        ps, invs = [], []
        for hg in range(GROUP):
            head = g * GROUP + hg
            sh = s[hg * WINDOW:(hg + 1) * WINDOW] + tab_ref[variant, head]
            sk = sink_ref[layer, head]
            m = jnp.maximum(jnp.max(sh, axis=-1, keepdims=True), sk)
            p = jnp.exp(sh - m)
            den = jnp.sum(p, axis=-1, keepdims=True) + jnp.exp(sk - m)
            ps.append(p.astype(BF16))
            invs.append(1.0 / den)
        o4 = _dot(jnp.concatenate(ps, axis=0),
                  v_s[j * WINDOW:(j + 2) * WINDOW, g * HEAD_DIM:(g + 1) * HEAD_DIM])
        for hg in range(GROUP):
            head = g * GROUP + hg
            attn_s[j * WINDOW:(j + 1) * WINDOW, head * HEAD_DIM:(head + 1) * HEAD_DIM] = (
                o4[hg * WINDOW:(hg + 1) * WINDOW] * invs[hg]).astype(BF16)

    pairs = [(j, g) for j in range(nblk) for g in range(N_KV)]
    s_next = scores(*pairs[0])
    for i, (j, g) in enumerate(pairs):
        s_cur = s_next
        if i + 1 < len(pairs):
            s_next = scores(*pairs[i + 1])
        attend(j, g, s_cur)

    xrc_s[...] = xr_s[tc - ndelay * SUBLANES:tc, :]
    k_s[0:WINDOW, :] = k_s[tc:tc + WINDOW, :]
    v_s[0:WINDOW, :] = v_s[tc:tc + WINDOW, :]

    merged = (g_s[:, 0:D_MODEL] * _dot(lru_s[...], w_lo_ref[...])
              + g_s[:, D_MODEL:2 * D_MODEL] * _dot(attn_s[...], w_ao_ref[...]))
    m = _dot(merged.astype(BF16), w_out_ref[...])
    y_ref[...] = x_tile[...] + _rmsnorm(m, vec_ref[V_NORM_MIX_POST:V_NORM_MIX_POST + 1, :])

    @pl.when(t == pl.num_programs(1) - 1)
    def _():
        for k in range(SEG_LEN):
            rows = slice(k * SUBLANES, (k + 1) * SUBLANES)
            for g in range(N_KV):
                kstate_ref[:, k, g, :] = kvf_s[0, rows, g * HEAD_DIM:(g + 1) * HEAD_DIM]
                vstate_ref[:, k, g, :] = kvf_s[1, rows, g * HEAD_DIM:(g + 1) * HEAD_DIM]


def _prompt_mixer(x, vecs, w_in, wri, w_lo, w_ao, w_out, tab, sink, layer, prev_states, permute_in):
    bsz, seq, _ = x.shape
    tc = PROMPT_TILE
    kern = functools.partial(_pmix_kernel, layer=layer, tc=tc, permute_in=permute_in)
    tile = pl.BlockSpec((None, tc, D_MODEL), lambda b, t: (b, t, 0))
    kv_state = pl.BlockSpec((None, None, SUBLANES, SEG_LEN, N_KV, HEAD_DIM), lambda b, t: (layer, b, 0, 0, 0, 0))
    if permute_in:
        x = x.reshape(bsz, seq // SEG_LEN, SEG_LEN, D_MODEL)
        x_spec = pl.BlockSpec((None, tc // SEG_LEN, SEG_LEN, D_MODEL), lambda b, t: (b, t, 0, 0))
    else:
        x_spec = tile
    in_specs = [
        x_spec,
        _const_spec((None, VEC_ROWS, D_MODEL), (layer, 0, 0)),
        _const_spec((D_MODEL, IN_COLS), (0, 0)),
        _const_spec((None, N_LRU_BLOCKS, LRU_BLOCK, 2 * LRU_BLOCK), (layer, 0, 0, 0)),
        _const_spec((D_RNN, D_MODEL), (0, 0)),
        _const_spec((Q_COLS, D_MODEL), (0, 0)),
        _const_spec((D_MODEL, D_MODEL), (0, 0)),
        _const_spec((2, N_HEADS, WINDOW, 2 * WINDOW), (0, 0, 0, 0)),
        pl.BlockSpec(memory_space=pltpu.SMEM),
    ]
    args = [x, vecs, w_in, wri, w_lo, w_ao, w_out, tab, sink]
    kern, in_specs, args, aliases = _with_state_aliases(kern, len(args), in_specs, args, prev_states, 1)
    return pl.pallas_call(
        kern,
        grid=(bsz, seq // tc),
        in_specs=in_specs,
        out_specs=(tile,
                   pl.BlockSpec((None, bsz, D_RNN), lambda b, t: (layer, 0, 0)),
                   pl.BlockSpec((None, None, CONV_LRU - 1, D_RNN), lambda b, t: (layer, b, 0, 0)),
                   kv_state, kv_state),
        out_shape=(jax.ShapeDtypeStruct((bsz, seq, D_MODEL), F32),
                   jax.ShapeDtypeStruct((DEPTH, bsz, D_RNN), F32),
                   jax.ShapeDtypeStruct((DEPTH, bsz, CONV_LRU - 1, D_RNN), F32),
                   jax.ShapeDtypeStruct((DEPTH, bsz, SUBLANES, SEG_LEN, N_KV, HEAD_DIM), F32),
                   jax.ShapeDtypeStruct((DEPTH, bsz, SUBLANES, SEG_LEN, N_KV, HEAD_DIM), F32)),
        input_output_aliases=aliases,
        scratch_shapes=[
            pltpu.VMEM((tc if permute_in else SUBLANES, D_MODEL), F32),
            pltpu.VMEM((tc, D_MODEL), BF16),
            pltpu.VMEM((tc, D_RNN), F32),
            pltpu.VMEM(((CONV_LRU - 1) * SUBLANES, D_RNN), F32),
            pltpu.VMEM((tc, Q_COLS), BF16),
            pltpu.VMEM((WINDOW + tc, KV_COLS), BF16),
            pltpu.VMEM((WINDOW + tc, KV_COLS), BF16),
            pltpu.VMEM((2, WINDOW, KV_COLS), F32),
            pltpu.VMEM((tc, 2 * D_MODEL), F32),
            pltpu.VMEM((tc, D_RNN), BF16),
            pltpu.VMEM((tc, Q_COLS), BF16),
            pltpu.VMEM((SUBLANES, D_RNN), F32),
            pltpu.VMEM((bsz, D_RNN), F32),
        ],
        compiler_params=pltpu.CompilerParams(
            dimension_semantics=("arbitrary", "arbitrary"), vmem_limit_bytes=VMEM_LIMIT_BYTES),
        name=f"prompt_mixer_l{layer}",
    )(*args)


def _pffn_kernel(x_ref, vec_ref, fcv_ref, w_up_ref, w_down_ref, y_ref, fstate_ref,
                 h_s, tail_s, acc_s, *, tc, permute_out):
    t = pl.program_id(1)
    nblk = tc // WINDOW
    ndelay = CONV_FF - 1

    @pl.when(t == 0)
    def _():
        tail_s[...] = jnp.zeros(tail_s.shape, F32)

    sub = lax.broadcasted_iota(jnp.int32, (SUBLANES, PROMPT_FF_CHUNK), 0)
    n_chunks = D_FF // PROMPT_FF_CHUNK

    def chunk_cols(c, part):
        return slice(part * D_FF + c * PROMPT_FF_CHUNK, part * D_FF + (c + 1) * PROMPT_FF_CHUNK)

    h_s[...] = _rmsnorm(x_ref[...], vec_ref[V_NORM_FFN_PRE:V_NORM_FFN_PRE + 1, :]).astype(BF16)
    hb = h_s[...]

    def up_project(c):
        return [_dot(hb, w_up_ref[:, chunk_cols(c, part)]) for part in range(2)]

    u_next = up_project(0)
    for c in range(n_chunks):
        u_cur = u_next
        if c + 1 < n_chunks:
            u_next = up_project(c + 1)
        conv = []
        for part in range(2):
            cols = chunk_cols(c, part)
            u = u_cur[part]
            bias = jnp.broadcast_to(fcv_ref[CONV_FF:CONV_FF + 1, cols], (SUBLANES, PROMPT_FF_CHUNK))
            taps = [jnp.broadcast_to(fcv_ref[j:j + 1, cols], (SUBLANES, PROMPT_FF_CHUNK)) for j in range(CONV_FF)]
            ys = []
            prev_tail = [tail_s[i * SUBLANES:(i + 1) * SUBLANES, cols] for i in range(ndelay)]
            for j in range(nblk):
                groups = _row_groups(u, j)
                delayed = _delayed_groups(groups, prev_tail, ndelay, sub)
                for k in range(SEG_LEN):
                    y = bias + taps[CONV_FF - 1] * groups[k]
                    for d in range(1, CONV_FF):
                        y = y + taps[CONV_FF - 1 - d] * delayed[d - 1][k]
                    ys.append(y)
                prev_tail = groups[SEG_LEN - ndelay:]
            conv.append(jnp.concatenate(ys, axis=0))
            tail_s[:, cols] = u[tc - ndelay * SUBLANES:tc]
            for i in range(ndelay):
                r = tc - (ndelay - i) * SUBLANES + SUBLANES - 1
                fstate_ref[i:i + 1, cols] = u[r:r + 1]
        act = (_gelu_tanh_doubled(conv[1]) * conv[0]).astype(BF16)
        part_f = _dot(act, w_down_ref[c * PROMPT_FF_CHUNK:(c + 1) * PROMPT_FF_CHUNK, :])
        if c == 0:
            acc_s[...] = part_f
        else:
            acc_s[...] += part_f
    y = x_ref[...] + _rmsnorm(acc_s[...], vec_ref[V_NORM_FFN_POST:V_NORM_FFN_POST + 1, :])
    if permute_out:
        for j in range(nblk):
            for k in range(SEG_LEN):
                y_ref[j * SUBLANES:(j + 1) * SUBLANES, k, :] = (
                    y[j * WINDOW + k * SUBLANES:j * WINDOW + (k + 1) * SUBLANES])
    else:
        y_ref[...] = y


def _prompt_ffn(x, vecs, fcv, w_up, w_down, layer, prev_states, permute_out):
    bsz, seq, _ = x.shape
    tc = PROMPT_TILE
    kern = functools.partial(_pffn_kernel, tc=tc, permute_out=permute_out)
    tile = pl.BlockSpec((None, tc, D_MODEL), lambda b, t: (b, t, 0))
    if permute_out:
        y_shape = (bsz, seq // SEG_LEN, SEG_LEN, D_MODEL)
        y_spec = pl.BlockSpec((None, tc // SEG_LEN, SEG_LEN, D_MODEL), lambda b, t: (b, t, 0, 0))
    else:
        y_shape, y_spec = (bsz, seq, D_MODEL), tile
    in_specs = [
        tile,
        _const_spec((None, VEC_ROWS, D_MODEL), (layer, 0, 0)),
        _const_spec((None, FCV_ROWS, 2 * D_FF), (layer, 0, 0)),
        _const_spec((D_MODEL, 2 * D_FF), (0, 0)),
        _const_spec((D_FF, D_MODEL), (0, 0)),
    ]
    args = [x, vecs, fcv, w_up, w_down]
    kern, in_specs, args, aliases = _with_state_aliases(kern, len(args), in_specs, args, prev_states, 1)
    return pl.pallas_call(
        kern,
        grid=(bsz, seq // tc),
        in_specs=in_specs,
        out_specs=(y_spec, pl.BlockSpec((None, None, CONV_FF - 1, 2 * D_FF), lambda b, t: (layer, b, 0, 0))),
        out_shape=(jax.ShapeDtypeStruct(y_shape, F32),
                   jax.ShapeDtypeStruct((DEPTH, bsz, CONV_FF - 1, 2 * D_FF), F32)),
        input_output_aliases=aliases,
        scratch_shapes=[
            pltpu.VMEM((tc, D_MODEL), BF16),
            pltpu.VMEM(((CONV_FF - 1) * SUBLANES, 2 * D_FF), F32),
            pltpu.VMEM((tc, D_MODEL), F32),
        ],
        compiler_params=pltpu.CompilerParams(
            dimension_semantics=("arbitrary", "arbitrary"), vmem_limit_bytes=VMEM_LIMIT_BYTES),
        name=f"prompt_ffn_l{layer}",
    )(*args)


def _cast_rows(src_ref, dst_ref, step=LRU_BLOCK):
    for r in range(0, src_ref.shape[0], step):
        dst_ref[r:r + step, :] = src_ref[r:r + step, :].astype(BF16)


def _smix_in_kernel(xs_ref, h0_ref, cprev_ref, vec_ref, w_in_f32_ref, wri_ref,
                    q_ref, k_ref, v_ref, lru_ref, gate_ref, w_in_ref, hlast_ref, cstate_ref,
                    h_s, z_s, *, nb, nt):
    c = pl.program_id(0)

    @pl.when(c == 0)
    def _():
        for t in range(nt):
            h_s[t * nb:(t + 1) * nb, :] = _rmsnorm(
                xs_ref[:, t, :], vec_ref[V_NORM_MIX_PRE:V_NORM_MIX_PRE + 1, :]).astype(BF16)

    _cast_rows(w_in_f32_ref, w_in_ref)
    z_s[c] = _dot(h_s[...], w_in_ref[...])

    @pl.when(c == pl.num_programs(0) - 1)
    def _():
        _smix_in_finish(h0_ref, cprev_ref, vec_ref, wri_ref, q_ref, k_ref, v_ref, lru_ref, gate_ref,
                        hlast_ref, cstate_ref, z_s, nb, nt)


def _smix_in_finish(h0_ref, cprev_ref, vec_ref, wri_ref, q_ref, k_ref, v_ref, lru_ref, gate_ref,
                    hlast_ref, cstate_ref, z_s, nb, nt):
    def proj(rows, c0, width):
        piece, off = divmod(c0, PROJ_PIECE)
        assert off + width <= PROJ_PIECE
        return z_s[piece, rows, off:off + width]

    for t in range(nt):
        rows = slice(t * nb, (t + 1) * nb)
        for h in range(N_HEADS):
            q_ref[:, t * N_HEADS + h, :] = proj(rows, C_Q + h * HEAD_DIM, HEAD_DIM) * (HEAD_DIM ** -0.5)
        for g in range(N_KV):
            k_ref[:, t * N_KV + g, :] = proj(rows, C_K + g * HEAD_DIM, HEAD_DIM)
            v_ref[:, t * N_KV + g, :] = proj(rows, C_V + g * HEAD_DIM, HEAD_DIM)
    for c0 in range(C_G, IN_COLS, PROJ_PIECE):
        gate_ref[:, c0 - C_G:c0 - C_G + PROJ_PIECE] = _sigmoid(z_s[c0 // PROJ_PIECE])

    npre = CONV_LRU - 1
    for t in range(nt - npre, nt):
        for c0 in range(0, D_RNN, PROJ_PIECE):
            cstate_ref[t - (nt - npre), :, c0:c0 + PROJ_PIECE] = z_s[c0 // PROJ_PIECE, t * nb:(t + 1) * nb, :]

    for n in range(N_LRU_BLOCKS):
        cb = slice(n * LRU_BLOCK, (n + 1) * LRU_BLOCK)
        xx = [cprev_ref[j, :, cb] for j in range(npre)]
        xx += [proj(slice(t * nb, (t + 1) * nb), n * LRU_BLOCK, LRU_BLOCK) for t in range(nt)]
        xcs = []
        for t in range(nt):
            xc = vec_ref[V_CONV_B:V_CONV_B + 1, cb]
            for j in range(CONV_LRU):
                xc = xc + vec_ref[V_CONV_W + j:V_CONV_W + j + 1, cb] * xx[t + j]
            xcs.append(xc)
        a, ix, mult = _lru_coeffs(jnp.concatenate(xcs, axis=0), vec_ref, wri_ref, n)
        b = mult * ix
        h = h0_ref[:, cb]
        for t in range(nt):
            rows = slice(t * nb, (t + 1) * nb)
            h = a[rows] * h + b[rows]
            lru_ref[rows, cb] = h.astype(BF16)
        hlast_ref[:, cb] = h


def _sample_mixer_in(xs, h0, cprev, vecs, w_in, wri, layer, prev_states):
    nb, nt, _ = xs.shape
    assert PAST_LEN > 0
    kern = functools.partial(_smix_in_kernel, nb=nb, nt=nt)
    whole = lambda shape: pl.BlockSpec(shape, lambda i: (0,) * len(shape))
    conv_state = pl.BlockSpec((None, CONV_LRU - 1, nb, D_RNN), lambda i: (layer, 0, 0, 0))
    in_specs = [
        whole((nb, nt, D_MODEL)),
        pl.BlockSpec((None, nb, D_RNN), lambda i: (layer, 0, 0)),
        conv_state,
        _const_spec((None, VEC_ROWS, D_MODEL), (layer, 0, 0)),
        pl.BlockSpec((None, D_MODEL, PROJ_PIECE), lambda i: (layer, 0, i)),
        _const_spec((None, N_LRU_BLOCKS, LRU_BLOCK, 2 * LRU_BLOCK), (layer, 0, 0, 0)),
    ]
    args = [xs, h0, cprev, vecs, w_in, wri]
    kern, in_specs, args, aliases = _with_state_aliases(kern, len(args), in_specs, args, prev_states, 6)
    return pl.pallas_call(
        kern,
        grid=(IN_COLS // PROJ_PIECE,),
        in_specs=in_specs,
        out_specs=(whole((nb, nt * N_HEADS, HEAD_DIM)),
                   whole((nb, nt * N_KV, HEAD_DIM)), whole((nb, nt * N_KV, HEAD_DIM)),
                   whole((nt * nb, D_RNN)), whole((nt * nb, 2 * D_MODEL)),
                   pl.BlockSpec((D_MODEL, PROJ_PIECE), lambda i: (0, i)),
                   pl.BlockSpec((None, nb, D_RNN), lambda i: (layer, 0, 0)), conv_state),
        out_shape=(jax.ShapeDtypeStruct((nb, nt * N_HEADS, HEAD_DIM), F32),
                   jax.ShapeDtypeStruct((nb, nt * N_KV, HEAD_DIM), F32),
                   jax.ShapeDtypeStruct((nb, nt * N_KV, HEAD_DIM), F32),
                   jax.ShapeDtypeStruct((nt * nb, D_RNN), BF16),
                   jax.ShapeDtypeStruct((nt * nb, 2 * D_MODEL), F32),
                   jax.ShapeDtypeStruct((D_MODEL, IN_COLS), BF16),
                   jax.ShapeDtypeStruct((DEPTH, nb, D_RNN), F32),
                   jax.ShapeDtypeStruct((DEPTH, CONV_LRU - 1, nb, D_RNN), F32)),
        input_output_aliases=aliases,
        scratch_shapes=[pltpu.VMEM((nt * nb, D_MODEL), BF16),
                        pltpu.VMEM((IN_COLS // PROJ_PIECE, nt * nb, PROJ_PIECE), F32)],
        compiler_params=pltpu.CompilerParams(
            dimension_semantics=("arbitrary",), vmem_limit_bytes=VMEM_LIMIT_BYTES),
        name=f"sample_mixer_in_l{layer}",
    )(*args)


def _sattn_kernel(q_ref, kn_ref, vn_ref, ck_ref, cv_ref, tab_ref, attn_ref, sk_ref, sv_ref,
                  kc_s, vc_s, *, bt, nt):
    n_cache = WINDOW * N_KV
    n_new = nt * N_KV
    for u in range(SAMPLE_ATTN_UNROLL):
        kc_s[u, n_cache + n_new:SAMPLE_KEYS, :] = jnp.zeros((SAMPLE_KEYS - n_cache - n_new, HEAD_DIM), F32)
        vc_s[u, n_cache + n_new:SAMPLE_KEYS, :] = jnp.zeros((SAMPLE_KEYS - n_cache - n_new, HEAD_DIM), F32)

    def body(i, _):
        seqs = [i * SAMPLE_ATTN_UNROLL + u for u in range(SAMPLE_ATTN_UNROLL)]
        scores = []
        for u, b in enumerate(seqs):
            for c_ref, n_ref, s_ref, scr in ((ck_ref, kn_ref, sk_ref, kc_s.at[u]), (cv_ref, vn_ref, sv_ref, vc_s.at[u])):
                scr[0:n_cache, :] = c_ref[b]
                scr[n_cache:n_cache + n_new, :] = n_ref[b]
                s_ref[b, 0:n_cache - n_new, :] = c_ref[b, n_new:n_cache, :]
                s_ref[b, n_cache - n_new:n_cache, :] = n_ref[b]
            scores.append(_dot_nt(q_ref[b].astype(BF16), kc_s[u].astype(BF16)))
        probs = []
        for s in scores:
            s = s + tab_ref[...]
            p = jnp.exp(s - jnp.max(s, axis=-1, keepdims=True))
            probs.append((p.astype(BF16), 1.0 / jnp.sum(p, axis=-1, keepdims=True)))
        for u, b in enumerate(seqs):
            p, inv = probs[u]
            attn_ref[b] = _dot(p, vc_s[u].astype(BF16)) * inv
        return 0

    lax.fori_loop(0, bt // SAMPLE_ATTN_UNROLL, body, 0)


def _sample_attention(q, kn, vn, cache_k, cache_v, stab, layer, prev_states):
    nb, rows, _ = q.shape
    nt = rows // N_HEADS
    bt = SAMPLE_BATCH_TILE
    kern = functools.partial(_sattn_kernel, bt=bt, nt=nt)
    cache_spec = pl.BlockSpec((None, bt, WINDOW * N_KV, HEAD_DIM), lambda i: (layer, i, 0, 0))
    new_spec = pl.BlockSpec((bt, nt * N_KV, HEAD_DIM), lambda i: (i, 0, 0))
    q_spec = pl.BlockSpec((bt, rows, HEAD_DIM), lambda i: (i, 0, 0))
    in_specs = [q_spec, new_spec, new_spec, cache_spec, cache_spec,
                _const_spec((None, rows, SAMPLE_KEYS), (layer, 0, 0))]
    args = [q, kn, vn, cache_k, cache_v, stab]
    kern, in_specs, args, aliases = _with_state_aliases(kern, len(args), in_specs, args, prev_states, 1)
    return pl.pallas_call(
        kern,
        grid=(nb // bt,),
        in_specs=in_specs,
        out_specs=(q_spec, cache_spec, cache_spec),
        out_shape=(jax.ShapeDtypeStruct((nb, rows, HEAD_DIM), F32),
                   jax.ShapeDtypeStruct((DEPTH, nb, WINDOW * N_KV, HEAD_DIM), F32),
                   jax.ShapeDtypeStruct((DEPTH, nb, WINDOW * N_KV, HEAD_DIM), F32)),
        input_output_aliases=aliases,
        scratch_shapes=[pltpu.VMEM((SAMPLE_ATTN_UNROLL, SAMPLE_KEYS, HEAD_DIM), F32),
                        pltpu.VMEM((SAMPLE_ATTN_UNROLL, SAMPLE_KEYS, HEAD_DIM), F32)],
        compiler_params=pltpu.CompilerParams(
            dimension_semantics=("arbitrary",), vmem_limit_bytes=VMEM_LIMIT_BYTES),
        name=f"sample_attention_l{layer}",
    )(*args)


def _smix_out_kernel(xs_ref, lru_ref, attn_ref, gate_ref, vec_ref, w_lo_f32_ref, w_ao_f32_ref, w_out_f32_ref,
                     xmid_ref, w_lo_ref, w_ao_ref, w_out_ref, a_s, *, nb, nt):
    _cast_rows(w_lo_f32_ref, w_lo_ref)
    _cast_rows(w_ao_f32_ref, w_ao_ref)
    _cast_rows(w_out_f32_ref, w_out_ref)
    for t in range(nt):
        for h in range(N_HEADS):
            a_s[t * nb:(t + 1) * nb, h * HEAD_DIM:(h + 1) * HEAD_DIM] = attn_ref[:, t * N_HEADS + h, :].astype(BF16)
    merged = (gate_ref[:, 0:D_MODEL] * _dot(lru_ref[...], w_lo_ref[...])
              + gate_ref[:, D_MODEL:2 * D_MODEL] * _dot(a_s[...], w_ao_ref[...]))
    m = _dot(merged.astype(BF16), w_out_ref[...])
    mn = _rmsnorm(m, vec_ref[V_NORM_MIX_POST:V_NORM_MIX_POST + 1, :])
    for t in range(nt):
        rows = slice(t * nb, (t + 1) * nb)
        xmid_ref[rows, :] = xs_ref[:, t, :] + mn[rows]


def _sample_mixer_out(xs, lru, attn, gates, vecs, w_lo, w_ao, w_out, layer):
    nb, nt, _ = xs.shape
    kern = functools.partial(_smix_out_kernel, nb=nb, nt=nt)
    whole = lambda shape: pl.BlockSpec(shape, lambda i: (0,) * len(shape))
    return pl.pallas_call(
        kern,
        grid=(1,),
        in_specs=[
            whole(xs.shape), whole(lru.shape), whole(attn.shape), whole(gates.shape),
            _const_spec((None, VEC_ROWS, D_MODEL), (layer, 0, 0)),
            _const_spec((None, D_RNN, D_MODEL), (layer, 0, 0)),
            _const_spec((None, Q_COLS, D_MODEL), (layer, 0, 0)),
            _const_spec((None, D_MODEL, D_MODEL), (layer, 0, 0)),
        ],
        out_specs=(whole((nt * nb, D_MODEL)),) + (whole((D_MODEL, D_MODEL)),) * 3,
        out_shape=(jax.ShapeDtypeStruct((nt * nb, D_MODEL), F32),)
        + (jax.ShapeDtypeStruct((D_MODEL, D_MODEL), BF16),) * 3,
        scratch_shapes=[pltpu.VMEM((nt * nb, Q_COLS), BF16)],
        compiler_params=pltpu.CompilerParams(
            dimension_semantics=("arbitrary",), vmem_limit_bytes=VMEM_LIMIT_BYTES),
        name=f"sample_mixer_out_l{layer}",
    )(xs, lru, attn, gates, vecs, w_lo, w_ao, w_out)


def _sffn_kernel(x_ref, vec_ref, fcv_ref, prev_ref, w_f32_ref, wd_f32_ref, ys_ref, w_ref, wd_ref, fstate_ref,
                 h_s, val_s, acc_s, *, nb, nt, nck):
    c = pl.program_id(0)

    @pl.when(c == 0)
    def _():
        h_s[...] = _rmsnorm(x_ref[...], vec_ref[V_NORM_FFN_PRE:V_NORM_FFN_PRE + 1, :]).astype(BF16)
        acc_s[...] = jnp.zeros(acc_s.shape, F32)

    _cast_rows(w_f32_ref, w_ref)
    u = _dot(h_s[...], w_ref[...])
    uu = [prev_ref[:, j, :] for j in range(CONV_FF - 1)] + [u[t * nb:(t + 1) * nb] for t in range(nt)]
    ys = []
    for t in range(nt):
        y = fcv_ref[CONV_FF:CONV_FF + 1, :]
        for j in range(CONV_FF):
            y = y + fcv_ref[j:j + 1, :] * uu[t + j]
        ys.append(y)
    conv = jnp.concatenate(ys, axis=0)
    for j in range(CONV_FF - 1):
        fstate_ref[:, j, :] = uu[nt + j]

    @pl.when(c < nck)
    def _():
        val_s[c] = conv

    @pl.when(c >= nck)
    def _():
        act = (_gelu_tanh_doubled(conv) * val_s[c - nck]).astype(BF16)
        _cast_rows(wd_f32_ref, wd_ref)
        acc_s[...] += _dot(act, wd_ref[...])

    @pl.when(c == 2 * nck - 1)
    def _():
        y = x_ref[...] + _rmsnorm(acc_s[...], vec_ref[V_NORM_FFN_POST:V_NORM_FFN_POST + 1, :])
        for t in range(nt):
            ys_ref[:, t, :] = y[t * nb:(t + 1) * nb]


def _sample_ffn(xmid, fprev, vecs, fcv, w_up, w_down, layer, nb, prev_states):
    nt = xmid.shape[0] // nb
    ck = FF_CHUNK
    nck = D_FF // ck
    kern = functools.partial(_sffn_kernel, nb=nb, nt=nt, nck=nck)
    state_spec = pl.BlockSpec((None, nb, CONV_FF - 1, ck), lambda c: (layer, 0, 0, c))
    in_specs = [
        _const_spec((nt * nb, D_MODEL), (0, 0)),
        _const_spec((None, VEC_ROWS, D_MODEL), (layer, 0, 0)),
        pl.BlockSpec((None, FCV_ROWS, ck), lambda c: (layer, 0, c)),
        state_spec,
        pl.BlockSpec((None, D_MODEL, ck), lambda c: (layer, 0, c)),
        pl.BlockSpec((None, ck, D_MODEL), lambda c: (layer, jnp.maximum(c - nck, 0), 0)),
    ]
    args = [xmid, vecs, fcv, fprev, w_up, w_down]
    kern, in_specs, args, aliases = _with_state_aliases(kern, len(args), in_specs, args, prev_states, 3)
    return pl.pallas_call(
        kern,
        grid=(2 * nck,),
        in_specs=in_specs,
        out_specs=(pl.BlockSpec((nb, nt, D_MODEL), lambda c: (0, 0, 0)),
                   pl.BlockSpec((D_MODEL, ck), lambda c: (0, c)),
                   pl.BlockSpec((ck, D_MODEL), lambda c: (jnp.maximum(c - nck, 0), 0)),
                   state_spec),
        out_shape=(jax.ShapeDtypeStruct((nb, nt, D_MODEL), F32),
                   jax.ShapeDtypeStruct((D_MODEL, 2 * D_FF), BF16),
                   jax.ShapeDtypeStruct((D_FF, D_MODEL), BF16),
                   jax.ShapeDtypeStruct((DEPTH, nb, CONV_FF - 1, 2 * D_FF), F32)),
        input_output_aliases=aliases,
        scratch_shapes=[pltpu.VMEM((nt * nb, D_MODEL), BF16),
                        pltpu.VMEM((nck, nt * nb, ck), F32),
                        pltpu.VMEM((nt * nb, D_MODEL), F32)],
        compiler_params=pltpu.CompilerParams(
            dimension_semantics=("arbitrary",), vmem_limit_bytes=VMEM_LIMIT_BYTES),
        name=f"sample_ffn_l{layer}",
    )(*args)


def kernel(x_prompt, x_sample, state_lru_h, state_lru_conv, cache_win_k, cache_win_v, state_ffn_conv,
           norm_mix_pre, norm_mix_post, norm_ffn_pre, norm_ffn_post, w_in, conv_lru_w, conv_lru_b,
           lru_wr, lru_br, lru_wi, lru_bi, lru_lambda, w_lru_o, w_attn_o, w_out, attn_sink, rel_bias,
           w_up, ffn_conv_w, ffn_conv_b, w_down):
    nb, nt, _ = x_sample.shape
    bp = x_prompt.shape[0]

    vecs = jnp.concatenate(
        [norm_mix_pre, norm_mix_post, conv_lru_b, lru_br, lru_bi, lru_lambda,
         conv_lru_w.reshape(DEPTH, CONV_LRU * D_RNN), norm_ffn_pre, norm_ffn_post], axis=-1
    ).reshape(DEPTH, VEC_ROWS, D_MODEL)
    fcv = jnp.concatenate([ffn_conv_w.reshape(DEPTH, CONV_FF * 2 * D_FF), ffn_conv_b], axis=-1
                          ).reshape(DEPTH, FCV_ROWS, 2 * D_FF)
    fcv = fcv * jnp.where(jnp.arange(2 * D_FF) < D_FF, 0.5, 1.0).astype(F32)
    wri_b = jnp.concatenate([lru_wr, lru_wi], axis=-1).astype(BF16)

    ptab, stab = _bias_tables(rel_bias.T, attn_sink)

    yp, xs = x_prompt, x_sample
    p_mix = p_ffn = s_mix = s_att = s_ffn = None
    cache_rows = lambda c: c.reshape(DEPTH, nb, WINDOW * N_KV, HEAD_DIM)
    for l in range(DEPTH):
        q, kn, vn, lru, gates, w_in_b, *s_mix = _sample_mixer_in(
            xs, state_lru_h, jnp.swapaxes(state_lru_conv, 1, 2), vecs, w_in, wri_b, l, s_mix)
        attn, *s_att = _sample_attention(
            q, kn, vn, cache_rows(cache_win_k), cache_rows(cache_win_v),
            stab, l, s_att)
        xmid, w_lo_b, w_ao_b, w_out_b = _sample_mixer_out(
            xs, lru, attn, gates, vecs, w_lru_o, w_attn_o, w_out, l)
        xs, w_up_b, w_down_b, *s_ffn = _sample_ffn(xmid, state_ffn_conv, vecs, fcv, w_up, w_down, l, nb, s_ffn)

        yp, *p_mix = _prompt_mixer(yp, vecs, w_in_b, wri_b, w_lo_b, w_ao_b, w_out_b, ptab, attn_sink, l, p_mix,
                                   permute_in=(l == 0))
        yp, *p_ffn = _prompt_ffn(yp, vecs, fcv, w_up_b, w_down_b, l, p_ffn, permute_out=(l == DEPTH - 1))

    p_h, p_c, p_k, p_v = p_mix
    s_h, s_c = s_mix
    s_k, s_v = s_att
    kv_shape = (DEPTH, bp, WINDOW, N_KV, HEAD_DIM)
    return (yp.reshape(x_prompt.shape), xs, p_h, p_c, p_k.reshape(kv_shape), p_v.reshape(kv_shape), p_ffn[0],
            s_h, jnp.swapaxes(s_c, 1, 2), s_k.reshape(cache_win_k.shape), s_v.reshape(cache_win_v.shape), s_ffn[0])
```
